```python
import math
import jax, jax.numpy as jnp
from jax import lax
import numpy as np

D_MODEL = 2048
BATCH = 32
SEQ = 256
DEPTH = 4
DEC_BATCH = 8
DEC_SEQ = 4096
PAST_LEN = 256

GRID_W = 64
N_EVEN = (DEPTH + 1) // 2
N_ODD = DEPTH // 2
S5_GROUP_CH = 16
S5_WIDTH = D_MODEL // 4
S5_GROUPS = S5_WIDTH // S5_GROUP_CH
S5_STATE = 64
HEAD_DIM = 128
NA_HEADS = D_MODEL // (2 * HEAD_DIM)
NA_WIDTH = NA_HEADS * HEAD_DIM
NA_WIN_R = 8
NA_WIN_C = 16
NA_QC = 16
NA_KC = 2 * NA_QC
AB_IN = S5_WIDTH + 3 * NA_WIDTH
AB_OUT = S5_WIDTH + NA_WIDTH
D_FF = 4 * D_MODEL
Q_BLOCK = 128
NORM_EPS = 1e-6
NEG_INF = -1e30

kernel_name = "hybrid_s5_natten_shortconv_diffusion_step"


def rms_norm(x, g):
    x32 = x.astype(jnp.float32)
    y = x32 * lax.rsqrt(jnp.mean(x32 * x32, axis=-1, keepdims=True) + NORM_EPS)
    return (y * g.astype(jnp.float32)).astype(x.dtype)


def ada_params(cvec, w, b):
    m = jax.nn.silu(cvec) @ w + b
    m = m.reshape(m.shape[0], 1, 6, D_MODEL)
    return tuple(m[:, :, i] for i in range(6))


def modulate(x, g, shift, scale):
    return rms_norm(x, g) * (1 + scale) + shift


def _complex_affine_combine(e1, e2):
    a1r, a1i, b1r, b1i = e1
    a2r, a2i, b2r, b2i = e2
    return (a2r * a1r - a2i * a1i,
            a2r * a1i + a2i * a1r,
            a2r * b1r - a2i * b1i + b2r,
            a2r * b1i + a2i * b1r + b2i)


def s5_scan(u, h0, lam_re, lam_im, log_dt, b_re, b_im, c_re, c_im):
    seq_len = u.shape[1]
    dt = jnp.exp(log_dt)[:, None]
    ar, ai = lam_re * dt, lam_im * dt
    decay = jnp.exp(ar)
    abar_re, abar_im = decay * jnp.cos(ai), decay * jnp.sin(ai)
    den = lam_re * lam_re + lam_im * lam_im
    nr = abar_re - 1.0
    f_re = (nr * lam_re + abar_im * lam_im) / den
    f_im = (abar_im * lam_re - nr * lam_im) / den
    bb_re = f_re[..., None] * b_re - f_im[..., None] * b_im
    bb_im = f_re[..., None] * b_im + f_im[..., None] * b_re
    bu_re = jnp.einsum('blgn,gpn->blgp', u, bb_re)
    bu_im = jnp.einsum('blgn,gpn->blgp', u, bb_im)
    a_re = jnp.broadcast_to(abar_re, (1, seq_len) + abar_re.shape)
    a_im = jnp.broadcast_to(abar_im, (1, seq_len) + abar_im.shape)
    _, _, x_re, x_im = lax.associative_scan(_complex_affine_combine, (a_re, a_im, bu_re, bu_im), axis=1)
    if h0 is not None:
        steps = jnp.arange(1, seq_len + 1, dtype=jnp.float32)[:, None, None]
        pw_mag = jnp.exp(steps * ar)
        pw_re, pw_im = pw_mag * jnp.cos(steps * ai), pw_mag * jnp.sin(steps * ai)
        h_re, h_im = h0[0][:, None], h0[1][:, None]
        x_re = x_re + pw_re * h_re - pw_im * h_im
        x_im = x_im + pw_re * h_im + pw_im * h_re
    y = jnp.einsum('blgp,gnp->blgn', x_re, c_re) - jnp.einsum('blgp,gnp->blgn', x_im, c_im)
    return y, x_re, x_im


def s5_mixer(u, h0_re, h0_im, lam_re, lam_im, log_dt, b_re, b_im, c_re, c_im, d, glu_w, glu_b):
    f32 = jnp.float32
    bsz, seq_len, _ = u.shape
    u32 = u.astype(f32).reshape(bsz, seq_len, S5_GROUPS, S5_GROUP_CH)
    y = d.astype(f32).reshape(S5_GROUPS, S5_GROUP_CH) * u32
    last_re, last_im = [], []
    for direction in range(2):
        ud = u32 if direction == 0 else u32[:, ::-1]
        h0 = None if h0_re is None else (h0_re[:, direction].astype(f32), h0_im[:, direction].astype(f32))
        yd, x_re, x_im = s5_scan(ud, h0, lam_re[direction].astype(f32), lam_im[direction].astype(f32),
                                 log_dt[direction].astype(f32), b_re[direction].astype(f32),
                                 b_im[direction].astype(f32), c_re[direction].astype(f32),
                                 c_im[direction].astype(f32))
        y = y + (yd if direction == 0 else yd[:, ::-1])
        if h0_re is None:
            last_re.append(x_re[:, -1])
            last_im.append(x_im[:, -1])
    g = jax.nn.gelu(y.reshape(bsz, seq_len, S5_WIDTH))
    out = (g * jax.nn.sigmoid(g @ glu_w.astype(f32) + glu_b.astype(f32))).astype(u.dtype)
    if h0_re is None:
        return out, jnp.stack(last_re, axis=1), jnp.stack(last_im, axis=1)
    return out


def ab_projection(h, w_in, qn_g, kn_g):
    bsz, seq_len, _ = h.shape
    proj = h @ w_in
    u = proj[..., :S5_WIDTH]
    q, k, v = jnp.split(proj[..., S5_WIDTH:], 3, axis=-1)
    q = rms_norm(q.reshape(bsz, seq_len, NA_HEADS, HEAD_DIM), qn_g)
    k = rms_norm(k.reshape(bsz, seq_len, NA_HEADS, HEAD_DIM), kn_g)
    v = v.reshape(bsz, seq_len, NA_HEADS, HEAD_DIM)
    return u, q, k, v


def context_attention(q, k, v):
    bsz, ctx_len, n_heads, hd = q.shape
    scale = hd ** -0.5
    qb = q.reshape(bsz, ctx_len // Q_BLOCK, Q_BLOCK, n_heads, hd).transpose(1, 0, 2, 3, 4)

    def block(qi):
        s = jnp.einsum('bqhd,bkhd->bhqk', qi, k, preferred_element_type=jnp.float32) * scale
        p = jax.nn.softmax(s, axis=-1).astype(v.dtype)
        return jnp.einsum('bhqk,bkhd->bqhd', p, v)

    o = lax.map(block, qb)
    return o.transpose(1, 0, 2, 3, 4).reshape(bsz, ctx_len, n_heads * hd)


def neighbourhood_attention(q, k, v, k_ctx, v_ctx, rpb):
    f32 = jnp.float32
    bsz, n_tok, n_heads, hd = q.shape
    rows = n_tok // GRID_W
    wr = min(NA_WIN_R, rows)
    n_cb = GRID_W // NA_QC
    scale = hd ** -0.5
    qg = q.reshape(bsz, rows, GRID_W, n_heads, hd)
    kg = k.reshape(bsz, rows, GRID_W, n_heads, hd)
    vg = v.reshape(bsz, rows, GRID_W, n_heads, hd)
    q_cols = np.arange(GRID_W).reshape(n_cb, NA_QC)
    c0 = np.clip(q_cols - NA_WIN_C // 2, 0, GRID_W - NA_WIN_C)
    kc0 = np.clip(np.arange(n_cb) * NA_QC - NA_WIN_C // 2, 0, GRID_W - NA_KC)
    k_cols = kc0[:, None] + np.arange(NA_KC)[None, :]
    col_ok = (k_cols[:, None, :] >= c0[:, :, None]) & (k_cols[:, None, :] < c0[:, :, None] + NA_WIN_C)
    n_loc = wr * NA_KC
    mask = np.broadcast_to(col_ok[:, None, :, None, :], (n_cb, 1, NA_QC, wr, NA_KC)).reshape(n_cb, 1, NA_QC, n_loc)
    dc_idx = np.clip(k_cols[:, None, :] - q_cols[:, :, None] + NA_WIN_C - 1, 0, 2 * NA_WIN_C - 2)

    def row_block(r):
        r0 = jnp.clip(r - wr // 2, 0, rows - wr)
        k_blk = lax.dynamic_slice_in_dim(kg, r0, wr, axis=1)[:, :, k_cols]
        v_blk = lax.dynamic_slice_in_dim(vg, r0, wr, axis=1)[:, :, k_cols]
        k_blk = k_blk.transpose(0, 2, 1, 3, 4, 5).reshape(bsz, n_cb, n_loc, n_heads, hd)
        v_blk = v_blk.transpose(0, 2, 1, 3, 4, 5).reshape(bsz, n_cb, n_loc, n_heads, hd)
        q_row = lax.dynamic_index_in_dim(qg, r, axis=1, keepdims=False).reshape(bsz, n_cb, NA_QC, n_heads, hd)
        dr_idx = r0 + jnp.arange(wr) - r + NA_WIN_R - 1
        bias = rpb[:, dr_idx[:, None, None, None], dc_idx[None]]
        bias = bias.transpose(2, 0, 3, 1, 4).reshape(n_cb, n_heads, NA_QC, n_loc).astype(f32)
        s_loc = jnp.einsum('bjqhd,bjkhd->bjhqk', q_row, k_blk, preferred_element_type=f32) * scale + bias
        s_loc = jnp.where(mask, s_loc, NEG_INF)
        s_ctx = jnp.einsum('bjqhd,bchd->bjhqc', q_row, k_ctx, preferred_element_type=f32) * scale
        p = jax.nn.softmax(jnp.concatenate([s_loc, s_ctx], axis=-1), axis=-1).astype(v.dtype)
        o = (jnp.einsum('bjhqk,bjkhd->bjqhd', p[..., :n_loc], v_blk)
             + jnp.einsum('bjhqc,bchd->bjqhd', p[..., n_loc:], v_ctx))
        return o.reshape(bsz, GRID_W, n_heads, hd)

    out = lax.map(row_block, jnp.arange(rows))
    return out.transpose(1, 0, 2, 3, 4).reshape(bsz, n_tok, n_heads * hd)


def ab_mixer_context(h, w_in, w_out, qn_g, kn_g, s5p):
    u, q, k, v = ab_projection(h, w_in, qn_g, kn_g)
    s5_out, s_re, s_im = s5_mixer(u, None, None, *s5p)
    attn = context_attention(q, k, v)
    out = jnp.concatenate([s5_out, attn], axis=-1) @ w_out
    return out, k, v, s_re, s_im


def ab_mixer_latent(h, k_ctx, v_ctx, h0_re, h0_im, w_in, w_out, qn_g, kn_g, rpb, s5p):
    u, q, k, v = ab_projection(h, w_in, qn_g, kn_g)
    s5_out = s5_mixer(u, h0_re, h0_im, *s5p)
    attn = neighbourhood_attention(q, k, v, k_ctx, v_ctx, rpb)
    return jnp.concatenate([s5_out, attn], axis=-1) @ w_out


def short_conv_mixer(h, w_in, conv_w, conv_b, w_out):
    gate_b, gate_c, xt = jnp.split(h @ w_in, 3, axis=-1)
    z = jnp.pad(gate_c * xt, ((0, 0), (1, 1), (0, 0)))
    conv = z[:, :-2] * conv_w[0] + z[:, 1:-1] * conv_w[1] + z[:, 2:] * conv_w[2] + conv_b
    return (gate_b * conv) @ w_out


def sq_relu_mlp(h, w1, w2):
    a = jax.nn.relu(h @ w1)
    return (a * a) @ w2


def setup_inputs(seed: int = 0) -> dict:
    key = jax.random.key(seed)
    ks = iter(jax.random.split(key, 40))
    f32 = jnp.float32

    def nrm(shape, scale):
        return jax.random.normal(next(ks), shape, f32) * scale

    inp = {}
    inp['x_prompt'] = nrm((BATCH, SEQ, D_MODEL), 1.0)
    inp['x_sample'] = nrm((DEC_BATCH, DEC_SEQ, D_MODEL), 1.0)
    inp['c'] = nrm((DEC_BATCH, D_MODEL), 1.0)
    inp['cache_k'] = nrm((DEC_BATCH, N_EVEN, PAST_LEN, NA_HEADS, HEAD_DIM), 1.0)
    inp['cache_v'] = nrm((DEC_BATCH, N_EVEN, PAST_LEN, NA_HEADS, HEAD_DIM), 1.0)
    inp['state_ssm_re'] = nrm((DEC_BATCH, N_EVEN, 2, S5_GROUPS, S5_STATE), 0.3)
    inp['state_ssm_im'] = nrm((DEC_BATCH, N_EVEN, 2, S5_GROUPS, S5_STATE), 0.3)
    inp['c_ctx'] = nrm((D_MODEL,), 1.0)
    inp['ada_w'] = nrm((DEPTH, D_MODEL, 6 * D_MODEL), 0.5 * D_MODEL ** -0.5)
    inp['ada_b'] = nrm((DEPTH, 6 * D_MODEL), 0.01)
    inp['norm1_g'] = 1.0 + nrm((DEPTH, D_MODEL), 0.02)
    inp['norm2_g'] = 1.0 + nrm((DEPTH, D_MODEL), 0.02)
    inp['ab_w_in'] = nrm((N_EVEN, D_MODEL, AB_IN), D_MODEL ** -0.5)
    inp['ab_w_out'] = nrm((N_EVEN, AB_OUT, D_MODEL), AB_OUT ** -0.5)
    inp['s5_lam_re'] = -0.5 + nrm((N_EVEN, 2, S5_GROUPS, S5_STATE), 0.01)
    inp['s5_lam_im'] = jnp.pi * jnp.arange(S5_STATE, dtype=f32) + nrm((N_EVEN, 2, S5_GROUPS, S5_STATE), 0.01)
    inp['s5_log_dt'] = jax.random.uniform(next(ks), (N_EVEN, 2, S5_GROUPS), f32, math.log(1e-3), math.log(1e-1))
    inp['s5_b_re'] = nrm((N_EVEN, 2, S5_GROUPS, S5_STATE, S5_GROUP_CH), (2 * S5_GROUP_CH) ** -0.5)
    inp['s5_b_im'] = nrm((N_EVEN, 2, S5_GROUPS, S5_STATE, S5_GROUP_CH), (2 * S5_GROUP_CH) ** -0.5)
    inp['s5_c_re'] = nrm((N_EVEN, 2, S5_GROUPS, S5_GROUP_CH, S5_STATE), S5_STATE ** -0.5)
    inp['s5_c_im'] = nrm((N_EVEN, 2, S5_GROUPS, S5_GROUP_CH, S5_STATE), S5_STATE ** -0.5)
    inp['s5_d'] = nrm((N_EVEN, S5_WIDTH), 1.0)
    inp['s5_glu_w'] = nrm((N_EVEN, S5_WIDTH, S5_WIDTH), S5_WIDTH ** -0.5)
    inp['s5_glu_b'] = nrm((N_EVEN, S5_WIDTH), 0.01)
    inp['q_norm_g'] = 1.0 + nrm((N_EVEN, HEAD_DIM), 0.02)
    inp['k_norm_g'] = 1.0 + nrm((N_EVEN, HEAD_DIM), 0.02)
    inp['na_rpb'] = nrm((N_EVEN, NA_HEADS, 2 * NA_WIN_R - 1, 2 * NA_WIN_C - 1), 0.1)
    inp['conv_w_in'] = nrm((N_ODD, D_MODEL, 3 * D_MODEL), D_MODEL ** -0.5)
    inp['conv_w'] = nrm((N_ODD, 3, D_MODEL), 3 ** -0.5)
    inp['conv_b'] = nrm((N_ODD, D_MODEL), 0.01)
    inp['conv_w_out'] = nrm((N_ODD, D_MODEL, D_MODEL), D_MODEL ** -0.5)
    inp['mlp_w1'] = nrm((DEPTH, D_MODEL, D_FF), D_MODEL ** -0.5)
    inp['mlp_w2'] = nrm((DEPTH, D_FF, D_MODEL), D_FF ** -0.5)
    return inp


def reference(x_prompt, x_sample, c, cache_k, cache_v, state_ssm_re, state_ssm_im, c_ctx,
              ada_w, ada_b, norm1_g, norm2_g, ab_w_in, ab_w_out,
              s5_lam_re, s5_lam_im, s5_log_dt, s5_b_re, s5_b_im, s5_c_re, s5_c_im, s5_d,
              s5_glu_w, s5_glu_b, q_norm_g, k_norm_g, na_rpb,
              conv_w_in, conv_w, conv_b, conv_w_out, mlp_w1, mlp_w2):
    xp, xs = x_prompt, x_sample
    ks, vs, srs, sis = [], [], [], []
    for layer in range(DEPTH):
        sh1p, sc1p, g1p, sh2p, sc2p, g2p = ada_params(c_ctx[None], ada_w[layer], ada_b[layer])
        sh1s, sc1s, g1s, sh2s, sc2s, g2s = ada_params(c, ada_w[layer], ada_b[layer])
        hp = modulate(xp, norm1_g[layer], sh1p, sc1p)
        hs = modulate(xs, norm1_g[layer], sh1s, sc1s)
        if layer % 2 == 0:
            e = layer // 2
            s5p = (s5_lam_re[e], s5_lam_im[e], s5_log_dt[e], s5_b_re[e], s5_b_im[e],
                   s5_c_re[e], s5_c_im[e], s5_d[e], s5_glu_w[e], s5_glu_b[e])
            yp, kp, vp, srp, sip = ab_mixer_context(hp, ab_w_in[e], ab_w_out[e], q_norm_g[e], k_norm_g[e], s5p)
            ys = ab_mixer_latent(hs, cache_k[:, e], cache_v[:, e], state_ssm_re[:, e], state_ssm_im[:, e],
                                 ab_w_in[e], ab_w_out[e], q_norm_g[e], k_norm_g[e], na_rpb[e], s5p)
            ks.append(kp)
            vs.append(vp)
            srs.append(srp)
            sis.append(sip)
        else:
            o = layer // 2
            yp = short_conv_mixer(hp, conv_w_in[o], conv_w[o], conv_b[o], conv_w_out[o])
            ys = short_conv_mixer(hs, conv_w_in[o], conv_w[o], conv_b[o], conv_w_out[o])
        xp = xp + g1p * yp
        xs = xs + g1s * ys
        xp = xp + g2p * sq_relu_mlp(modulate(xp, norm2_g[layer], sh2p, sc2p), mlp_w1[layer], mlp_w2[layer])
        xs = xs + g2s * sq_relu_mlp(modulate(xs, norm2_g[layer], sh2s, sc2s), mlp_w1[layer], mlp_w2[layer])
    new_cache_k = jnp.stack(ks, axis=1)
    new_cache_v = jnp.stack(vs, axis=1)
    new_state_ssm_re = jnp.stack(srs, axis=1)
    new_state_ssm_im = jnp.stack(sis, axis=1)
    return (xp, xs, new_cache_k, new_cache_v, new_state_ssm_re, new_state_ssm_im)
```

```python
import functools
import math

import numpy as np
import jax
import jax.numpy as jnp
from jax import lax
from jax.experimental import pallas as pl
from jax.experimental.pallas import tpu as pltpu

F32 = jnp.float32
BF16 = jnp.bfloat16

NORM_EPS = 1e-6
NEG_INF = -1e30

S5_GROUP_CH = 16
HEAD_DIM = 128
GRID_W = 64
NA_WIN_R = 8
NA_WIN_C = 16
S5_CHUNK = 16
NA_QROWS = 4
NA_KROWS = NA_QROWS + NA_WIN_R

VMEM_LIMIT_BYTES = 58 * 1024 * 1024


def _cparams(n_axes):
    return pltpu.CompilerParams(dimension_semantics=("arbitrary",) * n_axes,
                                vmem_limit_bytes=VMEM_LIMIT_BYTES)


def _const_spec(shape, index_map):
    return pl.BlockSpec(shape, index_map, pipeline_mode=pl.Buffered(1))


def _dot(a, b):
    return jnp.dot(a, b, preferred_element_type=F32)


def _dot_nt(a, b):
    return lax.dot_general(a, b, (((1,), (1,)), ((), ())), preferred_element_type=F32)


def _dot_split(a, b):
    a_hi = a.astype(BF16)
    a_lo = (a - a_hi.astype(F32)).astype(BF16)
    b_hi = b.astype(BF16)
    b_lo = (b - b_hi.astype(F32)).astype(BF16)
    return _dot(a_hi, b_hi) + _dot(a_hi, b_lo) + _dot(a_lo, b_hi)


def _modulate(x, g, shift, scale):
    ms = jnp.mean(x * x, axis=-1, keepdims=True)
    y = x * lax.rsqrt(ms + NORM_EPS) * g
    return y * (1.0 + scale) + shift


def _modulate_rows(x_ref, h_ref, g, shift, scale, rc, copy_ref=None):
    tm = x_ref.shape[0]

    def body(r, carry):
        rows = pl.ds(pl.multiple_of(r * rc, rc), rc)
        x = x_ref[rows, :]
        h_ref[rows, :] = _modulate(x, g, shift, scale).astype(h_ref.dtype)
        if copy_ref is not None:
            copy_ref[rows, :] = x
        return carry

    lax.fori_loop(0, tm // rc, body, 0)


def _ada_kernel(c_ref, w_ref, b_ref, o_ref, *, nc):
    cv = c_ref[...]
    sc = (cv * jax.nn.sigmoid(cv)).astype(BF16)
    tn = w_ref.shape[1]
    for n0 in range(0, tn, nc):
        w = w_ref[:, n0:n0 + nc].astype(BF16)
        o_ref[:, n0:n0 + nc] = _dot(sc, w) + b_ref[:, n0:n0 + nc]


def _ada_call(cvec, ada_w, ada_b):
    depth, d, n6 = ada_w.shape
    rows = cvec.shape[0]
    tn = 1536 if n6 % 1536 == 0 else n6
    nc = 512 if tn % 512 == 0 else tn
    return pl.pallas_call(
        functools.partial(_ada_kernel, nc=nc),
        grid=(depth, n6 // tn),
        in_specs=[
            pl.BlockSpec((rows, d), lambda l, j: (0, 0)),
            pl.BlockSpec((None, d, tn), lambda l, j: (l, 0, j)),
            pl.BlockSpec((None, 1, tn), lambda l, j: (l, 0, j)),
        ],
        out_specs=pl.BlockSpec((None, rows, tn), lambda l, j: (l, 0, j)),
        out_shape=jax.ShapeDtypeStruct((depth, rows, n6), F32),
        compiler_params=_cparams(2),
        name="ada_params",
    )(cvec, ada_w, ada_b.reshape(depth, 1, n6))


def _inproj_kernel(x_ref, mod_ref, g_ref, w_ref, qg_ref, kg_ref,
                   u_ref, q_ref, k_ref, v_ref, h_s, *, s5w, naw, rc):
    _modulate_rows(x_ref, h_s, g_ref[...], mod_ref[0:1, :], mod_ref[1:2, :], rc)
    h = h_s[...]
    u_ref[...] = _dot(h, w_ref[:, 0:s5w]).astype(u_ref.dtype)
    hd = HEAD_DIM
    cw = min(4 * hd, naw)
    qg = qg_ref[...]
    kg = kg_ref[...]

    def head_norm(t, gain):
        return t * lax.rsqrt(jnp.mean(t * t, axis=-1, keepdims=True) + NORM_EPS) * gain

    for c0 in range(0, naw, cw):
        qc = _dot(h, w_ref[:, s5w + c0:s5w + c0 + cw])
        for j in range(cw // hd):
            q_ref[:, c0 + j * hd:c0 + (j + 1) * hd] = head_norm(
                qc[:, j * hd:(j + 1) * hd], qg).astype(q_ref.dtype)
        kc = _dot(h, w_ref[:, s5w + naw + c0:s5w + naw + c0 + cw])
        for j in range(cw // hd):
            k_ref[:, c0 + j * hd:c0 + (j + 1) * hd] = head_norm(
                kc[:, j * hd:(j + 1) * hd], kg).astype(k_ref.dtype)
        v_ref[:, c0:c0 + cw] = _dot(
            h, w_ref[:, s5w + 2 * naw + c0:s5w + 2 * naw + c0 + cw]).astype(v_ref.dtype)


def _inproj_call(x, modt, g, w_in, qg, kg, *, tm, s5w, naw):
    m, d = x.shape
    n_in = w_in.shape[1]
    rc = min(128, tm)
    row = lambda i: (i, 0)
    return pl.pallas_call(
        functools.partial(_inproj_kernel, s5w=s5w, naw=naw, rc=rc),
        grid=(m // tm,),
        in_specs=[
            pl.BlockSpec((tm, d), row),
            pl.BlockSpec((None, 8, d), lambda i: (i, 0, 0)),
            _const_spec((1, d), lambda i: (0, 0)),
            _const_spec((d, n_in), lambda i: (0, 0)),
            _const_spec((1, HEAD_DIM), lambda i: (0, 0)),
            _const_spec((1, HEAD_DIM), lambda i: (0, 0)),
        ],
        out_specs=[
            pl.BlockSpec((tm, s5w), row),
            pl.BlockSpec((tm, naw), row),
            pl.BlockSpec((tm, naw), row),
            pl.BlockSpec((tm, naw), row),
        ],
        out_shape=[
            jax.ShapeDtypeStruct((m, s5w), BF16),
            jax.ShapeDtypeStruct((m, naw), BF16),
            jax.ShapeDtypeStruct((m, naw), F32),
            jax.ShapeDtypeStruct((m, naw), F32),
        ],
        scratch_shapes=[pltpu.VMEM((tm, d), BF16)],
        compiler_params=_cparams(1),
        name="ab_in_proj",
    )(x, modt, g, w_in, qg, kg)


def _s5prep_kernel(row_ref, col_ref, btr_ref, bti_ref, cer_ref, cei_ref, d_ref,
                   wu_ref, wy_ref, pw_ref, *, n_state, n_pw):
    t_chunk = S5_CHUNK
    width = t_chunk * S5_GROUP_CH
    p = n_state
    lg_ch = int(math.log2(S5_GROUP_CH))
    r_i = lax.broadcasted_iota(jnp.int32, (width, 1), 0)
    c_i = lax.broadcasted_iota(jnp.int32, (1, width), 1)
    s_i = r_i >> lg_ch
    t_i = c_i >> lg_ch
    s_f = s_i.astype(F32)
    t_f = t_i.astype(F32)

    def cmul(ar, ai, br, bi):
        return ar * br - ai * bi, ar * bi + ai * br

    acc = jnp.zeros((width, width), F32)
    for d in range(2):
        lr = row_ref[d, 0:1, 0:p]
        li = row_ref[d, 1:2, 0:p]
        dt = jnp.exp(row_ref[d, 2:3, 0:p])
        ar = lr * dt
        ai = li * dt
        er = jnp.exp(ar)
        abr = er * jnp.cos(ai)
        abi = er * jnp.sin(ai)
        den = lr * lr + li * li
        nr = abr - 1.0
        f_re = (nr * lr + abi * li) / den
        f_im = (abi * lr - nr * li) / den
        bb_re, bb_im = cmul(f_re, f_im, btr_ref[d], bti_ref[d])

        lrc = col_ref[d, :, 0:1]
        lic = col_ref[d, :, 1:2]
        dtc = jnp.exp(col_ref[d, :, 2:3])
        arc = lrc * dtc
        aic = lic * dtc
        ce_re = cer_ref[d]
        ce_im = cei_ref[d]

        def pow_row(k):
            mag = jnp.exp(k * ar)
            ph = k * ai
            return mag * jnp.cos(ph), mag * jnp.sin(ph)

        def pow_col(k):
            mag = jnp.exp(arc * k)
            ph = aic * k
            return mag * jnp.cos(ph), mag * jnp.sin(ph)

        terms = [(jnp.zeros_like(s_f), jnp.zeros_like(t_f), s_i == t_i)]
        blk = 2
        while blk <= t_chunk:
            lb = int(math.log2(blk))
            half = blk // 2
            mid_s = ((s_i >> lb) << lb) + half
            mid_t = ((t_i >> lb) << lb) + half
            same = (s_i >> lb) == (t_i >> lb)
            if d == 0:
                e_s, ok_s = mid_s - s_i, s_i < mid_s
                e_t, ok_t = t_i - mid_t, t_i >= mid_t
            else:
                e_s, ok_s = s_i - mid_s, s_i >= mid_s
                e_t, ok_t = mid_t - t_i, t_i < mid_t
            terms.append((jnp.maximum(e_s, 0).astype(F32), jnp.maximum(e_t, 0).astype(F32),
                          same & ok_s & ok_t))
            blk *= 2
        for e_s, e_t, mask in terms:
            l_re, l_im = cmul(*pow_row(e_s), bb_re, bb_im)
            r_re, r_im = cmul(*pow_col(e_t), ce_re, ce_im)
            term = _dot_split(l_re, r_re) - _dot_split(l_im, r_im)
            acc = acc + jnp.where(mask, term, 0.0)

        if d == 0:
            ws_re, ws_im = cmul(*pow_row(float(t_chunk - 1) - s_f), bb_re, bb_im)
            ca_re, ca_im = cmul(*pow_col(t_f + 1.0), ce_re, ce_im)
        else:
            ws_re, ws_im = cmul(*pow_row(s_f), bb_re, bb_im)
            ca_re, ca_im = cmul(*pow_col(float(t_chunk) - t_f), ce_re, ce_im)
        base = width + d * 2 * p
        wu_ref[:, base:base + p] = ws_re.astype(wu_ref.dtype)
        wu_ref[:, base + p:base + 2 * p] = ws_im.astype(wu_ref.dtype)
        wy_ref[d * 2 * p:d * 2 * p + p, :] = ca_re.astype(wy_ref.dtype)
        wy_ref[d * 2 * p + p:(d + 1) * 2 * p, :] = (-ca_im).astype(wy_ref.dtype)

        lr2 = row_ref[d, 0:1, :]
        li2 = row_ref[d, 1:2, :]
        dt2 = jnp.exp(row_ref[d, 2:3, :])
        k16 = float(t_chunk)
        mag = jnp.exp(k16 * (lr2 * dt2))
        pr = mag * jnp.cos(k16 * (li2 * dt2))
        pi = mag * jnp.sin(k16 * (li2 * dt2))
        lane = lax.broadcasted_iota(jnp.int32, pr.shape, 1)
        sign = jnp.where(lane < p, -1.0, 1.0)
        for i in range(n_pw):
            pw_ref[i * 4 + d * 2:i * 4 + d * 2 + 1, :] = pr
            pw_ref[i * 4 + d * 2 + 1:i * 4 + d * 2 + 2, :] = pi * sign
            pr, pi = pr * pr - pi * pi, 2.0 * pr * pi

    wu_ref[:, 0:width] = (acc + jnp.where(r_i == c_i, d_ref[...], 0.0)).astype(wu_ref.dtype)


def _s5prep_call(lam_re, lam_im, log_dt, b_re, b_im, c_re, c_im, d_skip, n_pw):
    n_e, _, n_g, p = lam_re.shape
    n_ch = S5_GROUP_CH
    width = S5_CHUNK * n_ch
    assert 2 * p == 128, "state rows are packed as [re | im] in one 128-lane tile"
    dtb = jnp.broadcast_to(log_dt[..., None], lam_re.shape)
    zeros = jnp.zeros_like(lam_re)
    rowp = jnp.stack([lam_re, lam_im, dtb] + [zeros] * 5, axis=-2)
    rowp = jnp.concatenate([rowp, rowp], axis=-1)
    colp = jnp.stack([lam_re, lam_im, dtb] + [zeros] * 5, axis=-1)
    bt = lambda b: jnp.tile(jnp.swapaxes(b, -1, -2), (1, 1, 1, S5_CHUNK, 1))
    ce = lambda c: jnp.tile(jnp.swapaxes(c, -1, -2), (1, 1, 1, 1, S5_CHUNK))
    d_row = jnp.tile(d_skip.reshape(n_e, n_g, 1, n_ch), (1, 1, 1, S5_CHUNK))

    def dspec(shape):
        return pl.BlockSpec((None, 2, None) + shape, lambda e, g: (e, 0, g, 0, 0))

    def ospec(shape):
        return pl.BlockSpec((None, None) + shape, lambda e, g: (e, g, 0, 0))

    return pl.pallas_call(
        functools.partial(_s5prep_kernel, n_state=p, n_pw=n_pw),
        grid=(n_e, n_g),
        in_specs=[dspec((8, 2 * p)), dspec((p, 8)), dspec((width, p)), dspec((width, p)),
                  dspec((p, width)), dspec((p, width)), ospec((1, width))],
        out_specs=[ospec((width, 2 * width)), ospec((width, width)), ospec((4 * n_pw, 2 * p))],
        out_shape=[
            jax.ShapeDtypeStruct((n_e, n_g, width, 2 * width), BF16),
            jax.ShapeDtypeStruct((n_e, n_g, width, width), BF16),
            jax.ShapeDtypeStruct((n_e, n_g, 4 * n_pw, 2 * p), F32),
        ],
        compiler_params=_cparams(2),
        name="s5_prep",
    )(rowp, colp, bt(b_re), bt(b_im), ce(c_re), ce(c_im), d_row)


def _gelu_tanh(y):
    return 0.5 * y * (1.0 + jnp.tanh(0.7978845608028654 * (y + 0.044715 * (y * y * y))))


def _s5_kernel(u_ref, wu_ref, wy_ref, pw_ref, h0_ref, g_ref, zp_ref, ph_s, *, rp, cp, cs, n_s):
    width = wy_ref.shape[0]
    half = width // 2
    wu = wu_ref[...]
    wy = wy_ref[...]

    def crot(a1, a2, x):
        return a1 * x + a2 * pltpu.roll(x, half // 2, 1)

    for d in range(2):
        h0d = h0_ref[:, d * half:(d + 1) * half]
        ph_s[:, d * half:(d + 1) * half] = crot(pw_ref[d * 2:d * 2 + 1, :], pw_ref[d * 2 + 1:d * 2 + 2, :], h0d)

    def process(u, cseq, h0, ph):
        n = u.shape[0]
        ys = _dot(u, wu)
        pos = lax.broadcasted_iota(jnp.int32, (n, 1), 0) & (cseq - 1)
        zs, xs = [], []
        for d in range(2):
            z = ys[:, width + d * half:width + (d + 1) * half]
            edge = (pos == 0) if d == 0 else (pos == cseq - 1)
            if h0 is not None:
                z = z + jnp.where(edge, ph[:, d * half:(d + 1) * half], 0.0)
            k = 1
            i = 0
            while k < cseq:
                a1 = pw_ref[i * 4 + d * 2:i * 4 + d * 2 + 1, :]
                a2 = pw_ref[i * 4 + d * 2 + 1:i * 4 + d * 2 + 2, :]
                if d == 0:
                    sh = pltpu.roll(z, k, 0)
                    valid = pos >= k
                else:
                    sh = pltpu.roll(z, n - k, 0)
                    valid = pos < cseq - k
                z = z + jnp.where(valid, crot(a1, a2, sh), 0.0)
                k *= 2
                i += 1
            x = pltpu.roll(z, 1, 0) if d == 0 else pltpu.roll(z, n - 1, 0)
            x_edge = 0.0 if h0 is None else h0[:, d * half:(d + 1) * half]
            xs.append(jnp.where(edge, x_edge, x))
            zs.append(z)
        x = jnp.concatenate(xs, axis=1).astype(BF16)
        y = ys[:, 0:width] + _dot(x, wy)
        return _gelu_tanh(y), jnp.concatenate(zs, axis=1)

    gp, zp = process(u_ref[0:rp, :], cp, None, None)
    g_ref[0:rp, :] = gp.astype(g_ref.dtype)
    zp_ref[...] = zp

    def body(j, carry):
        rows = pl.ds(pl.multiple_of(rp + j * cs, cs), cs)
        gs, _ = process(u_ref[rows, :], cs, h0_ref[pl.ds(j, 1), :], ph_s[pl.ds(j, 1), :])
        g_ref[rows, :] = gs.astype(g_ref.dtype)
        return carry

    lax.fori_loop(0, n_s, body, 0)


def _s5_call(u_flat, w_u, w_y, pw, h0, *, rp, cp, cs, n_s):
    n_g, r_tot, width = u_flat.shape
    n_h0 = h0.shape[1]
    grp = lambda shape: pl.BlockSpec((None,) + shape, lambda g: (g, 0, 0))
    return pl.pallas_call(
        functools.partial(_s5_kernel, rp=rp, cp=cp, cs=cs, n_s=n_s),
        grid=(n_g,),
        in_specs=[grp((r_tot, width)), grp((width, 2 * width)), grp((width, width)),
                  grp(pw.shape[1:]), grp((n_h0, width))],
        out_specs=[grp((r_tot, width)), grp((rp, width))],
        out_shape=[jax.ShapeDtypeStruct((n_g, r_tot, width), BF16),
                   jax.ShapeDtypeStruct((n_g, rp, width), F32)],
        scratch_shapes=[pltpu.VMEM((n_h0, width), F32)],
        compiler_params=_cparams(1),
        name="s5_core",
    )(u_flat, w_u, w_y, pw, h0)


def _ctx_attn_kernel(q_ref, k_ref, v_ref, o_ref, *, scale):
    hd = HEAD_DIM
    for h in range(q_ref.shape[1] // hd):
        sl = slice(h * hd, (h + 1) * hd)
        s = _dot_nt(q_ref[:, sl], k_ref[:, sl].astype(BF16)) * scale
        m = jnp.max(s, axis=-1, keepdims=True)
        p = jnp.exp(s - m)
        l = jnp.sum(p, axis=-1, keepdims=True)
        o = _dot(p.astype(BF16), v_ref[:, sl].astype(BF16)) / l
        o_ref[:, sl] = o.astype(o_ref.dtype)


def _ctx_attn_call(q, k, v, *, n_seq, seq_len):
    naw = q.shape[1]
    spec = pl.BlockSpec((seq_len, naw), lambda b: (b, 0))
    return pl.pallas_call(
        functools.partial(_ctx_attn_kernel, scale=HEAD_DIM ** -0.5),
        grid=(n_seq,),
        in_specs=[spec, spec, spec],
        out_specs=spec,
        out_shape=jax.ShapeDtypeStruct((n_seq * seq_len, naw), BF16),
        compiler_params=_cparams(1),
        name="ctx_attn",
    )(q, k, v)


def _na_bias_kernel(rpb_ref, o_ref, t_s):
    h = pl.program_id(0)
    n_dr = 2 * NA_WIN_R - 1
    n_dc = 2 * NA_WIN_C - 1
    w = GRID_W
    qc = lax.broadcasted_iota(jnp.int32, (w, w), 0)
    kc = lax.broadcasted_iota(jnp.int32, (w, w), 1)
    dc = kc - qc + (NA_WIN_C - 1)
    c0 = jnp.clip(qc - NA_WIN_C // 2, 0, w - NA_WIN_C)
    col_ok = (kc >= c0) & (kc < c0 + NA_WIN_C)
    for dr in range(n_dr):
        def pick(j, t, dr=dr):
            return jnp.where(dc == j, rpb_ref[h * (n_dr * n_dc) + dr * n_dc + j], t)
        t = lax.fori_loop(0, n_dc, pick, jnp.zeros((w, w), F32))
        t_s[dr] = jnp.where(col_ok, t, NEG_INF)
    neg = jnp.full((w, w), NEG_INF, F32)
    patterns = [(0, lambda i: 0), (NA_WIN_R // 2, lambda i: i), (NA_WIN_R, lambda i: NA_QROWS)]
    for pat, (r_rel, r0_rel) in enumerate(patterns):
        for i in range(NA_QROWS):
            for kr in range(NA_KROWS):
                dr = kr - i - r_rel + (NA_WIN_R - 1)
                valid = r0_rel(i) <= kr < r0_rel(i) + NA_WIN_R
                o_ref[pat, i * w:(i + 1) * w, kr * w:(kr + 1) * w] = t_s[dr] if valid else neg


def _na_bias_call(rpb):
    n_h = rpb.shape[0]
    nq = NA_QROWS * GRID_W
    nk = NA_KROWS * GRID_W
    return pl.pallas_call(
        _na_bias_kernel,
        grid=(n_h,),
        in_specs=[pl.BlockSpec(memory_space=pltpu.SMEM)],
        out_specs=pl.BlockSpec((None, 3, nq, nk), lambda h: (h, 0, 0, 0)),
        out_shape=jax.ShapeDtypeStruct((n_h, 3, nq, nk), F32),
        scratch_shapes=[pltpu.VMEM((2 * NA_WIN_R - 1, GRID_W, GRID_W), F32)],
        compiler_params=_cparams(1),
        name="na_bias",
    )(rpb.reshape(-1))


def _na_kernel(q_ref, k_ref, v_ref, kc_ref, vc_ref, bb_ref, o_ref, kb_s, vb_s, kcb_s, vcb_s,
               *, rows, scale):
    w = GRID_W
    nq = NA_QROWS * w
    nk = NA_KROWS * w
    n_blk = rows // NA_QROWS
    kb_s[...] = k_ref[...].astype(BF16)
    vb_s[...] = v_ref[...].astype(BF16)
    kcb_s[...] = kc_ref[...].astype(BF16)
    vcb_s[...] = vc_ref[...].astype(BF16)

    def body(blk, carry):
        q_rows = pl.ds(pl.multiple_of(blk * nq, nq), nq)
        k_base = jnp.clip(blk * NA_QROWS - NA_WIN_R // 2, 0, rows - NA_KROWS)
        k_rows = pl.ds(pl.multiple_of(k_base * w, nq), nk)
        pat = jnp.where(blk == 0, 0, jnp.where(blk == n_blk - 1, 2, 1))
        q = q_ref[q_rows, :]
        s_loc = _dot_nt(q, kb_s[k_rows, :]) * scale + bb_ref[pat]
        s_ctx = _dot_nt(q, kcb_s[...]) * scale
        m = jnp.maximum(jnp.max(s_loc, axis=-1, keepdims=True), jnp.max(s_ctx, axis=-1, keepdims=True))
        p_loc = jnp.exp(s_loc - m)
        p_ctx = jnp.exp(s_ctx - m)
        l = jnp.sum(p_loc, axis=-1, keepdims=True) + jnp.sum(p_ctx, axis=-1, keepdims=True)
        o = _dot(p_loc.astype(BF16), vb_s[k_rows, :]) + _dot(p_ctx.astype(BF16), vcb_s[...])
        o_ref[q_rows, :] = (o / l).astype(o_ref.dtype)
        return carry

    lax.fori_loop(0, n_blk, body, 0)


def _na_call(q, k, v, cache_k, cache_v, bias, *, layer_e, n_b, seq_len, row_blk0):
    naw = q.shape[1]
    n_h = naw // HEAD_DIM
    past = cache_k.shape[2]
    rows = seq_len // GRID_W
    assert rows % NA_QROWS == 0 and rows >= NA_KROWS
    tok = pl.BlockSpec((seq_len, HEAD_DIM), lambda b, h: (row_blk0 + b, h))
    ctx = pl.BlockSpec((None, None, past, HEAD_DIM), lambda b, h: (b, layer_e, 0, h))
    return pl.pallas_call(
        functools.partial(_na_kernel, rows=rows, scale=HEAD_DIM ** -0.5),
        grid=(n_b, n_h),
        in_specs=[tok, tok, tok, ctx, ctx,
                  pl.BlockSpec((None,) + bias.shape[1:], lambda b, h: (h, 0, 0, 0))],
        out_specs=pl.BlockSpec((seq_len, HEAD_DIM), lambda b, h: (b, h)),
        out_shape=jax.ShapeDtypeStruct((n_b * seq_len, naw), BF16),
        scratch_shapes=[pltpu.VMEM((seq_len, HEAD_DIM), BF16), pltpu.VMEM((seq_len, HEAD_DIM), BF16),
                        pltpu.VMEM((past, HEAD_DIM), BF16), pltpu.VMEM((past, HEAD_DIM), BF16)],
        compiler_params=_cparams(2),
        name="na_attn",
    )(q, k, v, cache_k, cache_v, bias)


def _about_kernel(g_ref, a_ref, x_ref, mod_ref, gw_ref, gb_ref, w_ref, o_ref, *, nc):
    s5w = g_ref.shape[1]
    g = g_ref[...]
    gl = _dot(g, gw_ref[...]) + gb_ref[...]
    s5o = (g.astype(F32) * jax.nn.sigmoid(gl)).astype(BF16)
    a = a_ref[...]
    d = o_ref.shape[1]
    for n0 in range(0, d, nc):
        y = _dot(s5o, w_ref[0:s5w, n0:n0 + nc]) + _dot(a, w_ref[s5w:, n0:n0 + nc])
        o_ref[:, n0:n0 + nc] = x_ref[:, n0:n0 + nc] + mod_ref[2:3, n0:n0 + nc] * y


def _about_call(g, attn, x, modt, glu_w, glu_b, w_out, *, tm):
    m, d = x.shape
    s5w = g.shape[1]
    naw = attn.shape[1]
    nc = min(512, d)
    row = lambda i: (i, 0)
    return pl.pallas_call(
        functools.partial(_about_kernel, nc=nc),
        grid=(m // tm,),
        in_specs=[
            pl.BlockSpec((tm, s5w), row),
            pl.BlockSpec((tm, naw), row),
            pl.BlockSpec((tm, d), row),
            pl.BlockSpec((None, 8, d), lambda i: (i, 0, 0)),
            _const_spec((s5w, s5w), lambda i: (0, 0)),
            _const_spec((1, s5w), lambda i: (0, 0)),
            _const_spec((s5w + naw, d), lambda i: (0, 0)),
        ],
        out_specs=pl.BlockSpec((tm, d), row),
        out_shape=jax.ShapeDtypeStruct((m, d), F32),
        compiler_params=_cparams(1),
        name="ab_out_proj",
    )(g, attn, x, modt, glu_w, glu_b, w_out)


def _mlp_kernel(x_ref, mod_ref, g_ref, w1_ref, w2_ref, o_ref, h_s, a_s, *, rc, nc1, nc2):
    @pl.when(pl.program_id(1) == 0)
    def _():
        _modulate_rows(x_ref, h_s, g_ref[...], mod_ref[3:4, :], mod_ref[4:5, :], rc, copy_ref=o_ref)

    h = h_s[...]
    tf = w1_ref.shape[1]
    for c0 in range(0, tf, nc1):
        a = jnp.maximum(_dot(h, w1_ref[:, c0:c0 + nc1]), 0.0)
        a_s[:, c0:c0 + nc1] = (a * a).astype(a_s.dtype)
    a = a_s[...]
    d = o_ref.shape[1]
    for n0 in range(0, d, nc2):
        o_ref[:, n0:n0 + nc2] += mod_ref[5:6, n0:n0 + nc2] * _dot(a, w2_ref[:, n0:n0 + nc2])


def _mlp_call(x, modt, g, w1, w2, *, tm, tf):
    m, d = x.shape
    d_ff = w1.shape[1]
    rc = min(128, tm)
    return pl.pallas_call(
        functools.partial(_mlp_kernel, rc=rc, nc1=min(256, tf), nc2=min(512, d)),
        grid=(m // tm, d_ff // tf),
        in_specs=[
            pl.BlockSpec((tm, d), lambda i, f: (i, 0)),
            pl.BlockSpec((None, 8, d), lambda i, f: (i, 0, 0)),
            _const_spec((1, d), lambda i, f: (0, 0)),
            pl.BlockSpec((d, tf), lambda i, f: (0, f)),
            pl.BlockSpec((tf, d), lambda i, f: (f, 0)),
        ],
        out_specs=pl.BlockSpec((tm, d), lambda i, f: (i, 0)),
        out_shape=jax.ShapeDtypeStruct((m, d), F32),
        scratch_shapes=[pltpu.VMEM((tm, d), BF16), pltpu.VMEM((tm, tf), BF16)],
        compiler_params=_cparams(2),
        name="mlp",
    )(x, modt, g, w1, w2)


def _convin_kernel(x_ref, mod_ref, g_ref, wb_ref, wc_ref, wx_ref, gb_ref, z_ref, h_s, *, rc):
    @pl.when(pl.program_id(1) == 0)
    def _():
        _modulate_rows(x_ref, h_s, g_ref[...], mod_ref[0:1, :], mod_ref[1:2, :], rc)

    h = h_s[...]
    gb_ref[...] = _dot(h, wb_ref[...]).astype(gb_ref.dtype)
    z_ref[...] = (_dot(h, wc_ref[...]) * _dot(h, wx_ref[...])).astype(z_ref.dtype)


def _convin_call(x, modt, g, w_in, *, tm, tn):
    m, d = x.shape
    nb = d // tn
    rc = min(128, tm)
    out = pl.BlockSpec((tm, tn), lambda i, j: (i, j))
    return pl.pallas_call(
        functools.partial(_convin_kernel, rc=rc),
        grid=(m // tm, nb),
        in_specs=[
            pl.BlockSpec((tm, d), lambda i, j: (i, 0)),
            pl.BlockSpec((None, 8, d), lambda i, j: (i, 0, 0)),
            _const_spec((1, d), lambda i, j: (0, 0)),
            pl.BlockSpec((d, tn), lambda i, j: (0, j)),
            pl.BlockSpec((d, tn), lambda i, j: (0, nb + j)),
            pl.BlockSpec((d, tn), lambda i, j: (0, 2 * nb + j)),
        ],
        out_specs=[out, out],
        out_shape=[jax.ShapeDtypeStruct((m, d), BF16), jax.ShapeDtypeStruct((m, d), BF16)],
        scratch_shapes=[pltpu.VMEM((tm, d), BF16)],
        compiler_params=_cparams(2),
        name="conv_in_proj",
    )(x, modt, g, w_in, w_in, w_in)


def _convout_kernel(gb_ref, z_ref, zp_ref, zn_ref, x_ref, mod_ref, cw_ref, cb_ref, w_ref, o_ref, t_s,
                    *, n_p_tiles, lp, ls, cc, nc):
    i = pl.program_id(0)
    tm, d = z_ref.shape
    halo = zp_ref.shape[0]
    seq_mask = jnp.where(i < n_p_tiles, lp - 1, ls - 1)
    ridx = lax.broadcasted_iota(jnp.int32, (tm, 1), 0)
    pos = (i * tm + ridx) & seq_mask
    is_start = pos == 0
    is_end = pos == seq_mask
    for c0 in range(0, d, cc):
        cs = slice(c0, c0 + cc)
        z = z_ref[:, cs].astype(F32)
        z_prev = jnp.where(ridx == 0, zp_ref[halo - 1:halo, cs].astype(F32), pltpu.roll(z, 1, 0))
        z_prev = jnp.where(is_start, 0.0, z_prev)
        z_next = jnp.where(ridx == tm - 1, zn_ref[0:1, cs].astype(F32), pltpu.roll(z, tm - 1, 0))
        z_next = jnp.where(is_end, 0.0, z_next)
        conv = z_prev * cw_ref[0:1, cs] + z * cw_ref[1:2, cs] + z_next * cw_ref[2:3, cs] + cb_ref[:, cs]
        t_s[:, cs] = (gb_ref[:, cs].astype(F32) * conv).astype(t_s.dtype)
    t = t_s[...]
    for n0 in range(0, d, nc):
        o_ref[:, n0:n0 + nc] = x_ref[:, n0:n0 + nc] + mod_ref[2:3, n0:n0 + nc] * _dot(t, w_ref[:, n0:n0 + nc])


def _convout_call(gb, z, x, modt, conv_w, conv_b, w_out, *, tm, n_p_tiles, lp, ls):
    m, d = x.shape
    halo = 16
    assert tm % halo == 0 and lp & (lp - 1) == 0 and ls & (ls - 1) == 0
    assert (n_p_tiles * tm) % ls == 0 or True
    hb = tm // halo
    last = m // halo - 1
    row = lambda i: (i, 0)
    return pl.pallas_call(
        functools.partial(_convout_kernel, n_p_tiles=n_p_tiles, lp=lp, ls=ls,
                          cc=min(256, d), nc=min(512, d)),
        grid=(m // tm,),
        in_specs=[
            pl.BlockSpec((tm, d), row),
            pl.BlockSpec((tm, d), row),
            pl.BlockSpec((halo, d), lambda i: (jnp.maximum(i * hb - 1, 0), 0)),
            pl.BlockSpec((halo, d), lambda i: (jnp.minimum((i + 1) * hb, last), 0)),
            pl.BlockSpec((tm, d), row),
            pl.BlockSpec((None, 8, d), lambda i: (i, 0, 0)),
            _const_spec((8, d), lambda i: (0, 0)),
            _const_spec((1, d), lambda i: (0, 0)),
            _const_spec((d, d), lambda i: (0, 0)),
        ],
        out_specs=pl.BlockSpec((tm, d), row),
        out_shape=jax.ShapeDtypeStruct((m, d), F32),
        scratch_shapes=[pltpu.VMEM((tm, d), BF16)],
        compiler_params=_cparams(1),
        name="conv_out_proj",
    )(gb, z, z, z, x, modt, conv_w, conv_b, w_out)


def _tile_rows(m_p, m_s, l_s, n_b, tm):
    assert m_p % tm == 0 and l_s % tm == 0
    return np.concatenate([np.full(m_p // tm, n_b), np.repeat(np.arange(n_b), l_s // tm)])


def _pick_tile(pref, m_p, l_s):
    tm = pref
    while m_p % tm or l_s % tm:
        tm //= 2
    return tm


def kernel(x_prompt, x_sample, c, cache_k, cache_v, state_ssm_re, state_ssm_im, c_ctx, ada_w, ada_b, norm1_g, norm2_g, ab_w_in, ab_w_out, s5_lam_re, s5_lam_im, s5_log_dt, s5_b_re, s5_b_im, s5_c_re, s5_c_im, s5_d, s5_glu_w, s5_glu_b, q_norm_g, k_norm_g, na_rpb, conv_w_in, conv_w, conv_b, conv_w_out, mlp_w1, mlp_w2):
    n_bp, l_p, d = x_prompt.shape
    n_bs, l_s, _ = x_sample.shape
    depth = ada_w.shape[0]
    m_p, m_s = n_bp * l_p, n_bs * l_s
    m = m_p + m_s
    n_g, n_state = s5_lam_re.shape[2], s5_lam_re.shape[3]
    s5w = n_g * S5_GROUP_CH
    naw = (ab_w_in.shape[2] - s5w) // 3
    n_h = naw // HEAD_DIM
    past = cache_k.shape[2]
    d_ff = mlp_w1.shape[2]
    assert m_p % l_s == 0, "latent sequences must start on a sequence-length row block"
    assert l_p % S5_CHUNK == 0 and l_s % S5_CHUNK == 0
    c_p, c_s = l_p // S5_CHUNK, l_s // S5_CHUNK
    assert c_p & (c_p - 1) == 0 and c_s & (c_s - 1) == 0
    r_p, r_s = m_p // S5_CHUNK, m_s // S5_CHUNK
    width = S5_CHUNK * S5_GROUP_CH

    x = jnp.concatenate([x_prompt.reshape(m_p, d), x_sample.reshape(m_s, d)], axis=0)

    n_rows = -(-(n_bs + 1) // 16) * 16
    cvec = jnp.concatenate([c, c_ctx[None], jnp.zeros((n_rows - n_bs - 1, d), F32)], axis=0)
    mod = _ada_call(cvec, ada_w, ada_b)

    def mod_tiles(layer, tm):
        t = mod[layer][_tile_rows(m_p, m_s, l_s, n_bs, tm)].reshape(m // tm, 6, d)
        return jnp.pad(t, ((0, 0), (0, 2), (0, 0)))

    tm_big = _pick_tile(1024, m_p, l_s)
    tm_mid = _pick_tile(512, m_p, l_s)
    tf = min(512, d_ff)
    tn_conv = min(512, d)

    n_pw = max(int(math.log2(c_s)), 1)
    w_u, w_y, pw = _s5prep_call(s5_lam_re, s5_lam_im, s5_log_dt, s5_b_re, s5_b_im, s5_c_re, s5_c_im,
                                s5_d, n_pw)
    n_h0 = -(-n_bs // 8) * 8
    h0_all = jnp.stack([state_ssm_re, state_ssm_im], axis=-2)
    h0_all = h0_all.transpose(1, 3, 0, 2, 4, 5).reshape(-1, n_g, n_bs, 4 * n_state)
    h0_all = jnp.pad(h0_all, ((0, 0), (0, 0), (0, n_h0 - n_bs), (0, 0)))
    cache_k4 = cache_k.reshape(n_bs, -1, past, naw)
    cache_v4 = cache_v.reshape(n_bs, -1, past, naw)

    new_k, new_v, new_re, new_im = [], [], [], []
    for layer in range(depth):
        g1 = norm1_g[layer].reshape(1, d)
        g2 = norm2_g[layer].reshape(1, d)
        if layer % 2 == 0:
            e = layer // 2
            u, q, k, v = _inproj_call(x, mod_tiles(layer, tm_mid), g1, ab_w_in[e].astype(BF16),
                                      q_norm_g[e].reshape(1, HEAD_DIM), k_norm_g[e].reshape(1, HEAD_DIM),
                                      tm=tm_mid, s5w=s5w, naw=naw)
            new_k.append(k[:m_p].reshape(n_bp, l_p, n_h, HEAD_DIM))
            new_v.append(v[:m_p].reshape(n_bp, l_p, n_h, HEAD_DIM))
            u_flat = u.reshape(r_p + r_s, S5_CHUNK, n_g, S5_GROUP_CH).transpose(2, 0, 1, 3)
            u_flat = u_flat.reshape(n_g, r_p + r_s, width)
            g_flat, z_p = _s5_call(u_flat, w_u[e], w_y[e], pw[e], h0_all[e],
                                   rp=r_p, cp=c_p, cs=c_s, n_s=n_bs)
            g_tok = g_flat.reshape(n_g, r_p + r_s, S5_CHUNK, S5_GROUP_CH).transpose(1, 2, 0, 3)
            g_tok = g_tok.reshape(m, s5w)
            z_p = z_p.reshape(n_g, n_bp, c_p, 2, 2, n_state)
            fin = jnp.stack([z_p[:, :, c_p - 1, 0], z_p[:, :, 0, 1]], axis=2)
            new_re.append(fin[:, :, :, 0].transpose(1, 2, 0, 3))
            new_im.append(fin[:, :, :, 1].transpose(1, 2, 0, 3))
            attn_p = _ctx_attn_call(q, k, v, n_seq=n_bp, seq_len=l_p)
            bias = _na_bias_call(na_rpb[e])
            attn_s = _na_call(q, k, v, cache_k4, cache_v4, bias, layer_e=e, n_b=n_bs, seq_len=l_s,
                              row_blk0=m_p // l_s)
            attn = jnp.concatenate([attn_p, attn_s], axis=0)
            x = _about_call(g_tok, attn, x, mod_tiles(layer, tm_mid), s5_glu_w[e].astype(BF16),
                            s5_glu_b[e].reshape(1, s5w), ab_w_out[e].astype(BF16), tm=tm_mid)
        else:
            o = layer // 2
            gb, z = _convin_call(x, mod_tiles(layer, tm_big), g1, conv_w_in[o].astype(BF16),
                                 tm=tm_big, tn=tn_conv)
            cw8 = jnp.pad(conv_w[o], ((0, 5), (0, 0)))
            x = _convout_call(gb, z, x, mod_tiles(layer, tm_mid), cw8, conv_b[o].reshape(1, d),
                              conv_w_out[o].astype(BF16), tm=tm_mid, n_p_tiles=m_p // tm_mid,
                              lp=l_p, ls=l_s)
        x = _mlp_call(x, mod_tiles(layer, tm_big), g2, mlp_w1[layer].astype(BF16),
                      mlp_w2[layer].astype(BF16), tm=tm_big, tf=tf)

    y_prompt = x[:m_p].reshape(n_bp, l_p, d)
    y_sample = x[m_p:].reshape(n_bs, l_s, d)
    return (y_prompt, y_sample, jnp.stack(new_k, axis=1), jnp.stack(new_v, axis=1),
            jnp.stack(new_re, axis=1), jnp.stack(new_im, axis=1))
```

```python
import functools
import math

import numpy as np
import jax
import jax.numpy as jnp
from jax import lax
from jax.experimental import pallas as pl
from jax.experimental.pallas import tpu as pltpu

F32 = jnp.float32
BF16 = jnp.bfloat16

NORM_EPS = 1e-6
NEG_INF = -1e30

S5_GROUP_CH = 16
HEAD_DIM = 128
GRID_W = 64
NA_WIN_R = 8
NA_WIN_C = 16
S5_CHUNK = 16
NA_QROWS = 4
NA_KROWS = NA_QROWS + NA_WIN_R

VMEM_LIMIT_BYTES = 62 * 1024 * 1024


def _cparams(n_axes):
    return pltpu.CompilerParams(dimension_semantics=("arbitrary",) * n_axes,
                                vmem_limit_bytes=VMEM_LIMIT_BYTES)


def _const_spec(shape, index_map):
    return pl.BlockSpec(shape, index_map, pipeline_mode=pl.Buffered(1))


def _dot(a, b):
    return jnp.dot(a, b, preferred_element_type=F32)


def _dot_nt(a, b):
    return lax.dot_general(a, b, (((1,), (1,)), ((), ())), preferred_element_type=F32)


def _dot_split(a, b):
    a_hi = a.astype(BF16)
    a_lo = (a - a_hi.astype(F32)).astype(BF16)
    b_hi = b.astype(BF16)
    b_lo = (b - b_hi.astype(F32)).astype(BF16)
    return _dot(a_hi, b_hi) + _dot(a_hi, b_lo) + _dot(a_lo, b_hi)


def _modulate(x, g, shift, scale):
    ms = jnp.mean(x * x, axis=-1, keepdims=True)
    y = x * lax.rsqrt(ms + NORM_EPS) * g
    return y * (1.0 + scale) + shift


def _modulate_rows(x_ref, h_ref, g, shift, scale, rc, copy_ref=None):
    tm = x_ref.shape[0]

    def body(r, carry):
        rows = pl.ds(pl.multiple_of(r * rc, rc), rc)
        x = x_ref[rows, :]
        h_ref[rows, :] = _modulate(x, g, shift, scale).astype(h_ref.dtype)
        if copy_ref is not None:
            copy_ref[rows, :] = x
        return carry

    lax.fori_loop(0, tm // rc, body, 0)


def _ada_kernel(c_ref, w_ref, b_ref, o_ref, *, nc):
    cv = c_ref[...]
    sc = (cv * jax.nn.sigmoid(cv)).astype(BF16)
    tn = w_ref.shape[1]
    for n0 in range(0, tn, nc):
        w = w_ref[:, n0:n0 + nc].astype(BF16)
        o_ref[:, n0:n0 + nc] = _dot(sc, w) + b_ref[:, n0:n0 + nc]


def _ada_call(cvec, ada_w, ada_b):
    depth, d, n6 = ada_w.shape
    rows = cvec.shape[0]
    tn = 1536 if n6 % 1536 == 0 else n6
    nc = 512 if tn % 512 == 0 else tn
    return pl.pallas_call(
        functools.partial(_ada_kernel, nc=nc),
        grid=(depth, n6 // tn),
        in_specs=[
            pl.BlockSpec((rows, d), lambda l, j: (0, 0)),
            pl.BlockSpec((None, d, tn), lambda l, j: (l, 0, j)),
            pl.BlockSpec((None, 1, tn), lambda l, j: (l, 0, j)),
        ],
        out_specs=pl.BlockSpec((None, rows, tn), lambda l, j: (l, 0, j)),
        out_shape=jax.ShapeDtypeStruct((depth, rows, n6), F32),
        compiler_params=_cparams(2),
        name="ada_params",
    )(cvec, ada_w, ada_b.reshape(depth, 1, n6))


def _x_specs(n_x, tm, d, n_p_tiles):
    if n_x == 1:
        return [pl.BlockSpec((tm, d), lambda i: (i, 0))]
    return [pl.BlockSpec((tm, d), lambda i: (jnp.minimum(i, n_p_tiles - 1), 0)),
            pl.BlockSpec((tm, d), lambda i: (jnp.maximum(i - n_p_tiles, 0), 0))]


def _for_part(i, n_p_tiles, refs, fn):
    if len(refs) == 1:
        fn(refs[0])
    else:
        pl.when(i < n_p_tiles)(lambda: fn(refs[0]))
        pl.when(i >= n_p_tiles)(lambda: fn(refs[1]))


def _inproj_kernel(*refs, n_x, n_p_tiles, s5w, naw, rc):
    x_refs = refs[:n_x]
    mod_ref, g_ref, w_ref, qg_ref, kg_ref, u_ref, q_ref, k_ref, v_ref, h_s = refs[n_x:]
    _for_part(pl.program_id(0), n_p_tiles, x_refs,
              lambda x_ref: _modulate_rows(x_ref, h_s, g_ref[...], mod_ref[0:1, :], mod_ref[1:2, :], rc))
    h = h_s[...]
    u_ref[...] = _dot(h, w_ref[:, 0:s5w]).astype(u_ref.dtype)
    hd = HEAD_DIM
    cw = min(4 * hd, naw)
    qg = qg_ref[...]
    kg = kg_ref[...]

    def head_norm(t, gain):
        return t * lax.rsqrt(jnp.mean(t * t, axis=-1, keepdims=True) + NORM_EPS) * gain

    for c0 in range(0, naw, cw):
        qc = _dot(h, w_ref[:, s5w + c0:s5w + c0 + cw])
        for j in range(cw // hd):
            q_ref[:, c0 + j * hd:c0 + (j + 1) * hd] = head_norm(
                qc[:, j * hd:(j + 1) * hd], qg).astype(q_ref.dtype)
        kc = _dot(h, w_ref[:, s5w + naw + c0:s5w + naw + c0 + cw])
        for j in range(cw // hd):
            k_ref[:, c0 + j * hd:c0 + (j + 1) * hd] = head_norm(
                kc[:, j * hd:(j + 1) * hd], kg).astype(k_ref.dtype)
        v_ref[:, c0:c0 + cw] = _dot(
            h, w_ref[:, s5w + 2 * naw + c0:s5w + 2 * naw + c0 + cw]).astype(v_ref.dtype)


def _inproj_call(xs, modt, g, w_in, qg, kg, *, tm, s5w, naw, n_p_tiles):
    m = sum(x.shape[0] for x in xs)
    d = xs[0].shape[1]
    n_in = w_in.shape[1]
    rc = min(128, tm)
    row = lambda i: (i, 0)
    return pl.pallas_call(
        functools.partial(_inproj_kernel, n_x=len(xs), n_p_tiles=n_p_tiles, s5w=s5w, naw=naw, rc=rc),
        grid=(m // tm,),
        in_specs=_x_specs(len(xs), tm, d, n_p_tiles) + [
            pl.BlockSpec((None, 8, d), lambda i: (i, 0, 0)),
            _const_spec((1, d), lambda i: (0, 0)),
            _const_spec((d, n_in), lambda i: (0, 0)),
            _const_spec((1, HEAD_DIM), lambda i: (0, 0)),
            _const_spec((1, HEAD_DIM), lambda i: (0, 0)),
        ],
        out_specs=[
            pl.BlockSpec((tm, s5w), row),
            pl.BlockSpec((tm, naw), row),
            pl.BlockSpec((tm, naw), row),
            pl.BlockSpec((tm, naw), row),
        ],
        out_shape=[
            jax.ShapeDtypeStruct((m, s5w), F32),
            jax.ShapeDtypeStruct((m, naw), BF16),
            jax.ShapeDtypeStruct((m, naw), F32),
            jax.ShapeDtypeStruct((m, naw), F32),
        ],
        scratch_shapes=[pltpu.VMEM((tm, d), BF16)],
        compiler_params=_cparams(1),
        name="ab_in_proj",
    )(*xs, modt, g, w_in, qg, kg)


def _s5prep_kernel(row_ref, col_ref, btr_ref, bti_ref, cer_ref, cei_ref, d_ref,
                   wu_ref, wy_ref, pw_ref, *, n_state, n_pw):
    t_chunk = S5_CHUNK
    width = t_chunk * S5_GROUP_CH
    p = n_state
    lg_ch = int(math.log2(S5_GROUP_CH))
    r_i = lax.broadcasted_iota(jnp.int32, (width, 1), 0)
    c_i = lax.broadcasted_iota(jnp.int32, (1, width), 1)
    s_i = r_i >> lg_ch
    t_i = c_i >> lg_ch
    n_pow = -(-(t_chunk + 1) // 8) * 8

    def cmul(ar, ai, br, bi):
        return ar * br - ai * bi, ar * bi + ai * br

    acc = jnp.zeros((width, width), F32)
    for d in range(2):
        lr = row_ref[d, 0:1, 0:p]
        li = row_ref[d, 1:2, 0:p]
        dt = jnp.exp(row_ref[d, 2:3, 0:p])
        ar = lr * dt
        ai = li * dt
        er = jnp.exp(ar)
        abr = er * jnp.cos(ai)
        abi = er * jnp.sin(ai)
        den = lr * lr + li * li
        nr = abr - 1.0
        f_re = (nr * lr + abi * li) / den
        f_im = (abi * lr - nr * li) / den
        bb_re, bb_im = cmul(f_re, f_im, btr_ref[d], bti_ref[d])

        lrc = col_ref[d, :, 0:1]
        lic = col_ref[d, :, 1:2]
        dtc = jnp.exp(col_ref[d, :, 2:3])
        arc = lrc * dtc
        aic = lic * dtc
        ce_re = cer_ref[d]
        ce_im = cei_ref[d]

        k_r = lax.broadcasted_iota(jnp.int32, (n_pow, 1), 0).astype(F32)
        k_c = lax.broadcasted_iota(jnp.int32, (1, 128), 1).astype(F32)
        mag_r = jnp.exp(k_r * ar)
        tr_re, tr_im = mag_r * jnp.cos(k_r * ai), mag_r * jnp.sin(k_r * ai)
        mag_c = jnp.exp(arc * k_c)
        tc_re, tc_im = mag_c * jnp.cos(aic * k_c), mag_c * jnp.sin(aic * k_c)

        def pow_row(e, e_max):
            re = jnp.zeros((width, p), F32)
            im = jnp.zeros((width, p), F32)
            for kk in range(e_max + 1):
                hit = e == kk
                re = jnp.where(hit, tr_re[kk:kk + 1, :], re)
                im = jnp.where(hit, tr_im[kk:kk + 1, :], im)
            return re, im

        def pow_col(e, e_max):
            re = jnp.zeros((p, width), F32)
            im = jnp.zeros((p, width), F32)
            for kk in range(e_max + 1):
                hit = e == kk
                re = jnp.where(hit, tc_re[:, kk:kk + 1], re)
                im = jnp.where(hit, tc_im[:, kk:kk + 1], im)
            return re, im

        terms = [(jnp.zeros_like(s_i), jnp.zeros_like(t_i), s_i == t_i, 0)]
        blk = 2
        while blk <= t_chunk:
            lb = int(math.log2(blk))
            half = blk // 2
            mid_s = ((s_i >> lb) << lb) + half
            mid_t = ((t_i >> lb) << lb) + half
            same = (s_i >> lb) == (t_i >> lb)
            if d == 0:
                e_s, ok_s = mid_s - s_i, s_i < mid_s
                e_t, ok_t = t_i - mid_t, t_i >= mid_t
            else:
                e_s, ok_s = s_i - mid_s, s_i >= mid_s
                e_t, ok_t = mid_t - t_i, t_i < mid_t
            terms.append((jnp.maximum(e_s, 0), jnp.maximum(e_t, 0), same & ok_s & ok_t, half))
            blk *= 2
        for e_s, e_t, mask, e_max in terms:
            l_re, l_im = cmul(*pow_row(e_s, e_max), bb_re, bb_im)
            r_re, r_im = cmul(*pow_col(e_t, e_max), ce_re, ce_im)
            term = _dot_split(l_re, r_re) - _dot_split(l_im, r_im)
            acc = acc + jnp.where(mask, term, 0.0)

        if d == 0:
            ws_re, ws_im = cmul(*pow_row(t_chunk - 1 - s_i, t_chunk), bb_re, bb_im)
            ca_re, ca_im = cmul(*pow_col(t_i + 1, t_chunk), ce_re, ce_im)
        else:
            ws_re, ws_im = cmul(*pow_row(s_i, t_chunk), bb_re, bb_im)
            ca_re, ca_im = cmul(*pow_col(t_chunk - t_i, t_chunk), ce_re, ce_im)
        base = width + d * 2 * p
        wu_ref[:, base:base + p] = ws_re.astype(wu_ref.dtype)
        wu_ref[:, base + p:base + 2 * p] = ws_im.astype(wu_ref.dtype)
        wy_ref[d * 2 * p:d * 2 * p + p, :] = ca_re.astype(wy_ref.dtype)
        wy_ref[d * 2 * p + p:(d + 1) * 2 * p, :] = (-ca_im).astype(wy_ref.dtype)

        lr2 = row_ref[d, 0:1, :]
        li2 = row_ref[d, 1:2, :]
        dt2 = jnp.exp(row_ref[d, 2:3, :])
        k16 = float(t_chunk)
        mag = jnp.exp(k16 * (lr2 * dt2))
        pr = mag * jnp.cos(k16 * (li2 * dt2))
        pi = mag * jnp.sin(k16 * (li2 * dt2))
        lane = lax.broadcasted_iota(jnp.int32, pr.shape, 1)
        sign = jnp.where(lane < p, -1.0, 1.0)
        for i in range(n_pw):
            pw_ref[i * 4 + d * 2:i * 4 + d * 2 + 1, :] = pr
            pw_ref[i * 4 + d * 2 + 1:i * 4 + d * 2 + 2, :] = pi * sign
            pr, pi = pr * pr - pi * pi, 2.0 * pr * pi

    wu_ref[:, 0:width] = (acc + jnp.where(r_i == c_i, d_ref[...], 0.0)).astype(wu_ref.dtype)


def _s5prep_call(lam_re, lam_im, log_dt, b_re, b_im, c_re, c_im, d_skip, n_pw):
    n_e, _, n_g, p = lam_re.shape
    n_ch = S5_GROUP_CH
    width = S5_CHUNK * n_ch
    assert 2 * p == 128, "state rows are packed as [re | im] in one 128-lane tile"
    dtb = jnp.broadcast_to(log_dt[..., None], lam_re.shape)
    zeros = jnp.zeros_like(lam_re)
    rowp = jnp.stack([lam_re, lam_im, dtb] + [zeros] * 5, axis=-2)
    rowp = jnp.concatenate([rowp, rowp], axis=-1)
    colp = jnp.stack([lam_re, lam_im, dtb] + [zeros] * 5, axis=-1)
    bt = lambda b: jnp.tile(jnp.swapaxes(b, -1, -2), (1, 1, 1, S5_CHUNK, 1))
    ce = lambda c: jnp.tile(jnp.swapaxes(c, -1, -2), (1, 1, 1, 1, S5_CHUNK))
    d_row = jnp.tile(d_skip.reshape(n_e, n_g, 1, n_ch), (1, 1, 1, S5_CHUNK))

    def dspec(shape):
        return pl.BlockSpec((None, 2, None) + shape, lambda e, g: (e, 0, g, 0, 0))

    def ospec(shape):
        return pl.BlockSpec((None, None) + shape, lambda e, g: (e, g, 0, 0))

    return pl.pallas_call(
        functools.partial(_s5prep_kernel, n_state=p, n_pw=n_pw),
        grid=(n_e, n_g),
        in_specs=[dspec((8, 2 * p)), dspec((p, 8)), dspec((width, p)), dspec((width, p)),
                  dspec((p, width)), dspec((p, width)), ospec((1, width))],
        out_specs=[ospec((width, 2 * width)), ospec((width, width)), ospec((4 * n_pw, 2 * p))],
        out_shape=[
            jax.ShapeDtypeStruct((n_e, n_g, width, 2 * width), BF16),
            jax.ShapeDtypeStruct((n_e, n_g, width, width), BF16),
            jax.ShapeDtypeStruct((n_e, n_g, 4 * n_pw, 2 * p), F32),
        ],
        compiler_params=_cparams(2),
        name="s5_prep",
    )(rowp, colp, bt(b_re), bt(b_im), ce(c_re), ce(c_im), d_row)


def _gelu_tanh(y):
    return 0.5 * y * (1.0 + jnp.tanh(0.7978845608028654 * (y + 0.044715 * (y * y * y))))


def _s5_kernel(u_ref, pin_ref, pout_ref, wu_ref, wy_ref, pw_ref, h0_ref, g_ref, z_ref, ph_s, ug_s, yg_s,
               *, n_p_tiles, cp, cs):
    tile = pl.program_id(1)
    t_chunk = S5_CHUNK
    n = u_ref.shape[0] // t_chunk
    ngb, width = wy_ref.shape[0], wy_ref.shape[1]
    half = width // 2
    lanes = u_ref.shape[1]

    def crot(a1, a2, x):
        return a1 * x + a2 * pltpu.roll(x, half // 2, 1)

    x_cat = jnp.concatenate(
        [u_ref[pl.ds(s, n, stride=t_chunk), :].astype(BF16) for s in range(t_chunk)], axis=1)
    for gi in range(ngb):
        ug_s[gi] = _dot(x_cat, pin_ref[:, gi * width:(gi + 1) * width]).astype(ug_s.dtype)

    def process(gi, cseq, h0, ph):
        ys = _dot(ug_s[gi], wu_ref[gi])
        wy = wy_ref[gi]
        pos = lax.broadcasted_iota(jnp.int32, (n, 1), 0) & (cseq - 1)
        zs, xs = [], []
        for d in range(2):
            z = ys[:, width + d * half:width + (d + 1) * half]
            edge = (pos == 0) if d == 0 else (pos == cseq - 1)
            if h0 is not None:
                z = z + jnp.where(edge, ph[:, d * half:(d + 1) * half], 0.0)
            k = 1
            i = 0
            while k < cseq:
                a1 = pw_ref[gi, pl.ds(i * 4 + d * 2, 1), :]
                a2 = pw_ref[gi, pl.ds(i * 4 + d * 2 + 1, 1), :]
                if d == 0:
                    sh = pltpu.roll(z, k, 0)
                    valid = pos >= k
                else:
                    sh = pltpu.roll(z, n - k, 0)
                    valid = pos < cseq - k
                z = z + jnp.where(valid, crot(a1, a2, sh), 0.0)
                k *= 2
                i += 1
            x = pltpu.roll(z, 1, 0) if d == 0 else pltpu.roll(z, n - 1, 0)
            x_edge = 0.0 if h0 is None else h0[:, d * half:(d + 1) * half]
            xs.append(jnp.where(edge, x_edge, x))
            zs.append(z)
        x = jnp.concatenate(xs, axis=1).astype(BF16)
        y = ys[:, 0:width] + _dot(x, wy)
        yg_s[gi] = _gelu_tanh(y).astype(yg_s.dtype)
        z_ref[gi] = jnp.concatenate(zs, axis=1)

    @pl.when(tile < n_p_tiles)
    def _():
        def body(gi, carry):
            process(gi, cp, None, None)
            return carry
        lax.fori_loop(0, ngb, body, 0)

    @pl.when(tile >= n_p_tiles)
    def _():
        j = tile - n_p_tiles

        def body(gi, carry):
            h0 = h0_ref[gi]
            for d in range(2):
                ph_s[:, d * half:(d + 1) * half] = crot(
                    pw_ref[gi, pl.ds(d * 2, 1), :], pw_ref[gi, pl.ds(d * 2 + 1, 1), :],
                    h0[:, d * half:(d + 1) * half])
            process(gi, cs, h0_ref[gi, pl.ds(j, 1), :], ph_s[pl.ds(j, 1), :])
            return carry
        lax.fori_loop(0, ngb, body, 0)

    y_all = jnp.concatenate([yg_s[gi] for gi in range(ngb)], axis=1)
    for t in range(t_chunk):
        g_ref[pl.ds(t, n, stride=t_chunk), :] = _dot(y_all, pout_ref[:, t * lanes:(t + 1) * lanes])


def _s5_call(u, perm_in, perm_out, w_u, w_y, pw, h0, *, l_tile, n_p_tiles, cp, cs):
    m, s5w = u.shape
    n_g, width = w_y.shape[0], w_y.shape[1]
    ngb = min(8, n_g)
    lanes = ngb * S5_GROUP_CH
    assert lanes == 128 and n_g % ngb == 0 and m % l_tile == 0
    n = l_tile // S5_CHUNK
    n_h0 = h0.shape[1]
    tok = pl.BlockSpec((l_tile, lanes), lambda cb, t: (t, cb))
    grp = lambda shape: pl.BlockSpec((ngb,) + shape, lambda cb, t: (cb, 0, 0))
    return pl.pallas_call(
        functools.partial(_s5_kernel, n_p_tiles=n_p_tiles, cp=cp, cs=cs),
        grid=(n_g // ngb, m // l_tile),
        in_specs=[tok,
                  _const_spec(perm_in.shape, lambda cb, t: (0, 0)),
                  _const_spec(perm_out.shape, lambda cb, t: (0, 0)),
                  grp((width, 2 * width)), grp((width, width)), grp(pw.shape[1:]), grp((n_h0, width))],
        out_specs=[tok, pl.BlockSpec((ngb, n, width), lambda cb, t: (cb, t, 0))],
        out_shape=[jax.ShapeDtypeStruct((m, s5w), F32),
                   jax.ShapeDtypeStruct((n_g, m // S5_CHUNK, width), F32)],
        scratch_shapes=[pltpu.VMEM((n_h0, width), F32),
                        pltpu.VMEM((ngb, n, width), BF16),
                        pltpu.VMEM((ngb, n, width), BF16)],
        compiler_params=_cparams(2),
        name="s5_core",
    )(u, perm_in, perm_out, w_u, w_y, pw, h0)


def _chunk_perm(ngb):
    size = S5_CHUNK * ngb * S5_GROUP_CH
    src = np.arange(size).reshape(S5_CHUNK, ngb, S5_GROUP_CH).transpose(1, 0, 2).reshape(-1)
    perm_in = (jnp.arange(size, dtype=jnp.int32)[:, None] == jnp.asarray(src, jnp.int32)[None, :]).astype(BF16)
    return perm_in, perm_in.T


def _ctx_attn_kernel(q_ref, k_ref, v_ref, o_ref, *, scale):
    hd = HEAD_DIM
    for h in range(q_ref.shape[1] // hd):
        sl = slice(h * hd, (h + 1) * hd)
        s = _dot_nt(q_ref[:, sl], k_ref[:, sl].astype(BF16)) * scale
        m = jnp.max(s, axis=-1, keepdims=True)
        p = jnp.exp(s - m)
        l = jnp.sum(p, axis=-1, keepdims=True)
        o = _dot(p.astype(BF16), v_ref[:, sl].astype(BF16)) / l
        o_ref[:, sl] = o.astype(o_ref.dtype)


def _ctx_attn_call(q, k, v, *, n_seq, seq_len):
    naw = q.shape[1]
    spec = pl.BlockSpec((seq_len, naw), lambda b: (b, 0))
    return pl.pallas_call(
        functools.partial(_ctx_attn_kernel, scale=HEAD_DIM ** -0.5),
        grid=(n_seq,),
        in_specs=[spec, spec, spec],
        out_specs=spec,
        out_shape=jax.ShapeDtypeStruct((n_seq * seq_len, naw), BF16),
        compiler_params=_cparams(1),
        name="ctx_attn",
    )(q, k, v)


def _na_bias_kernel(rpb_ref, o_ref, t_s):
    h = pl.program_id(0)
    n_dr = 2 * NA_WIN_R - 1
    n_dc = 2 * NA_WIN_C - 1
    w = GRID_W
    qc = lax.broadcasted_iota(jnp.int32, (w, w), 0)
    kc = lax.broadcasted_iota(jnp.int32, (w, w), 1)
    dc = kc - qc + (NA_WIN_C - 1)
    c0 = jnp.clip(qc - NA_WIN_C // 2, 0, w - NA_WIN_C)
    col_ok = (kc >= c0) & (kc < c0 + NA_WIN_C)
    for dr in range(n_dr):
        def pick(j, t, dr=dr):
            return jnp.where(dc == j, rpb_ref[h * (n_dr * n_dc) + dr * n_dc + j], t)
        t = lax.fori_loop(0, n_dc, pick, jnp.zeros((w, w), F32))
        t_s[dr] = jnp.where(col_ok, t, NEG_INF)
    neg = jnp.full((w, w), NEG_INF, F32)
    patterns = [(0, lambda i: 0), (NA_WIN_R // 2, lambda i: i), (NA_WIN_R, lambda i: NA_QROWS)]
    for pat, (r_rel, r0_rel) in enumerate(patterns):
        for i in range(NA_QROWS):
            for kr in range(NA_KROWS):
                dr = kr - i - r_rel + (NA_WIN_R - 1)
                valid = r0_rel(i) <= kr < r0_rel(i) + NA_WIN_R
                o_ref[pat, i * w:(i + 1) * w, kr * w:(kr + 1) * w] = t_s[dr] if valid else neg


def _na_bias_call(rpb):
    n_h = rpb.shape[0]
    nq = NA_QROWS * GRID_W
    nk = NA_KROWS * GRID_W
    return pl.pallas_call(
        _na_bias_kernel,
        grid=(n_h,),
        in_specs=[pl.BlockSpec(memory_space=pltpu.SMEM)],
        out_specs=pl.BlockSpec((None, 3, nq, nk), lambda h: (h, 0, 0, 0)),
        out_shape=jax.ShapeDtypeStruct((n_h, 3, nq, nk), F32),
        scratch_shapes=[pltpu.VMEM((2 * NA_WIN_R - 1, GRID_W, GRID_W), F32)],
        compiler_params=_cparams(1),
        name="na_bias",
    )(rpb.reshape(-1))


def _na_kernel(q_ref, k_ref, v_ref, kc_ref, vc_ref, bb_ref, o_ref, kb_s, vb_s, kcb_s, vcb_s,
               *, rows, scale):
    w = GRID_W
    nq = NA_QROWS * w
    nk = NA_KROWS * w
    n_blk = rows // NA_QROWS
    kb_s[...] = k_ref[...].astype(BF16)
    vb_s[...] = v_ref[...].astype(BF16)
    kcb_s[...] = kc_ref[...].astype(BF16)
    vcb_s[...] = vc_ref[...].astype(BF16)

    def body(blk, carry):
        q_rows = pl.ds(pl.multiple_of(blk * nq, nq), nq)
        k_base = jnp.clip(blk * NA_QROWS - NA_WIN_R // 2, 0, rows - NA_KROWS)
        k_rows = pl.ds(pl.multiple_of(k_base * w, nq), nk)
        pat = jnp.where(blk == 0, 0, jnp.where(blk == n_blk - 1, 2, 1))
        q = q_ref[q_rows, :]
        s_loc = _dot_nt(q, kb_s[k_rows, :]) * scale + bb_ref[pat]
        s_ctx = _dot_nt(q, kcb_s[...]) * scale
        m = jnp.maximum(jnp.max(s_loc, axis=-1, keepdims=True), jnp.max(s_ctx, axis=-1, keepdims=True))
        p_loc = jnp.exp(s_loc - m)
        p_ctx = jnp.exp(s_ctx - m)
        l = jnp.sum(p_loc, axis=-1, keepdims=True) + jnp.sum(p_ctx, axis=-1, keepdims=True)
        o = _dot(p_loc.astype(BF16), vb_s[k_rows, :]) + _dot(p_ctx.astype(BF16), vcb_s[...])
        o_ref[q_rows, :] = (o / l).astype(o_ref.dtype)
        return carry

    lax.fori_loop(0, n_blk, body, 0)


def _na_call(q, k, v, cache_k, cache_v, bias, *, layer_e, n_b, seq_len, row_blk0):
    naw = q.shape[1]
    n_h = naw // HEAD_DIM
    past = cache_k.shape[2]
    rows = seq_len // GRID_W
    assert rows % NA_QROWS == 0 and rows >= NA_KROWS
    tok = pl.BlockSpec((seq_len, HEAD_DIM), lambda b, h: (row_blk0 + b, h))
    ctx = pl.BlockSpec((None, None, past, HEAD_DIM), lambda b, h: (b, layer_e, 0, h))
    return pl.pallas_call(
        functools.partial(_na_kernel, rows=rows, scale=HEAD_DIM ** -0.5),
        grid=(n_b, n_h),
        in_specs=[tok, tok, tok, ctx, ctx,
                  pl.BlockSpec((None,) + bias.shape[1:], lambda b, h: (h, 0, 0, 0))],
        out_specs=pl.BlockSpec((seq_len, HEAD_DIM), lambda b, h: (b, h)),
        out_shape=jax.ShapeDtypeStruct((n_b * seq_len, naw), BF16),
        scratch_shapes=[pltpu.VMEM((seq_len, HEAD_DIM), BF16), pltpu.VMEM((seq_len, HEAD_DIM), BF16),
                        pltpu.VMEM((past, HEAD_DIM), BF16), pltpu.VMEM((past, HEAD_DIM), BF16)],
        compiler_params=_cparams(2),
        name="na_attn",
    )(q, k, v, cache_k, cache_v, bias)


def _about_kernel(*refs, n_x, n_p_tiles, nc, rc):
    g_ref, ap_ref, as_ref = refs[:3]
    x_refs = refs[3:3 + n_x]
    mod_ref, gw_ref, gb_ref, w_ref, o_ref, a_s = refs[3 + n_x:]
    i = pl.program_id(0)
    s5w = g_ref.shape[1]
    tm, d = o_ref.shape
    g = g_ref[...]
    gl = _dot(g.astype(BF16), gw_ref[...]) + gb_ref[...]
    s5o = (g * jax.nn.sigmoid(gl)).astype(BF16)

    def stage(a_ref):
        a_s[...] = a_ref[...]
    _for_part(i, n_p_tiles, (ap_ref, as_ref), stage)
    a = a_s[...]
    for n0 in range(0, d, nc):
        y = _dot(s5o, w_ref[0:s5w, n0:n0 + nc]) + _dot(a, w_ref[s5w:, n0:n0 + nc])
        o_ref[:, n0:n0 + nc] = mod_ref[2:3, n0:n0 + nc] * y

    def residual(x_ref):
        def body(r, carry):
            rows = pl.ds(pl.multiple_of(r * rc, rc), rc)
            o_ref[rows, :] += x_ref[rows, :]
            return carry
        lax.fori_loop(0, tm // rc, body, 0)
    _for_part(i, n_p_tiles, x_refs, residual)


def _about_call(g, attn_p, attn_s, xs, modt, glu_w, glu_b, w_out, *, tm, n_p_tiles):
    m = sum(x.shape[0] for x in xs)
    d = xs[0].shape[1]
    s5w = g.shape[1]
    naw = attn_p.shape[1]
    nc = min(512, d)
    row = lambda i: (i, 0)
    return pl.pallas_call(
        functools.partial(_about_kernel, n_x=len(xs), n_p_tiles=n_p_tiles, nc=nc, rc=min(128, tm)),
        grid=(m // tm,),
        in_specs=[pl.BlockSpec((tm, s5w), row)] + _x_specs(2, tm, naw, n_p_tiles)
        + _x_specs(len(xs), tm, d, n_p_tiles) + [
            pl.BlockSpec((None, 8, d), lambda i: (i, 0, 0)),
            _const_spec((s5w, s5w), lambda i: (0, 0)),
            _const_spec((1, s5w), lambda i: (0, 0)),
            _const_spec((s5w + naw, d), lambda i: (0, 0)),
        ],
        out_specs=pl.BlockSpec((tm, d), row),
        out_shape=jax.ShapeDtypeStruct((m, d), F32),
        scratch_shapes=[pltpu.VMEM((tm, naw), BF16)],
        compiler_params=_cparams(1),
        name="ab_out_proj",
    )(g, attn_p, attn_s, *xs, modt, glu_w, glu_b, w_out)


def _mlp_kernel(x_ref, mod_ref, g_ref, w1_ref, w2_ref, o_ref, h_s, a_s, *, rc, nc1, nc2):
    @pl.when(pl.program_id(1) == 0)
    def _():
        _modulate_rows(x_ref, h_s, g_ref[...], mod_ref[3:4, :], mod_ref[4:5, :], rc, copy_ref=o_ref)

    h = h_s[...]
    tf = w1_ref.shape[1]
    ta = a_s.shape[1]
    d = o_ref.shape[1]
    for f0 in range(0, tf, ta):
        for c0 in range(0, ta, nc1):
            a = jnp.maximum(_dot(h, w1_ref[:, f0 + c0:f0 + c0 + nc1]), 0.0)
            a_s[:, c0:c0 + nc1] = (a * a).astype(a_s.dtype)
        a = a_s[...]
        for n0 in range(0, d, nc2):
            o_ref[:, n0:n0 + nc2] += mod_ref[5:6, n0:n0 + nc2] * _dot(a, w2_ref[f0:f0 + ta, n0:n0 + nc2])


def _mlp_call(x, modt, g, w1, w2, *, tm, tf, tile0=0, n_tiles=None):
    d = x.shape[1]
    n_tiles = x.shape[0] // tm if n_tiles is None else n_tiles
    m = n_tiles * tm
    d_ff = w1.shape[1]
    rc = min(128, tm)
    return pl.pallas_call(
        functools.partial(_mlp_kernel, rc=rc, nc1=min(256, tf), nc2=min(512, d)),
        grid=(n_tiles, d_ff // tf),
        in_specs=[
            pl.BlockSpec((tm, d), lambda i, f: (i + tile0, 0)),
            pl.BlockSpec((None, 8, d), lambda i, f: (i + tile0, 0, 0)),
            _const_spec((1, d), lambda i, f: (0, 0)),
            pl.BlockSpec((d, tf), lambda i, f: (0, f)),
            pl.BlockSpec((tf, d), lambda i, f: (f, 0)),
        ],
        out_specs=pl.BlockSpec((tm, d), lambda i, f: (i, 0)),
        out_shape=jax.ShapeDtypeStruct((m, d), F32),
        scratch_shapes=[pltpu.VMEM((tm, d), BF16), pltpu.VMEM((tm, min(512, tf)), BF16)],
        compiler_params=_cparams(2),
        name="mlp",
    )(x, modt, g, w1, w2)


def _convin_kernel(x_ref, mod_ref, g_ref, wb_ref, wc_ref, wx_ref, gb_ref, z_ref, h_s, *, rc):
    @pl.when(pl.program_id(1) == 0)
    def _():
        _modulate_rows(x_ref, h_s, g_ref[...], mod_ref[0:1, :], mod_ref[1:2, :], rc)

    h = h_s[...]
    gb_ref[...] = _dot(h, wb_ref[...]).astype(gb_ref.dtype)
    z_ref[...] = (_dot(h, wc_ref[...]) * _dot(h, wx_ref[...])).astype(z_ref.dtype)


def _convin_call(x, modt, g, w_in, *, tm, tn):
    m, d = x.shape
    nb = d // tn
    rc = min(128, tm)
    out = pl.BlockSpec((tm, tn), lambda i, j: (i, j))
    return pl.pallas_call(
        functools.partial(_convin_kernel, rc=rc),
        grid=(m // tm, nb),
        in_specs=[
            pl.BlockSpec((tm, d), lambda i, j: (i, 0)),
            pl.BlockSpec((None, 8, d), lambda i, j: (i, 0, 0)),
            _const_spec((1, d), lambda i, j: (0, 0)),
            pl.BlockSpec((d, tn), lambda i, j: (0, j)),
            pl.BlockSpec((d, tn), lambda i, j: (0, nb + j)),
            pl.BlockSpec((d, tn), lambda i, j: (0, 2 * nb + j)),
        ],
        out_specs=[out, out],
        out_shape=[jax.ShapeDtypeStruct((m, d), BF16), jax.ShapeDtypeStruct((m, d), BF16)],
        scratch_shapes=[pltpu.VMEM((tm, d), BF16)],
        compiler_params=_cparams(2),
        name="conv_in_proj",
    )(x, modt, g, w_in, w_in, w_in)


def _convout_kernel(gb_ref, z_ref, zp_ref, zn_ref, x_ref, mod_ref, cw_ref, cb_ref, w_ref, o_ref, t_s,
                    *, n_p_tiles, lp, ls, cc, nc):
    i = pl.program_id(0)
    tm, d = z_ref.shape
    halo = zp_ref.shape[0]
    seq_mask = jnp.where(i < n_p_tiles, lp - 1, ls - 1)
    ridx = lax.broadcasted_iota(jnp.int32, (tm, 1), 0)
    pos = (i * tm + ridx) & seq_mask
    is_start = pos == 0
    is_end = pos == seq_mask
    for c0 in range(0, d, cc):
        cs = slice(c0, c0 + cc)
        z = z_ref[:, cs].astype(F32)
        z_prev = jnp.where(ridx == 0, zp_ref[halo - 1:halo, cs].astype(F32), pltpu.roll(z, 1, 0))
        z_prev = jnp.where(is_start, 0.0, z_prev)
        z_next = jnp.where(ridx == tm - 1, zn_ref[0:1, cs].astype(F32), pltpu.roll(z, tm - 1, 0))
        z_next = jnp.where(is_end, 0.0, z_next)
        conv = z_prev * cw_ref[0:1, cs] + z * cw_ref[1:2, cs] + z_next * cw_ref[2:3, cs] + cb_ref[:, cs]
        t_s[:, cs] = (gb_ref[:, cs].astype(F32) * conv).astype(t_s.dtype)
    t = t_s[...]
    for n0 in range(0, d, nc):
        o_ref[:, n0:n0 + nc] = x_ref[:, n0:n0 + nc] + mod_ref[2:3, n0:n0 + nc] * _dot(t, w_ref[:, n0:n0 + nc])


def _convout_call(gb, z, x, modt, conv_w, conv_b, w_out, *, tm, n_p_tiles, lp, ls):
    m, d = x.shape
    halo = 16
    assert tm % halo == 0 and lp & (lp - 1) == 0 and ls & (ls - 1) == 0
    assert (n_p_tiles * tm) % ls == 0 or True
    hb = tm // halo
    last = m // halo - 1
    row = lambda i: (i, 0)
    return pl.pallas_call(
        functools.partial(_convout_kernel, n_p_tiles=n_p_tiles, lp=lp, ls=ls,
                          cc=min(256, d), nc=min(512, d)),
        grid=(m // tm,),
        in_specs=[
            pl.BlockSpec((tm, d), row),
            pl.BlockSpec((tm, d), row),
            pl.BlockSpec((halo, d), lambda i: (jnp.maximum(i * hb - 1, 0), 0)),
            pl.BlockSpec((halo, d), lambda i: (jnp.minimum((i + 1) * hb, last), 0)),
            pl.BlockSpec((tm, d), row),
            pl.BlockSpec((None, 8, d), lambda i: (i, 0, 0)),
            _const_spec((8, d), lambda i: (0, 0)),
            _const_spec((1, d), lambda i: (0, 0)),
            _const_spec((d, d), lambda i: (0, 0)),
        ],
        out_specs=pl.BlockSpec((tm, d), row),
        out_shape=jax.ShapeDtypeStruct((m, d), F32),
        scratch_shapes=[pltpu.VMEM((tm, d), BF16)],
        compiler_params=_cparams(1),
        name="conv_out_proj",
    )(gb, z, z, z, x, modt, conv_w, conv_b, w_out)


def _tile_rows(m_p, m_s, l_s, n_b, tm):
    assert m_p % tm == 0 and l_s % tm == 0
    return np.concatenate([np.full(m_p // tm, n_b), np.repeat(np.arange(n_b), l_s // tm)])


def _pick_tile(pref, m_p, l_s):
    tm = pref
    while m_p % tm or l_s % tm:
        tm //= 2
    return tm


def kernel(x_prompt, x_sample, c, cache_k, cache_v, state_ssm_re, state_ssm_im, c_ctx, ada_w, ada_b, norm1_g, norm2_g, ab_w_in, ab_w_out, s5_lam_re, s5_lam_im, s5_log_dt, s5_b_re, s5_b_im, s5_c_re, s5_c_im, s5_d, s5_glu_w, s5_glu_b, q_norm_g, k_norm_g, na_rpb, conv_w_in, conv_w, conv_b, conv_w_out, mlp_w1, mlp_w2):
    n_bp, l_p, d = x_prompt.shape
    n_bs, l_s, _ = x_sample.shape
    depth = ada_w.shape[0]
    m_p, m_s = n_bp * l_p, n_bs * l_s
    m = m_p + m_s
    n_g, n_state = s5_lam_re.shape[2], s5_lam_re.shape[3]
    s5w = n_g * S5_GROUP_CH
    naw = (ab_w_in.shape[2] - s5w) // 3
    n_h = naw // HEAD_DIM
    past = cache_k.shape[2]
    d_ff = mlp_w1.shape[2]
    assert m_p % l_s == 0, "latent sequences must start on a sequence-length row block"
    assert l_p % S5_CHUNK == 0 and l_s % S5_CHUNK == 0
    c_p, c_s = l_p // S5_CHUNK, l_s // S5_CHUNK
    assert c_p & (c_p - 1) == 0 and c_s & (c_s - 1) == 0
    r_p, r_s = m_p // S5_CHUNK, m_s // S5_CHUNK
    width = S5_CHUNK * S5_GROUP_CH

    xs = [x_prompt.reshape(m_p, d), x_sample.reshape(m_s, d)]

    n_rows = -(-(n_bs + 1) // 16) * 16
    cvec = jnp.concatenate([c, c_ctx[None], jnp.zeros((n_rows - n_bs - 1, d), F32)], axis=0)
    mod = _ada_call(cvec, ada_w, ada_b)

    def mod_tiles(layer, tm):
        t = mod[layer][_tile_rows(m_p, m_s, l_s, n_bs, tm)].reshape(m // tm, 6, d)
        return jnp.pad(t, ((0, 0), (0, 2), (0, 0)))

    tm_big = _pick_tile(1024, m_p, l_s)
    tm_mid = _pick_tile(512, m_p, l_s)
    tf = min(1024, d_ff)
    tn_conv = min(512, d)
    perm_in, perm_out = _chunk_perm(min(8, n_g))

    n_pw = max(int(math.log2(c_s)), 1)
    w_u, w_y, pw = _s5prep_call(s5_lam_re, s5_lam_im, s5_log_dt, s5_b_re, s5_b_im, s5_c_re, s5_c_im,
                                s5_d, n_pw)
    n_h0 = -(-n_bs // 8) * 8
    h0_all = jnp.stack([state_ssm_re, state_ssm_im], axis=-2)
    h0_all = h0_all.transpose(1, 3, 0, 2, 4, 5).reshape(-1, n_g, n_bs, 4 * n_state)
    h0_all = jnp.pad(h0_all, ((0, 0), (0, 0), (0, n_h0 - n_bs), (0, 0)))
    cache_k4 = cache_k.reshape(n_bs, -1, past, naw)
    cache_v4 = cache_v.reshape(n_bs, -1, past, naw)

    new_k, new_v, new_re, new_im = [], [], [], []
    for layer in range(depth):
        g1 = norm1_g[layer].reshape(1, d)
        g2 = norm2_g[layer].reshape(1, d)
        if layer % 2 == 0:
            e = layer // 2
            u, q, k, v = _inproj_call(xs, mod_tiles(layer, tm_mid), g1, ab_w_in[e].astype(BF16),
                                      q_norm_g[e].reshape(1, HEAD_DIM), k_norm_g[e].reshape(1, HEAD_DIM),
                                      tm=tm_mid, s5w=s5w, naw=naw, n_p_tiles=m_p // tm_mid)
            new_k.append(k[:m_p].reshape(n_bp, l_p, n_h, HEAD_DIM))
            new_v.append(v[:m_p].reshape(n_bp, l_p, n_h, HEAD_DIM))
            g_tok, z_all = _s5_call(u, perm_in, perm_out, w_u[e], w_y[e], pw[e], h0_all[e],
                                    l_tile=l_s, n_p_tiles=m_p // l_s, cp=c_p, cs=c_s)
            z_p = z_all[:, :r_p].reshape(n_g, n_bp, c_p, 2, 2, n_state)
            fin = jnp.stack([z_p[:, :, c_p - 1, 0], z_p[:, :, 0, 1]], axis=2)
            new_re.append(fin[:, :, :, 0].transpose(1, 2, 0, 3))
            new_im.append(fin[:, :, :, 1].transpose(1, 2, 0, 3))
            attn_p = _ctx_attn_call(q, k, v, n_seq=n_bp, seq_len=l_p)
            bias = _na_bias_call(na_rpb[e])
            attn_s = _na_call(q, k, v, cache_k4, cache_v4, bias, layer_e=e, n_b=n_bs, seq_len=l_s,
                              row_blk0=m_p // l_s)
            x = _about_call(g_tok, attn_p, attn_s, xs, mod_tiles(layer, tm_mid), s5_glu_w[e].astype(BF16),
                            s5_glu_b[e].reshape(1, s5w), ab_w_out[e].astype(BF16), tm=tm_mid,
                            n_p_tiles=m_p // tm_mid)
        else:
            x = xs[0]
            o = layer // 2
            gb, z = _convin_call(x, mod_tiles(layer, tm_big), g1, conv_w_in[o].astype(BF16),
                                 tm=tm_big, tn=tn_conv)
            cw8 = jnp.pad(conv_w[o], ((0, 5), (0, 0)))
            x = _convout_call(gb, z, x, mod_tiles(layer, tm_mid), cw8, conv_b[o].reshape(1, d),
                              conv_w_out[o].astype(BF16), tm=tm_mid, n_p_tiles=m_p // tm_mid,
                              lp=l_p, ls=l_s)
        mlp = functools.partial(_mlp_call, x, mod_tiles(layer, tm_big), g2, mlp_w1[layer].astype(BF16),
                                mlp_w2[layer].astype(BF16), tm=tm_big, tf=tf)
        if layer < depth - 1:
            xs = [mlp()]
        else:
            y_prompt = mlp(tile0=0, n_tiles=m_p // tm_big).reshape(n_bp, l_p, d)
            y_sample = mlp(tile0=m_p // tm_big, n_tiles=m_s // tm_big).reshape(n_bs, l_s, d)

    return (y_prompt, y_sample, jnp.stack(new_k, axis=1), jnp.stack(new_v, axis=1),
            jnp.stack(new_re, axis=1), jnp.stack(new_im, axis=1))
```

```python
import functools
import math

import numpy as np
import jax
import jax.numpy as jnp
from jax import lax
from jax.experimental import pallas as pl
from jax.experimental.pallas import tpu as pltpu

F32 = jnp.float32
BF16 = jnp.bfloat16

NORM_EPS = 1e-6
NEG_INF = -1e30

S5_GROUP_CH = 16
HEAD_DIM = 128
GRID_W = 64
NA_WIN_R = 8
NA_WIN_C = 16
S5_CHUNK = 16
NA_QROWS = 4
NA_KROWS = NA_QROWS + NA_WIN_R
ROW_CHUNK = 256

VMEM_LIMIT_BYTES = 62 * 1024 * 1024


def _cparams(n_axes):
    return pltpu.CompilerParams(dimension_semantics=("arbitrary",) * n_axes,
                                vmem_limit_bytes=VMEM_LIMIT_BYTES)


def _const_spec(shape, index_map):
    return pl.BlockSpec(shape, index_map, pipeline_mode=pl.Buffered(1))


def _dot(a, b):
    return jnp.dot(a, b, preferred_element_type=F32)


def _dot_nt(a, b):
    return lax.dot_general(a, b, (((1,), (1,)), ((), ())), preferred_element_type=F32)


def _dot_split(a, b):
    a_hi = a.astype(BF16)
    a_lo = (a - a_hi.astype(F32)).astype(BF16)
    b_hi = b.astype(BF16)
    b_lo = (b - b_hi.astype(F32)).astype(BF16)
    return _dot(a_hi, b_hi) + _dot(a_hi, b_lo) + _dot(a_lo, b_hi)


def _modulate(x, g, shift, scale):
    ms = jnp.mean(x * x, axis=-1, keepdims=True)
    y = x * lax.rsqrt(ms + NORM_EPS) * g
    return y * (1.0 + scale) + shift


def _ada_kernel(c_ref, w_ref, b_ref, o_ref, *, nc):
    cv = c_ref[...]
    sc = (cv * jax.nn.sigmoid(cv)).astype(BF16)
    tn = w_ref.shape[1]
    for n0 in range(0, tn, nc):
        w = w_ref[:, n0:n0 + nc].astype(BF16)
        o_ref[:, n0:n0 + nc] = _dot(sc, w) + b_ref[:, n0:n0 + nc]


def _ada_call(cvec, ada_w, ada_b):
    depth, d, n6 = ada_w.shape
    rows = cvec.shape[0]
    tn = 1536 if n6 % 1536 == 0 else n6
    nc = 512 if tn % 512 == 0 else tn
    return pl.pallas_call(
        functools.partial(_ada_kernel, nc=nc),
        grid=(depth, n6 // tn),
        in_specs=[
            pl.BlockSpec((rows, d), lambda l, j: (0, 0)),
            pl.BlockSpec((None, d, tn), lambda l, j: (l, 0, j)),
            pl.BlockSpec((None, 1, tn), lambda l, j: (l, 0, j)),
        ],
        out_specs=pl.BlockSpec((None, rows, tn), lambda l, j: (l, 0, j)),
        out_shape=jax.ShapeDtypeStruct((depth, rows, n6), F32),
        compiler_params=_cparams(2),
        name="ada_params",
    )(cvec, ada_w, ada_b.reshape(depth, 1, n6))


def _x_specs(n_x, tm, d, n_p_tiles):
    if n_x == 1:
        return [pl.BlockSpec((tm, d), lambda i: (i, 0))]
    return [pl.BlockSpec((tm, d), lambda i: (jnp.minimum(i, n_p_tiles - 1), 0)),
            pl.BlockSpec((tm, d), lambda i: (jnp.maximum(i - n_p_tiles, 0), 0))]


def _for_part(i, n_p_tiles, refs, fn):
    if len(refs) == 1:
        fn(refs[0])
    else:
        pl.when(i < n_p_tiles)(lambda: fn(refs[0]))
        pl.when(i >= n_p_tiles)(lambda: fn(refs[1]))


def _inproj_kernel(*refs, n_x, n_p_tiles, s5w, naw, rc):
    x_refs = refs[:n_x]
    mod_ref, g_ref, w_ref, qg_ref, kg_ref, u_ref, q_ref, k_ref, v_ref, h_s = refs[n_x:]
    tm = u_ref.shape[0]
    hd = HEAD_DIM
    cw = min(4 * hd, naw)
    g, shift, scale = g_ref[...], mod_ref[0:1, :], mod_ref[1:2, :]
    qg = qg_ref[...]
    kg = kg_ref[...]

    def head_norm(t, gain):
        return t * lax.rsqrt(jnp.mean(t * t, axis=-1, keepdims=True) + NORM_EPS) * gain

    def run(x_ref):
        for r0 in range(0, tm, rc):
            rows = slice(r0, r0 + rc)
            h_s[rows, :] = _modulate(x_ref[rows, :], g, shift, scale).astype(h_s.dtype)
            h = h_s[rows, :]
            u_ref[rows, :] = _dot(h, w_ref[:, 0:s5w]).astype(u_ref.dtype)
            for c0 in range(0, naw, cw):
                qc = _dot(h, w_ref[:, s5w + c0:s5w + c0 + cw])
                for j in range(cw // hd):
                    q_ref[rows, c0 + j * hd:c0 + (j + 1) * hd] = head_norm(
                        qc[:, j * hd:(j + 1) * hd], qg).astype(q_ref.dtype)
                kc = _dot(h, w_ref[:, s5w + naw + c0:s5w + naw + c0 + cw])
                for j in range(cw // hd):
                    k_ref[rows, c0 + j * hd:c0 + (j + 1) * hd] = head_norm(
                        kc[:, j * hd:(j + 1) * hd], kg).astype(k_ref.dtype)
                v_ref[rows, c0:c0 + cw] = _dot(
                    h, w_ref[:, s5w + 2 * naw + c0:s5w + 2 * naw + c0 + cw]).astype(v_ref.dtype)

    _for_part(pl.program_id(0), n_p_tiles, x_refs, run)


def _inproj_call(xs, modt, g, w_in, qg, kg, *, tm, s5w, naw, n_p_tiles):
    m = sum(x.shape[0] for x in xs)
    d = xs[0].shape[1]
    n_in = w_in.shape[1]
    rc = min(ROW_CHUNK, tm)
    row = lambda i: (i, 0)
    return pl.pallas_call(
        functools.partial(_inproj_kernel, n_x=len(xs), n_p_tiles=n_p_tiles, s5w=s5w, naw=naw, rc=rc),
        grid=(m // tm,),
        in_specs=_x_specs(len(xs), tm, d, n_p_tiles) + [
            pl.BlockSpec((None, 8, d), lambda i: (i, 0, 0)),
            _const_spec((1, d), lambda i: (0, 0)),
            _const_spec((d, n_in), lambda i: (0, 0)),
            _const_spec((1, HEAD_DIM), lambda i: (0, 0)),
            _const_spec((1, HEAD_DIM), lambda i: (0, 0)),
        ],
        out_specs=[
            pl.BlockSpec((tm, s5w), row),
            pl.BlockSpec((tm, naw), row),
            pl.BlockSpec((tm, naw), row),
            pl.BlockSpec((tm, naw), row),
        ],
        out_shape=[
            jax.ShapeDtypeStruct((m, s5w), F32),
            jax.ShapeDtypeStruct((m, naw), BF16),
            jax.ShapeDtypeStruct((m, naw), F32),
            jax.ShapeDtypeStruct((m, naw), F32),
        ],
        scratch_shapes=[pltpu.VMEM((tm, d), BF16)],
        compiler_params=_cparams(1),
        name="ab_in_proj",
    )(*xs, modt, g, w_in, qg, kg)


def _s5prep_kernel(row_ref, col_ref, btr_ref, bti_ref, cer_ref, cei_ref, d_ref,
                   wu_ref, wy_ref, pw_ref, *, n_state, n_pw):
    t_chunk = S5_CHUNK
    width = t_chunk * S5_GROUP_CH
    p = n_state
    lg_ch = int(math.log2(S5_GROUP_CH))
    r_i = lax.broadcasted_iota(jnp.int32, (width, 1), 0)
    c_i = lax.broadcasted_iota(jnp.int32, (1, width), 1)
    s_i = r_i >> lg_ch
    t_i = c_i >> lg_ch
    n_pow = -(-(t_chunk + 1) // 8) * 8

    def cmul(ar, ai, br, bi):
        return ar * br - ai * bi, ar * bi + ai * br

    acc = jnp.zeros((width, width), F32)
    for d in range(2):
        lr = row_ref[d, 0:1, 0:p]
        li = row_ref[d, 1:2, 0:p]
        dt = jnp.exp(row_ref[d, 2:3, 0:p])
        ar = lr * dt
        ai = li * dt
        er = jnp.exp(ar)
        abr = er * jnp.cos(ai)
        abi = er * jnp.sin(ai)
        den = lr * lr + li * li
        nr = abr - 1.0
        f_re = (nr * lr + abi * li) / den
        f_im = (abi * lr - nr * li) / den
        bb_re, bb_im = cmul(f_re, f_im, btr_ref[d], bti_ref[d])

        lrc = col_ref[d, :, 0:1]
        lic = col_ref[d, :, 1:2]
        dtc = jnp.exp(col_ref[d, :, 2:3])
        arc = lrc * dtc
        aic = lic * dtc
        ce_re = cer_ref[d]
        ce_im = cei_ref[d]

        k_r = lax.broadcasted_iota(jnp.int32, (n_pow, 1), 0).astype(F32)
        k_c = lax.broadcasted_iota(jnp.int32, (1, 128), 1).astype(F32)
        mag_r = jnp.exp(k_r * ar)
        tr_re, tr_im = mag_r * jnp.cos(k_r * ai), mag_r * jnp.sin(k_r * ai)
        mag_c = jnp.exp(arc * k_c)
        tc_re, tc_im = mag_c * jnp.cos(aic * k_c), mag_c * jnp.sin(aic * k_c)

        def pow_row(e, e_max):
            re = jnp.zeros((width, p), F32)
            im = jnp.zeros((width, p), F32)
            for kk in range(e_max + 1):
                hit = e == kk
                re = jnp.where(hit, tr_re[kk:kk + 1, :], re)
                im = jnp.where(hit, tr_im[kk:kk + 1, :], im)
            return re, im

        def pow_col(e, e_max):
            re = jnp.zeros((p, width), F32)
            im = jnp.zeros((p, width), F32)
            for kk in range(e_max + 1):
                hit = e == kk
                re = jnp.where(hit, tc_re[:, kk:kk + 1], re)
                im = jnp.where(hit, tc_im[:, kk:kk + 1], im)
            return re, im

        terms = [(jnp.zeros_like(s_i), jnp.zeros_like(t_i), s_i == t_i, 0)]
        blk = 2
        while blk <= t_chunk:
            lb = int(math.log2(blk))
            half = blk // 2
            mid_s = ((s_i >> lb) << lb) + half
            mid_t = ((t_i >> lb) << lb) + half
            same = (s_i >> lb) == (t_i >> lb)
            if d == 0:
                e_s, ok_s = mid_s - s_i, s_i < mid_s
                e_t, ok_t = t_i - mid_t, t_i >= mid_t
            else:
                e_s, ok_s = s_i - mid_s, s_i >= mid_s
                e_t, ok_t = mid_t - t_i, t_i < mid_t
            terms.append((jnp.maximum(e_s, 0), jnp.maximum(e_t, 0), same & ok_s & ok_t, half))
            blk *= 2
        for e_s, e_t, mask, e_max in terms:
            l_re, l_im = cmul(*pow_row(e_s, e_max), bb_re, bb_im)
            r_re, r_im = cmul(*pow_col(e_t, e_max), ce_re, ce_im)
            term = _dot_split(l_re, r_re) - _dot_split(l_im, r_im)
            acc = acc + jnp.where(mask, term, 0.0)

        if d == 0:
            ws_re, ws_im = cmul(*pow_row(t_chunk - 1 - s_i, t_chunk), bb_re, bb_im)
            ca_re, ca_im = cmul(*pow_col(t_i + 1, t_chunk), ce_re, ce_im)
        else:
            ws_re, ws_im = cmul(*pow_row(s_i, t_chunk), bb_re, bb_im)
            ca_re, ca_im = cmul(*pow_col(t_chunk - t_i, t_chunk), ce_re, ce_im)
        base = width + d * 2 * p
        wu_ref[:, base:base + p] = ws_re.astype(wu_ref.dtype)
        wu_ref[:, base + p:base + 2 * p] = ws_im.astype(wu_ref.dtype)
        wy_ref[d * 2 * p:d * 2 * p + p, :] = ca_re.astype(wy_ref.dtype)
        wy_ref[d * 2 * p + p:(d + 1) * 2 * p, :] = (-ca_im).astype(wy_ref.dtype)

        lr2 = row_ref[d, 0:1, :]
        li2 = row_ref[d, 1:2, :]
        dt2 = jnp.exp(row_ref[d, 2:3, :])
        k16 = float(t_chunk)
        mag = jnp.exp(k16 * (lr2 * dt2))
        pr = mag * jnp.cos(k16 * (li2 * dt2))
        pi = mag * jnp.sin(k16 * (li2 * dt2))
        lane = lax.broadcasted_iota(jnp.int32, pr.shape, 1)
        sign = jnp.where(lane < p, -1.0, 1.0)
        for i in range(n_pw):
            pw_ref[i * 4 + d * 2:i * 4 + d * 2 + 1, :] = pr
            pw_ref[i * 4 + d * 2 + 1:i * 4 + d * 2 + 2, :] = pi * sign
            pr, pi = pr * pr - pi * pi, 2.0 * pr * pi

    wu_ref[:, 0:width] = (acc + jnp.where(r_i == c_i, d_ref[...], 0.0)).astype(wu_ref.dtype)


def _s5prep_call(lam_re, lam_im, log_dt, b_re, b_im, c_re, c_im, d_skip, n_pw):
    n_e, _, n_g, p = lam_re.shape
    n_ch = S5_GROUP_CH
    width = S5_CHUNK * n_ch
    assert 2 * p == 128, "state rows are packed as [re | im] in one 128-lane tile"
    dtb = jnp.broadcast_to(log_dt[..., None], lam_re.shape)
    zeros = jnp.zeros_like(lam_re)
    rowp = jnp.stack([lam_re, lam_im, dtb] + [zeros] * 5, axis=-2)
    rowp = jnp.concatenate([rowp, rowp], axis=-1)
    colp = jnp.stack([lam_re, lam_im, dtb] + [zeros] * 5, axis=-1)
    bt = lambda b: jnp.tile(jnp.swapaxes(b, -1, -2), (1, 1, 1, S5_CHUNK, 1))
    ce = lambda c: jnp.tile(jnp.swapaxes(c, -1, -2), (1, 1, 1, 1, S5_CHUNK))
    d_row = jnp.tile(d_skip.reshape(n_e, n_g, 1, n_ch), (1, 1, 1, S5_CHUNK))

    def dspec(shape):
        return pl.BlockSpec((None, 2, None) + shape, lambda e, g: (e, 0, g, 0, 0))

    def ospec(shape):
        return pl.BlockSpec((None, None) + shape, lambda e, g: (e, g, 0, 0))

    return pl.pallas_call(
        functools.partial(_s5prep_kernel, n_state=p, n_pw=n_pw),
        grid=(n_e, n_g),
        in_specs=[dspec((8, 2 * p)), dspec((p, 8)), dspec((width, p)), dspec((width, p)),
                  dspec((p, width)), dspec((p, width)), ospec((1, width))],
        out_specs=[ospec((width, 2 * width)), ospec((width, width)), ospec((4 * n_pw, 2 * p))],
        out_shape=[
            jax.ShapeDtypeStruct((n_e, n_g, width, 2 * width), BF16),
            jax.ShapeDtypeStruct((n_e, n_g, width, width), BF16),
            jax.ShapeDtypeStruct((n_e, n_g, 4 * n_pw, 2 * p), F32),
        ],
        compiler_params=_cparams(2),
        name="s5_prep",
    )(rowp, colp, bt(b_re), bt(b_im), ce(c_re), ce(c_im), d_row)


def _gelu_tanh(y):
    return 0.5 * y * (1.0 + jnp.tanh(0.7978845608028654 * (y + 0.044715 * (y * y * y))))


def _s5_kernel(u_ref, pin_ref, pout_ref, wu_ref, wy_ref, pw_ref, h0_ref, g_ref, z_ref, ph_s, ug_s, yg_s,
               *, n_p_tiles, cp, cs):
    tile = pl.program_id(1)
    t_chunk = S5_CHUNK
    n = u_ref.shape[0] // t_chunk
    ngb, width = wy_ref.shape[0], wy_ref.shape[1]
    half = width // 2
    lanes = u_ref.shape[1]

    def crot(a1, a2, x):
        return a1 * x + a2 * pltpu.roll(x, half // 2, 1)

    x_cat = jnp.concatenate(
        [u_ref[pl.ds(s, n, stride=t_chunk), :].astype(BF16) for s in range(t_chunk)], axis=1)
    for gi in range(ngb):
        ug_s[gi] = _dot(x_cat, pin_ref[:, gi * width:(gi + 1) * width]).astype(ug_s.dtype)

    def process(gi, cseq, h0, ph):
        ys = _dot(ug_s[gi], wu_ref[gi])
        wy = wy_ref[gi]
        pos = lax.broadcasted_iota(jnp.int32, (n, 1), 0) & (cseq - 1)
        zs, xs = [], []
        for d in range(2):
            z = ys[:, width + d * half:width + (d + 1) * half]
            edge = (pos == 0) if d == 0 else (pos == cseq - 1)
            if h0 is not None:
                z = z + jnp.where(edge, ph[:, d * half:(d + 1) * half], 0.0)
            k = 1
            i = 0
            while k < cseq:
                a1 = pw_ref[gi, pl.ds(i * 4 + d * 2, 1), :]
                a2 = pw_ref[gi, pl.ds(i * 4 + d * 2 + 1, 1), :]
                if d == 0:
                    sh = pltpu.roll(z, k, 0)
                    valid = pos >= k
                else:
                    sh = pltpu.roll(z, n - k, 0)
                    valid = pos < cseq - k
                z = z + jnp.where(valid, crot(a1, a2, sh), 0.0)
                k *= 2
                i += 1
            x = pltpu.roll(z, 1, 0) if d == 0 else pltpu.roll(z, n - 1, 0)
            x_edge = 0.0 if h0 is None else h0[:, d * half:(d + 1) * half]
            xs.append(jnp.where(edge, x_edge, x))
            zs.append(z)
        x = jnp.concatenate(xs, axis=1).astype(BF16)
        y = ys[:, 0:width] + _dot(x, wy)
        yg_s[gi] = _gelu_tanh(y).astype(yg_s.dtype)
        z_ref[gi] = jnp.concatenate(zs, axis=1)

    @pl.when(tile < n_p_tiles)
    def _():
        def body(gi, carry):
            process(gi, cp, None, None)
            return carry
        lax.fori_loop(0, ngb, body, 0)

    @pl.when(tile >= n_p_tiles)
    def _():
        j = tile - n_p_tiles

        def body(gi, carry):
            h0 = h0_ref[gi]
            for d in range(2):
                ph_s[:, d * half:(d + 1) * half] = crot(
                    pw_ref[gi, pl.ds(d * 2, 1), :], pw_ref[gi, pl.ds(d * 2 + 1, 1), :],
                    h0[:, d * half:(d + 1) * half])
            process(gi, cs, h0_ref[gi, pl.ds(j, 1), :], ph_s[pl.ds(j, 1), :])
            return carry
        lax.fori_loop(0, ngb, body, 0)

    y_all = jnp.concatenate([yg_s[gi] for gi in range(ngb)], axis=1)
    for t in range(t_chunk):
        g_ref[pl.ds(t, n, stride=t_chunk), :] = _dot(y_all, pout_ref[:, t * lanes:(t + 1) * lanes])


def _s5_call(u, perm_in, perm_out, w_u, w_y, pw, h0, *, l_tile, n_p_tiles, cp, cs):
    m, s5w = u.shape
    n_g, width = w_y.shape[0], w_y.shape[1]
    ngb = min(8, n_g)
    lanes = ngb * S5_GROUP_CH
    assert lanes == 128 and n_g % ngb == 0 and m % l_tile == 0
    n = l_tile // S5_CHUNK
    n_h0 = h0.shape[1]
    tok = pl.BlockSpec((l_tile, lanes), lambda cb, t: (t, cb))
    grp = lambda shape: pl.BlockSpec((ngb,) + shape, lambda cb, t: (cb, 0, 0))
    return pl.pallas_call(
        functools.partial(_s5_kernel, n_p_tiles=n_p_tiles, cp=cp, cs=cs),
        grid=(n_g // ngb, m // l_tile),
        in_specs=[tok,
                  _const_spec(perm_in.shape, lambda cb, t: (0, 0)),
                  _const_spec(perm_out.shape, lambda cb, t: (0, 0)),
                  grp((width, 2 * width)), grp((width, width)), grp(pw.shape[1:]), grp((n_h0, width))],
        out_specs=[tok, pl.BlockSpec((ngb, n, width), lambda cb, t: (cb, t, 0))],
        out_shape=[jax.ShapeDtypeStruct((m, s5w), F32),
                   jax.ShapeDtypeStruct((n_g, m // S5_CHUNK, width), F32)],
        scratch_shapes=[pltpu.VMEM((n_h0, width), F32),
                        pltpu.VMEM((ngb, n, width), BF16),
                        pltpu.VMEM((ngb, n, width), BF16)],
        compiler_params=_cparams(2),
        name="s5_core",
    )(u, perm_in, perm_out, w_u, w_y, pw, h0)


def _chunk_perm(ngb):
    size = S5_CHUNK * ngb * S5_GROUP_CH
    src = np.arange(size).reshape(S5_CHUNK, ngb, S5_GROUP_CH).transpose(1, 0, 2).reshape(-1)
    perm_in = (jnp.arange(size, dtype=jnp.int32)[:, None] == jnp.asarray(src, jnp.int32)[None, :]).astype(BF16)
    return perm_in, perm_in.T


def _ctx_attn_kernel(q_ref, k_ref, v_ref, o_ref, *, scale):
    hd = HEAD_DIM
    for h in range(q_ref.shape[1] // hd):
        sl = slice(h * hd, (h + 1) * hd)
        s = _dot_nt(q_ref[:, sl], k_ref[:, sl].astype(BF16)) * scale
        m = jnp.max(s, axis=-1, keepdims=True)
        p = jnp.exp(s - m)
        l = jnp.sum(p, axis=-1, keepdims=True)
        o = _dot(p.astype(BF16), v_ref[:, sl].astype(BF16)) / l
        o_ref[:, sl] = o.astype(o_ref.dtype)


def _ctx_attn_call(q, k, v, *, n_seq, seq_len):
    naw = q.shape[1]
    spec = pl.BlockSpec((seq_len, naw), lambda b: (b, 0))
    return pl.pallas_call(
        functools.partial(_ctx_attn_kernel, scale=HEAD_DIM ** -0.5),
        grid=(n_seq,),
        in_specs=[spec, spec, spec],
        out_specs=spec,
        out_shape=jax.ShapeDtypeStruct((n_seq * seq_len, naw), BF16),
        compiler_params=_cparams(1),
        name="ctx_attn",
    )(q, k, v)


def _na_bias_kernel(rpb_ref, o_ref, t_s):
    h = pl.program_id(0)
    n_dr = 2 * NA_WIN_R - 1
    n_dc = 2 * NA_WIN_C - 1
    w = GRID_W
    qc = lax.broadcasted_iota(jnp.int32, (w, w), 0)
    kc = lax.broadcasted_iota(jnp.int32, (w, w), 1)
    dc = kc - qc + (NA_WIN_C - 1)
    c0 = jnp.clip(qc - NA_WIN_C // 2, 0, w - NA_WIN_C)
    col_ok = (kc >= c0) & (kc < c0 + NA_WIN_C)
    for dr in range(n_dr):
        def pick(j, t, dr=dr):
            return jnp.where(dc == j, rpb_ref[h * (n_dr * n_dc) + dr * n_dc + j], t)
        t = lax.fori_loop(0, n_dc, pick, jnp.zeros((w, w), F32))
        t_s[dr] = jnp.where(col_ok, t, NEG_INF)
    neg = jnp.full((w, w), NEG_INF, F32)
    patterns = [(0, lambda i: 0), (NA_WIN_R // 2, lambda i: i), (NA_WIN_R, lambda i: NA_QROWS)]
    for pat, (r_rel, r0_rel) in enumerate(patterns):
        for i in range(NA_QROWS):
            for kr in range(NA_KROWS):
                dr = kr - i - r_rel + (NA_WIN_R - 1)
                valid = r0_rel(i) <= kr < r0_rel(i) + NA_WIN_R
                o_ref[pat, i * w:(i + 1) * w, kr * w:(kr + 1) * w] = t_s[dr] if valid else neg


def _na_bias_call(rpb):
    n_h = rpb.shape[0]
    nq = NA_QROWS * GRID_W
    nk = NA_KROWS * GRID_W
    return pl.pallas_call(
        _na_bias_kernel,
        grid=(n_h,),
        in_specs=[pl.BlockSpec(memory_space=pltpu.SMEM)],
        out_specs=pl.BlockSpec((None, 3, nq, nk), lambda h: (h, 0, 0, 0)),
        out_shape=jax.ShapeDtypeStruct((n_h, 3, nq, nk), F32),
        scratch_shapes=[pltpu.VMEM((2 * NA_WIN_R - 1, GRID_W, GRID_W), F32)],
        compiler_params=_cparams(1),
        name="na_bias",
    )(rpb.reshape(-1))


def _na_kernel(q_ref, k_ref, v_ref, kc_ref, vc_ref, bb_ref, o_ref, kb_s, vb_s, kcb_s, vcb_s,
               *, rows, scale):
    w = GRID_W
    nq = NA_QROWS * w
    nk = NA_KROWS * w
    n_blk = rows // NA_QROWS
    kb_s[...] = k_ref[...].astype(BF16)
    vb_s[...] = v_ref[...].astype(BF16)
    kcb_s[...] = kc_ref[...].astype(BF16)
    vcb_s[...] = vc_ref[...].astype(BF16)

    def body(blk, carry):
        q_rows = pl.ds(pl.multiple_of(blk * nq, nq), nq)
        k_base = jnp.clip(blk * NA_QROWS - NA_WIN_R // 2, 0, rows - NA_KROWS)
        k_rows = pl.ds(pl.multiple_of(k_base * w, nq), nk)
        pat = jnp.where(blk == 0, 0, jnp.where(blk == n_blk - 1, 2, 1))
        q = q_ref[q_rows, :]
        s_loc = _dot_nt(q, kb_s[k_rows, :]) * scale + bb_ref[pat]
        s_ctx = _dot_nt(q, kcb_s[...]) * scale
        m = jnp.maximum(jnp.max(s_loc, axis=-1, keepdims=True), jnp.max(s_ctx, axis=-1, keepdims=True))
        p_loc = jnp.exp(s_loc - m)
        p_ctx = jnp.exp(s_ctx - m)
        l = jnp.sum(p_loc, axis=-1, keepdims=True) + jnp.sum(p_ctx, axis=-1, keepdims=True)
        o = _dot(p_loc.astype(BF16), vb_s[k_rows, :]) + _dot(p_ctx.astype(BF16), vcb_s[...])
        o_ref[q_rows, :] = (o / l).astype(o_ref.dtype)
        return carry

    lax.fori_loop(0, n_blk, body, 0)


def _na_call(q, k, v, cache_k, cache_v, bias, *, layer_e, n_b, seq_len, row_blk0):
    naw = q.shape[1]
    n_h = naw // HEAD_DIM
    past = cache_k.shape[2]
    rows = seq_len // GRID_W
    assert rows % NA_QROWS == 0 and rows >= NA_KROWS
    tok = pl.BlockSpec((seq_len, HEAD_DIM), lambda b, h: (row_blk0 + b, h))
    ctx = pl.BlockSpec((None, None, past, HEAD_DIM), lambda b, h: (b, layer_e, 0, h))
    return pl.pallas_call(
        functools.partial(_na_kernel, rows=rows, scale=HEAD_DIM ** -0.5),
        grid=(n_b, n_h),
        in_specs=[tok, tok, tok, ctx, ctx,
                  pl.BlockSpec((None,) + bias.shape[1:], lambda b, h: (h, 0, 0, 0))],
        out_specs=pl.BlockSpec((seq_len, HEAD_DIM), lambda b, h: (b, h)),
        out_shape=jax.ShapeDtypeStruct((n_b * seq_len, naw), BF16),
        scratch_shapes=[pltpu.VMEM((seq_len, HEAD_DIM), BF16), pltpu.VMEM((seq_len, HEAD_DIM), BF16),
                        pltpu.VMEM((past, HEAD_DIM), BF16), pltpu.VMEM((past, HEAD_DIM), BF16)],
        compiler_params=_cparams(2),
        name="na_attn",
    )(q, k, v, cache_k, cache_v, bias)


def _about_kernel(*refs, n_x, n_p_tiles, nc, rc):
    g_ref, ap_ref, as_ref = refs[:3]
    x_refs = refs[3:3 + n_x]
    mod_ref, gw_ref, gb_ref, w_ref, o_ref = refs[3 + n_x:]
    s5w = g_ref.shape[1]
    tm, d = o_ref.shape

    def run(part):
        a_ref, x_ref = part
        for r0 in range(0, tm, rc):
            rows = slice(r0, r0 + rc)
            g = g_ref[rows, :]
            gl = _dot(g.astype(BF16), gw_ref[...]) + gb_ref[...]
            s5o = (g * jax.nn.sigmoid(gl)).astype(BF16)
            a = a_ref[rows, :]
            for n0 in range(0, d, nc):
                y = _dot(s5o, w_ref[0:s5w, n0:n0 + nc]) + _dot(a, w_ref[s5w:, n0:n0 + nc])
                o_ref[rows, n0:n0 + nc] = x_ref[rows, n0:n0 + nc] + mod_ref[2:3, n0:n0 + nc] * y

    _for_part(pl.program_id(0), n_p_tiles, [(ap_ref, x_refs[0]), (as_ref, x_refs[-1])], run)


def _about_call(g, attn_p, attn_s, xs, modt, glu_w, glu_b, w_out, *, tm, n_p_tiles):
    m = sum(x.shape[0] for x in xs)
    d = xs[0].shape[1]
    s5w = g.shape[1]
    naw = attn_p.shape[1]
    nc = min(512, d)
    row = lambda i: (i, 0)
    return pl.pallas_call(
        functools.partial(_about_kernel, n_x=len(xs), n_p_tiles=n_p_tiles, nc=nc, rc=min(ROW_CHUNK, tm)),
        grid=(m // tm,),
        in_specs=[pl.BlockSpec((tm, s5w), row)] + _x_specs(2, tm, naw, n_p_tiles)
        + _x_specs(len(xs), tm, d, n_p_tiles) + [
            pl.BlockSpec((None, 8, d), lambda i: (i, 0, 0)),
            _const_spec((s5w, s5w), lambda i: (0, 0)),
            _const_spec((1, s5w), lambda i: (0, 0)),
            _const_spec((s5w + naw, d), lambda i: (0, 0)),
        ],
        out_specs=pl.BlockSpec((tm, d), row),
        out_shape=jax.ShapeDtypeStruct((m, d), F32),
        compiler_params=_cparams(1),
        name="ab_out_proj",
    )(g, attn_p, attn_s, *xs, modt, glu_w, glu_b, w_out)


def _mlp_kernel(x_ref, mod_ref, g_ref, w1_ref, w2_ref, o_ref, h_s, a_s, *, rc, nc1, nc2):
    tm, d = o_ref.shape
    tf = w1_ref.shape[1]
    ta = a_s.shape[1]

    def chunk_dots(rows):
        h = h_s[rows, :]
        for f0 in range(0, tf, ta):
            for c0 in range(0, ta, nc1):
                a = jnp.maximum(_dot(h, w1_ref[:, f0 + c0:f0 + c0 + nc1]), 0.0)
                a_s[rows, c0:c0 + nc1] = (a * a).astype(a_s.dtype)
            a = a_s[rows, :]
            for n0 in range(0, d, nc2):
                o_ref[rows, n0:n0 + nc2] += mod_ref[5:6, n0:n0 + nc2] * _dot(a, w2_ref[f0:f0 + ta, n0:n0 + nc2])

    @pl.when(pl.program_id(1) == 0)
    def _():
        g, shift, scale = g_ref[...], mod_ref[3:4, :], mod_ref[4:5, :]
        for r0 in range(0, tm, rc):
            rows = slice(r0, r0 + rc)
            x = x_ref[rows, :]
            h_s[rows, :] = _modulate(x, g, shift, scale).astype(h_s.dtype)
            o_ref[rows, :] = x
            chunk_dots(rows)

    @pl.when(pl.program_id(1) != 0)
    def _():
        chunk_dots(slice(0, tm))


def _mlp_call(x, modt, g, w1, w2, *, tm, tf, tile0=0, n_tiles=None):
    d = x.shape[1]
    n_tiles = x.shape[0] // tm if n_tiles is None else n_tiles
    m = n_tiles * tm
    d_ff = w1.shape[1]
    rc = min(ROW_CHUNK, tm)
    return pl.pallas_call(
        functools.partial(_mlp_kernel, rc=rc, nc1=min(256, tf), nc2=min(512, d)),
        grid=(n_tiles, d_ff // tf),
        in_specs=[
            pl.BlockSpec((tm, d), lambda i, f: (i + tile0, 0)),
            pl.BlockSpec((None, 8, d), lambda i, f: (i + tile0, 0, 0)),
            _const_spec((1, d), lambda i, f: (0, 0)),
            pl.BlockSpec((d, tf), lambda i, f: (0, f)),
            pl.BlockSpec((tf, d), lambda i, f: (f, 0)),
        ],
        out_specs=pl.BlockSpec((tm, d), lambda i, f: (i, 0)),
        out_shape=jax.ShapeDtypeStruct((m, d), F32),
        scratch_shapes=[pltpu.VMEM((tm, d), BF16), pltpu.VMEM((tm, min(512, tf)), BF16)],
        compiler_params=_cparams(2),
        name="mlp",
    )(x, modt, g, w1, w2)


def _convin_kernel(x_ref, mod_ref, g_ref, w_ref, gb_ref, z_ref, h_s, *, rc, nc):
    tm, d = gb_ref.shape
    g, shift, scale = g_ref[...], mod_ref[0:1, :], mod_ref[1:2, :]
    for r0 in range(0, tm, rc):
        rows = slice(r0, r0 + rc)
        h_s[rows, :] = _modulate(x_ref[rows, :], g, shift, scale).astype(h_s.dtype)
        h = h_s[rows, :]
        for n0 in range(0, d, nc):
            gb_ref[rows, n0:n0 + nc] = _dot(h, w_ref[:, n0:n0 + nc]).astype(gb_ref.dtype)
            z_ref[rows, n0:n0 + nc] = (_dot(h, w_ref[:, d + n0:d + n0 + nc])
                                       * _dot(h, w_ref[:, 2 * d + n0:2 * d + n0 + nc])).astype(z_ref.dtype)


def _convin_call(x, modt, g, w_in, *, tm):
    m, d = x.shape
    row = lambda i: (i, 0)
    return pl.pallas_call(
        functools.partial(_convin_kernel, rc=min(ROW_CHUNK, tm), nc=min(512, d)),
        grid=(m // tm,),
        in_specs=[
            pl.BlockSpec((tm, d), row),
            pl.BlockSpec((None, 8, d), lambda i: (i, 0, 0)),
            _const_spec((1, d), lambda i: (0, 0)),
            _const_spec((d, 3 * d), lambda i: (0, 0)),
        ],
        out_specs=[pl.BlockSpec((tm, d), row), pl.BlockSpec((tm, d), row)],
        out_shape=[jax.ShapeDtypeStruct((m, d), BF16), jax.ShapeDtypeStruct((m, d), BF16)],
        scratch_shapes=[pltpu.VMEM((tm, d), BF16)],
        compiler_params=_cparams(1),
        name="conv_in_proj",
    )(x, modt, g, w_in)


def _convout_kernel(gb_ref, z_ref, zp_ref, zn_ref, x_ref, mod_ref, cw_ref, cb_ref, w_ref, o_ref, t_s,
                    *, n_p_tiles, lp, ls, rc, cc, nc):
    i = pl.program_id(0)
    tm, d = z_ref.shape
    halo = zp_ref.shape[0]
    seq_mask = jnp.where(i < n_p_tiles, lp - 1, ls - 1)
    ridx = lax.broadcasted_iota(jnp.int32, (rc, 1), 0)
    chunk_aligned = lp % rc == 0 and ls % rc == 0
    for r0 in range(0, tm, rc):
        rows = slice(r0, r0 + rc)
        if chunk_aligned:
            starts = ((i * tm + r0) & seq_mask) == 0
            ends = ((i * tm + r0 + rc) & seq_mask) == 0
        else:
            pos = (i * tm + r0 + ridx) & seq_mask
            is_start = pos == 0
            is_end = pos == seq_mask
        for c0 in range(0, d, cc):
            cs = slice(c0, c0 + cc)
            z = z_ref[rows, cs].astype(F32)
            before = (zp_ref[halo - 1:halo, cs] if r0 == 0 else z_ref[r0 - 1:r0, cs]).astype(F32)
            after = (zn_ref[0:1, cs] if r0 + rc == tm else z_ref[r0 + rc:r0 + rc + 1, cs]).astype(F32)
            if chunk_aligned:
                before = jnp.where(starts, 0.0, before)
                after = jnp.where(ends, 0.0, after)
            z_prev = jnp.where(ridx == 0, before, pltpu.roll(z, 1, 0))
            z_next = jnp.where(ridx == rc - 1, after, pltpu.roll(z, rc - 1, 0))
            if not chunk_aligned:
                z_prev = jnp.where(is_start, 0.0, z_prev)
                z_next = jnp.where(is_end, 0.0, z_next)
            conv = z_prev * cw_ref[0:1, cs] + z * cw_ref[1:2, cs] + z_next * cw_ref[2:3, cs] + cb_ref[:, cs]
            t_s[rows, cs] = (gb_ref[rows, cs].astype(F32) * conv).astype(t_s.dtype)
        t = t_s[rows, :]
        for n0 in range(0, d, nc):
            o_ref[rows, n0:n0 + nc] = (x_ref[rows, n0:n0 + nc]
                                       + mod_ref[2:3, n0:n0 + nc] * _dot(t, w_ref[:, n0:n0 + nc]))


def _convout_call(gb, z, x, modt, conv_w, conv_b, w_out, *, tm, n_p_tiles, lp, ls):
    m, d = x.shape
    halo = 16
    assert tm % halo == 0 and lp & (lp - 1) == 0 and ls & (ls - 1) == 0
    assert (n_p_tiles * tm) % ls == 0 or True
    hb = tm // halo
    last = m // halo - 1
    row = lambda i: (i, 0)
    return pl.pallas_call(
        functools.partial(_convout_kernel, n_p_tiles=n_p_tiles, lp=lp, ls=ls, rc=min(ROW_CHUNK, tm),
                          cc=min(256, d), nc=min(512, d)),
        grid=(m // tm,),
        in_specs=[
            pl.BlockSpec((tm, d), row),
            pl.BlockSpec((tm, d), row),
            pl.BlockSpec((halo, d), lambda i: (jnp.maximum(i * hb - 1, 0), 0)),
            pl.BlockSpec((halo, d), lambda i: (jnp.minimum((i + 1) * hb, last), 0)),
            pl.BlockSpec((tm, d), row),
            pl.BlockSpec((None, 8, d), lambda i: (i, 0, 0)),
            _const_spec((8, d), lambda i: (0, 0)),
            _const_spec((1, d), lambda i: (0, 0)),
            _const_spec((d, d), lambda i: (0, 0)),
        ],
        out_specs=pl.BlockSpec((tm, d), row),
        out_shape=jax.ShapeDtypeStruct((m, d), F32),
        scratch_shapes=[pltpu.VMEM((tm, d), BF16)],
        compiler_params=_cparams(1),
        name="conv_out_proj",
    )(gb, z, z, z, x, modt, conv_w, conv_b, w_out)


def _tile_rows(m_p, m_s, l_s, n_b, tm):
    assert m_p % tm == 0 and l_s % tm == 0
    return np.concatenate([np.full(m_p // tm, n_b), np.repeat(np.arange(n_b), l_s // tm)])


def _pick_tile(pref, m_p, l_s):
    tm = pref
    while m_p % tm or l_s % tm:
        tm //= 2
    return tm


def kernel(x_prompt, x_sample, c, cache_k, cache_v, state_ssm_re, state_ssm_im, c_ctx, ada_w, ada_b, norm1_g, norm2_g, ab_w_in, ab_w_out, s5_lam_re, s5_lam_im, s5_log_dt, s5_b_re, s5_b_im, s5_c_re, s5_c_im, s5_d, s5_glu_w, s5_glu_b, q_norm_g, k_norm_g, na_rpb, conv_w_in, conv_w, conv_b, conv_w_out, mlp_w1, mlp_w2):
    n_bp, l_p, d = x_prompt.shape
    n_bs, l_s, _ = x_sample.shape
    depth = ada_w.shape[0]
    m_p, m_s = n_bp * l_p, n_bs * l_s
    m = m_p + m_s
    n_g, n_state = s5_lam_re.shape[2], s5_lam_re.shape[3]
    s5w = n_g * S5_GROUP_CH
    naw = (ab_w_in.shape[2] - s5w) // 3
    n_h = naw // HEAD_DIM
    past = cache_k.shape[2]
    d_ff = mlp_w1.shape[2]
    assert m_p % l_s == 0, "latent sequences must start on a sequence-length row block"
    assert l_p % S5_CHUNK == 0 and l_s % S5_CHUNK == 0
    c_p, c_s = l_p // S5_CHUNK, l_s // S5_CHUNK
    assert c_p & (c_p - 1) == 0 and c_s & (c_s - 1) == 0
    r_p, r_s = m_p // S5_CHUNK, m_s // S5_CHUNK
    width = S5_CHUNK * S5_GROUP_CH

    xs = [x_prompt.reshape(m_p, d), x_sample.reshape(m_s, d)]

    n_rows = -(-(n_bs + 1) // 16) * 16
    cvec = jnp.concatenate([c, c_ctx[None], jnp.zeros((n_rows - n_bs - 1, d), F32)], axis=0)
    mod = _ada_call(cvec, ada_w, ada_b)

    def mod_tiles(layer, tm):
        t = mod[layer][_tile_rows(m_p, m_s, l_s, n_bs, tm)].reshape(m // tm, 6, d)
        return jnp.pad(t, ((0, 0), (0, 2), (0, 0)))

    tm_big = _pick_tile(1024, m_p, l_s)
    tm_mid = _pick_tile(512, m_p, l_s)
    tf = min(1024, d_ff)
    perm_in, perm_out = _chunk_perm(min(8, n_g))

    n_pw = max(int(math.log2(c_s)), 1)
    w_u, w_y, pw = _s5prep_call(s5_lam_re, s5_lam_im, s5_log_dt, s5_b_re, s5_b_im, s5_c_re, s5_c_im,
                                s5_d, n_pw)
    n_h0 = -(-n_bs // 8) * 8
    h0_all = jnp.stack([state_ssm_re, state_ssm_im], axis=-2)
    h0_all = h0_all.transpose(1, 3, 0, 2, 4, 5).reshape(-1, n_g, n_bs, 4 * n_state)
    h0_all = jnp.pad(h0_all, ((0, 0), (0, 0), (0, n_h0 - n_bs), (0, 0)))
    cache_k4 = cache_k.reshape(n_bs, -1, past, naw)
    cache_v4 = cache_v.reshape(n_bs, -1, past, naw)

    new_k, new_v, new_re, new_im = [], [], [], []
    for layer in range(depth):
        g1 = norm1_g[layer].reshape(1, d)
        g2 = norm2_g[layer].reshape(1, d)
        if layer % 2 == 0:
            e = layer // 2
            u, q, k, v = _inproj_call(xs, mod_tiles(layer, tm_mid), g1, ab_w_in[e].astype(BF16),
                                      q_norm_g[e].reshape(1, HEAD_DIM), k_norm_g[e].reshape(1, HEAD_DIM),
                                      tm=tm_mid, s5w=s5w, naw=naw, n_p_tiles=m_p // tm_mid)
            new_k.append(k[:m_p].reshape(n_bp, l_p, n_h, HEAD_DIM))
            new_v.append(v[:m_p].reshape(n_bp, l_p, n_h, HEAD_DIM))
            g_tok, z_all = _s5_call(u, perm_in, perm_out, w_u[e], w_y[e], pw[e], h0_all[e],
                                    l_tile=l_s, n_p_tiles=m_p // l_s, cp=c_p, cs=c_s)
            z_p = z_all[:, :r_p].reshape(n_g, n_bp, c_p, 2, 2, n_state)
            fin = jnp.stack([z_p[:, :, c_p - 1, 0], z_p[:, :, 0, 1]], axis=2)
            new_re.append(fin[:, :, :, 0].transpose(1, 2, 0, 3))
            new_im.append(fin[:, :, :, 1].transpose(1, 2, 0, 3))
            attn_p = _ctx_attn_call(q, k, v, n_seq=n_bp, seq_len=l_p)
            bias = _na_bias_call(na_rpb[e])
            attn_s = _na_call(q, k, v, cache_k4, cache_v4, bias, layer_e=e, n_b=n_bs, seq_len=l_s,
                              row_blk0=m_p // l_s)
            x = _about_call(g_tok, attn_p, attn_s, xs, mod_tiles(layer, tm_mid), s5_glu_w[e].astype(BF16),
                            s5_glu_b[e].reshape(1, s5w), ab_w_out[e].astype(BF16), tm=tm_mid,
                            n_p_tiles=m_p // tm_mid)
        else:
            x = xs[0]
            o = layer // 2
            gb, z = _convin_call(x, mod_tiles(layer, tm_mid), g1, conv_w_in[o].astype(BF16), tm=tm_mid)
            cw8 = jnp.pad(conv_w[o], ((0, 5), (0, 0)))
            x = _convout_call(gb, z, x, mod_tiles(layer, tm_mid), cw8, conv_b[o].reshape(1, d),
                              conv_w_out[o].astype(BF16), tm=tm_mid, n_p_tiles=m_p // tm_mid,
                              lp=l_p, ls=l_s)
        mlp = functools.partial(_mlp_call, x, mod_tiles(layer, tm_big), g2, mlp_w1[layer].astype(BF16),
                                mlp_w2[layer].astype(BF16), tm=tm_big, tf=tf)
        if layer < depth - 1:
            xs = [mlp()]
        else:
            y_prompt = mlp(tile0=0, n_tiles=m_p // tm_big).reshape(n_bp, l_p, d)
            y_sample = mlp(tile0=m_p // tm_big, n_tiles=m_s // tm_big).reshape(n_bs, l_s, d)

    return (y_prompt, y_sample, jnp.stack(new_k, axis=1), jnp.stack(new_v, axis=1),
            jnp.stack(new_re, axis=1), jnp.stack(new_im, axis=1))
```

```python
import functools
import math

import numpy as np
import jax
import jax.numpy as jnp
from jax import lax
from jax.experimental import pallas as pl
from jax.experimental.pallas import tpu as pltpu

F32 = jnp.float32
BF16 = jnp.bfloat16

NORM_EPS = 1e-6
NEG_INF = -1e30

S5_GROUP_CH = 16
HEAD_DIM = 128
GRID_W = 64
NA_WIN_R = 8
NA_WIN_C = 16
S5_CHUNK = 16
NA_QROWS = 4
NA_KROWS = NA_QROWS + NA_WIN_R
ROW_CHUNK = 256

VMEM_LIMIT_BYTES = 62 * 1024 * 1024


def _cparams(n_axes):
    return pltpu.CompilerParams(dimension_semantics=("arbitrary",) * n_axes,
                                vmem_limit_bytes=VMEM_LIMIT_BYTES)


def _const_spec(shape, index_map):
    return pl.BlockSpec(shape, index_map, pipeline_mode=pl.Buffered(1))


def _dot(a, b):
    return jnp.dot(a, b, preferred_element_type=F32)


def _dot_nt(a, b):
    return lax.dot_general(a, b, (((1,), (1,)), ((), ())), preferred_element_type=F32)


def _dot_split(a, b):
    a_hi = a.astype(BF16)
    a_lo = (a - a_hi.astype(F32)).astype(BF16)
    b_hi = b.astype(BF16)
    b_lo = (b - b_hi.astype(F32)).astype(BF16)
    return _dot(a_hi, b_hi) + _dot(a_hi, b_lo) + _dot(a_lo, b_hi)


def _modulate(x, g, shift, scale):
    ms = jnp.mean(x * x, axis=-1, keepdims=True)
    y = x * lax.rsqrt(ms + NORM_EPS) * g
    return y * (1.0 + scale) + shift


def _ada_kernel(c_ref, w_ref, b_ref, o_ref, *, nc):
    cv = c_ref[...]
    sc = (cv * jax.nn.sigmoid(cv)).astype(BF16)
    tn = w_ref.shape[1]
    for n0 in range(0, tn, nc):
        w = w_ref[:, n0:n0 + nc].astype(BF16)
        o_ref[:, n0:n0 + nc] = _dot(sc, w) + b_ref[:, n0:n0 + nc]


def _ada_call(cvec, ada_w, ada_b):
    depth, d, n6 = ada_w.shape
    rows = cvec.shape[0]
    tn = 1536 if n6 % 1536 == 0 else n6
    nc = 512 if tn % 512 == 0 else tn
    return pl.pallas_call(
        functools.partial(_ada_kernel, nc=nc),
        grid=(depth, n6 // tn),
        in_specs=[
            pl.BlockSpec((rows, d), lambda l, j: (0, 0)),
            pl.BlockSpec((None, d, tn), lambda l, j: (l, 0, j)),
            pl.BlockSpec((None, 1, tn), lambda l, j: (l, 0, j)),
        ],
        out_specs=pl.BlockSpec((None, rows, tn), lambda l, j: (l, 0, j)),
        out_shape=jax.ShapeDtypeStruct((depth, rows, n6), F32),
        compiler_params=_cparams(2),
        name="ada_params",
    )(cvec, ada_w, ada_b.reshape(depth, 1, n6))


def _x_specs(n_x, tm, d, n_p_tiles):
    if n_x == 1:
        return [pl.BlockSpec((tm, d), lambda i: (i, 0))]
    return [pl.BlockSpec((tm, d), lambda i: (jnp.minimum(i, n_p_tiles - 1), 0)),
            pl.BlockSpec((tm, d), lambda i: (jnp.maximum(i - n_p_tiles, 0), 0))]


def _for_part(i, n_p_tiles, refs, fn):
    if len(refs) == 1:
        fn(refs[0])
    else:
        pl.when(i < n_p_tiles)(lambda: fn(refs[0]))
        pl.when(i >= n_p_tiles)(lambda: fn(refs[1]))


def _inproj_kernel(*refs, n_x, n_p_tiles, s5w, naw, rc):
    x_refs = refs[:n_x]
    mod_ref, g_ref, w_ref, qg_ref, kg_ref, u_ref, q_ref, k_ref, v_ref, h_s = refs[n_x:]
    tm = u_ref.shape[0]
    hd = HEAD_DIM
    cw = min(4 * hd, naw)
    g, shift, scale = g_ref[...], mod_ref[0:1, :], mod_ref[1:2, :]
    qg = qg_ref[...]
    kg = kg_ref[...]

    def head_norm(t, gain):
        return t * lax.rsqrt(jnp.mean(t * t, axis=-1, keepdims=True) + NORM_EPS) * gain

    def run(x_ref):
        for r0 in range(0, tm, rc):
            rows = slice(r0, r0 + rc)
            h_s[rows, :] = _modulate(x_ref[rows, :], g, shift, scale).astype(h_s.dtype)
            h = h_s[rows, :]
            u_ref[rows, :] = _dot(h, w_ref[:, 0:s5w]).astype(u_ref.dtype)
            for c0 in range(0, naw, cw):
                qc = _dot(h, w_ref[:, s5w + c0:s5w + c0 + cw])
                for j in range(cw // hd):
                    q_ref[rows, c0 + j * hd:c0 + (j + 1) * hd] = head_norm(
                        qc[:, j * hd:(j + 1) * hd], qg).astype(q_ref.dtype)
                kc = _dot(h, w_ref[:, s5w + naw + c0:s5w + naw + c0 + cw])
                for j in range(cw // hd):
                    k_ref[rows, c0 + j * hd:c0 + (j + 1) * hd] = head_norm(
                        kc[:, j * hd:(j + 1) * hd], kg).astype(k_ref.dtype)
                v_ref[rows, c0:c0 + cw] = _dot(
                    h, w_ref[:, s5w + 2 * naw + c0:s5w + 2 * naw + c0 + cw]).astype(v_ref.dtype)

    _for_part(pl.program_id(0), n_p_tiles, x_refs, run)


def _inproj_call(xs, modt, g, w_in, qg, kg, *, layer, tm, s5w, naw, n_p_tiles):
    m = sum(x.shape[0] for x in xs)
    d = xs[0].shape[1]
    n_in = w_in.shape[2]
    rc = min(ROW_CHUNK, tm)
    row = lambda i: (i, 0)
    return pl.pallas_call(
        functools.partial(_inproj_kernel, n_x=len(xs), n_p_tiles=n_p_tiles, s5w=s5w, naw=naw, rc=rc),
        grid=(m // tm,),
        in_specs=_x_specs(len(xs), tm, d, n_p_tiles) + [
            pl.BlockSpec((None, 8, d), lambda i: (i, 0, 0)),
            _const_spec((1, d), lambda i: (0, 0)),
            _const_spec((None, d, n_in), lambda i: (layer, 0, 0)),
            _const_spec((1, HEAD_DIM), lambda i: (0, 0)),
            _const_spec((1, HEAD_DIM), lambda i: (0, 0)),
        ],
        out_specs=[
            pl.BlockSpec((tm, s5w), row),
            pl.BlockSpec((tm, naw), row),
            pl.BlockSpec((tm, naw), row),
            pl.BlockSpec((tm, naw), row),
        ],
        out_shape=[
            jax.ShapeDtypeStruct((m, s5w), F32),
            jax.ShapeDtypeStruct((m, naw), BF16),
            jax.ShapeDtypeStruct((m, naw), F32),
            jax.ShapeDtypeStruct((m, naw), F32),
        ],
        scratch_shapes=[pltpu.VMEM((tm, d), BF16)],
        compiler_params=_cparams(1),
        name="ab_in_proj",
    )(*xs, modt, g, w_in, qg, kg)


def _s5prep_kernel(row_ref, col_ref, btr_ref, bti_ref, cer_ref, cei_ref, d_ref,
                   wu_ref, wy_ref, pw_ref, *, n_state, n_pw):
    t_chunk = S5_CHUNK
    width = t_chunk * S5_GROUP_CH
    p = n_state
    lg_ch = int(math.log2(S5_GROUP_CH))
    r_i = lax.broadcasted_iota(jnp.int32, (width, 1), 0)
    c_i = lax.broadcasted_iota(jnp.int32, (1, width), 1)
    s_i = r_i >> lg_ch
    t_i = c_i >> lg_ch
    n_pow = -(-(t_chunk + 1) // 8) * 8

    def cmul(ar, ai, br, bi):
        return ar * br - ai * bi, ar * bi + ai * br

    acc = jnp.zeros((width, width), F32)
    for d in range(2):
        lr = row_ref[d, 0:1, 0:p]
        li = row_ref[d, 1:2, 0:p]
        dt = jnp.exp(row_ref[d, 2:3, 0:p])
        ar = lr * dt
        ai = li * dt
        er = jnp.exp(ar)
        abr = er * jnp.cos(ai)
        abi = er * jnp.sin(ai)
        den = lr * lr + li * li
        nr = abr - 1.0
        f_re = (nr * lr + abi * li) / den
        f_im = (abi * lr - nr * li) / den
        bb_re, bb_im = cmul(f_re, f_im, btr_ref[d], bti_ref[d])

        lrc = col_ref[d, :, 0:1]
        lic = col_ref[d, :, 1:2]
        dtc = jnp.exp(col_ref[d, :, 2:3])
        arc = lrc * dtc
        aic = lic * dtc
        ce_re = cer_ref[d]
        ce_im = cei_ref[d]

        k_r = lax.broadcasted_iota(jnp.int32, (n_pow, 1), 0).astype(F32)
        k_c = lax.broadcasted_iota(jnp.int32, (1, 128), 1).astype(F32)
        mag_r = jnp.exp(k_r * ar)
        tr_re, tr_im = mag_r * jnp.cos(k_r * ai), mag_r * jnp.sin(k_r * ai)
        mag_c = jnp.exp(arc * k_c)
        tc_re, tc_im = mag_c * jnp.cos(aic * k_c), mag_c * jnp.sin(aic * k_c)

        def pow_row(e, e_max):
            re = jnp.zeros((width, p), F32)
            im = jnp.zeros((width, p), F32)
            for kk in range(e_max + 1):
                hit = e == kk
                re = jnp.where(hit, tr_re[kk:kk + 1, :], re)
                im = jnp.where(hit, tr_im[kk:kk + 1, :], im)
            return re, im

        def pow_col(e, e_max):
            re = jnp.zeros((p, width), F32)
            im = jnp.zeros((p, width), F32)
            for kk in range(e_max + 1):
                hit = e == kk
                re = jnp.where(hit, tc_re[:, kk:kk + 1], re)
                im = jnp.where(hit, tc_im[:, kk:kk + 1], im)
            return re, im

        terms = [(jnp.zeros_like(s_i), jnp.zeros_like(t_i), s_i == t_i, 0)]
        blk = 2
        while blk <= t_chunk:
            lb = int(math.log2(blk))
            half = blk // 2
            mid_s = ((s_i >> lb) << lb) + half
            mid_t = ((t_i >> lb) << lb) + half
            same = (s_i >> lb) == (t_i >> lb)
            if d == 0:
                e_s, ok_s = mid_s - s_i, s_i < mid_s
                e_t, ok_t = t_i - mid_t, t_i >= mid_t
            else:
                e_s, ok_s = s_i - mid_s, s_i >= mid_s
                e_t, ok_t = mid_t - t_i, t_i < mid_t
            terms.append((jnp.maximum(e_s, 0), jnp.maximum(e_t, 0), same & ok_s & ok_t, half))
            blk *= 2
        for e_s, e_t, mask, e_max in terms:
            l_re, l_im = cmul(*pow_row(e_s, e_max), bb_re, bb_im)
            r_re, r_im = cmul(*pow_col(e_t, e_max), ce_re, ce_im)
            term = _dot_split(l_re, r_re) - _dot_split(l_im, r_im)
            acc = acc + jnp.where(mask, term, 0.0)

        if d == 0:
            ws_re, ws_im = cmul(*pow_row(t_chunk - 1 - s_i, t_chunk), bb_re, bb_im)
            ca_re, ca_im = cmul(*pow_col(t_i + 1, t_chunk), ce_re, ce_im)
        else:
            ws_re, ws_im = cmul(*pow_row(s_i, t_chunk), bb_re, bb_im)
            ca_re, ca_im = cmul(*pow_col(t_chunk - t_i, t_chunk), ce_re, ce_im)
        base = width + d * 2 * p
        wu_ref[:, base:base + p] = ws_re.astype(wu_ref.dtype)
        wu_ref[:, base + p:base + 2 * p] = ws_im.astype(wu_ref.dtype)
        wy_ref[d * 2 * p:d * 2 * p + p, :] = ca_re.astype(wy_ref.dtype)
        wy_ref[d * 2 * p + p:(d + 1) * 2 * p, :] = (-ca_im).astype(wy_ref.dtype)

        lr2 = row_ref[d, 0:1, :]
        li2 = row_ref[d, 1:2, :]
        dt2 = jnp.exp(row_ref[d, 2:3, :])
        k16 = float(t_chunk)
        mag = jnp.exp(k16 * (lr2 * dt2))
        pr = mag * jnp.cos(k16 * (li2 * dt2))
        pi = mag * jnp.sin(k16 * (li2 * dt2))
        lane = lax.broadcasted_iota(jnp.int32, pr.shape, 1)
        sign = jnp.where(lane < p, -1.0, 1.0)
        for i in range(n_pw):
            pw_ref[i * 4 + d * 2:i * 4 + d * 2 + 1, :] = pr
            pw_ref[i * 4 + d * 2 + 1:i * 4 + d * 2 + 2, :] = pi * sign
            pr, pi = pr * pr - pi * pi, 2.0 * pr * pi

    wu_ref[:, 0:width] = (acc + jnp.where(r_i == c_i, d_ref[...], 0.0)).astype(wu_ref.dtype)


def _s5prep_call(lam_re, lam_im, log_dt, b_re, b_im, c_re, c_im, d_skip, n_pw):
    n_e, _, n_g, p = lam_re.shape
    n_ch = S5_GROUP_CH
    width = S5_CHUNK * n_ch
    assert 2 * p == 128, "state rows are packed as [re | im] in one 128-lane tile"
    dtb = jnp.broadcast_to(log_dt[..., None], lam_re.shape)
    zeros = jnp.zeros_like(lam_re)
    rowp = jnp.stack([lam_re, lam_im, dtb] + [zeros] * 5, axis=-2)
    rowp = jnp.concatenate([rowp, rowp], axis=-1)
    colp = jnp.stack([lam_re, lam_im, dtb] + [zeros] * 5, axis=-1)
    bt = lambda b: jnp.tile(jnp.swapaxes(b, -1, -2), (1, 1, 1, S5_CHUNK, 1))
    ce = lambda c: jnp.tile(jnp.swapaxes(c, -1, -2), (1, 1, 1, 1, S5_CHUNK))
    d_row = jnp.tile(d_skip.reshape(n_e, n_g, 1, n_ch), (1, 1, 1, S5_CHUNK))

    def dspec(shape):
        return pl.BlockSpec((None, 2, None) + shape, lambda e, g: (e, 0, g, 0, 0))

    def ospec(shape):
        return pl.BlockSpec((None, None) + shape, lambda e, g: (e, g, 0, 0))

    return pl.pallas_call(
        functools.partial(_s5prep_kernel, n_state=p, n_pw=n_pw),
        grid=(n_e, n_g),
        in_specs=[dspec((8, 2 * p)), dspec((p, 8)), dspec((width, p)), dspec((width, p)),
                  dspec((p, width)), dspec((p, width)), ospec((1, width))],
        out_specs=[ospec((width, 2 * width)), ospec((width, width)), ospec((4 * n_pw, 2 * p))],
        out_shape=[
            jax.ShapeDtypeStruct((n_e, n_g, width, 2 * width), BF16),
            jax.ShapeDtypeStruct((n_e, n_g, width, width), BF16),
            jax.ShapeDtypeStruct((n_e, n_g, 4 * n_pw, 2 * p), F32),
        ],
        compiler_params=_cparams(2),
        name="s5_prep",
    )(rowp, colp, bt(b_re), bt(b_im), ce(c_re), ce(c_im), d_row)


def _gelu_tanh(y):
    return 0.5 * y * (1.0 + jnp.tanh(0.7978845608028654 * (y + 0.044715 * (y * y * y))))


def _block_transpose8(v, lane):
    for dist in (4, 2, 1):
        width = dist * S5_GROUP_CH
        low = (lane & width) == 0
        out = list(v)
        for a in range(8):
            if a & dist == 0:
                lo, hi = v[a], v[a + dist]
                out[a] = jnp.where(low, lo, pltpu.roll(hi, width, 1))
                out[a + dist] = jnp.where(low, pltpu.roll(lo, 128 - width, 1), hi)
        v = out
    return v


def _s5_kernel(u_ref, wu_ref, wy_ref, pw_ref, h0_ref, g_ref, z_ref, ph_s, ug_s, yg_s,
               *, n_p_tiles, cp, cs):
    tile = pl.program_id(1)
    t_chunk = S5_CHUNK
    n = u_ref.shape[0] // t_chunk
    ngb, width = wy_ref.shape[0], wy_ref.shape[1]
    half = width // 2
    lanes = u_ref.shape[1]

    def crot(a1, a2, x):
        return a1 * x + a2 * pltpu.roll(x, half // 2, 1)

    lane = lax.broadcasted_iota(jnp.int32, (1, lanes), 1)
    for s_hi in range(t_chunk // 8):
        w = _block_transpose8([u_ref[pl.ds(s_hi * 8 + a, n, stride=t_chunk), :] for a in range(8)], lane)
        for gi in range(ngb):
            ug_s[gi, :, s_hi * lanes:(s_hi + 1) * lanes] = w[gi].astype(ug_s.dtype)

    def process(gi, cseq, h0, ph):
        ys = _dot(ug_s[gi], wu_ref[gi])
        wy = wy_ref[gi]
        pos = lax.broadcasted_iota(jnp.int32, (n, 1), 0) & (cseq - 1)
        zs, xs = [], []
        for d in range(2):
            z = ys[:, width + d * half:width + (d + 1) * half]
            edge = (pos == 0) if d == 0 else (pos == cseq - 1)
            if h0 is not None:
                z = z + jnp.where(edge, ph[:, d * half:(d + 1) * half], 0.0)
            k = 1
            i = 0
            while k < cseq:
                a1 = pw_ref[gi, pl.ds(i * 4 + d * 2, 1), :]
                a2 = pw_ref[gi, pl.ds(i * 4 + d * 2 + 1, 1), :]
                if d == 0:
                    sh = pltpu.roll(z, k, 0)
                    valid = pos >= k
                else:
                    sh = pltpu.roll(z, n - k, 0)
                    valid = pos < cseq - k
                z = z + jnp.where(valid, crot(a1, a2, sh), 0.0)
                k *= 2
                i += 1
            x = pltpu.roll(z, 1, 0) if d == 0 else pltpu.roll(z, n - 1, 0)
            x_edge = 0.0 if h0 is None else h0[:, d * half:(d + 1) * half]
            xs.append(jnp.where(edge, x_edge, x))
            zs.append(z)
        x = jnp.concatenate(xs, axis=1).astype(BF16)
        y = ys[:, 0:width] + _dot(x, wy)
        yg_s[gi] = _gelu_tanh(y).astype(yg_s.dtype)
        if h0 is None:
            z_ref[gi] = jnp.concatenate(zs, axis=1)

    @pl.when(tile < n_p_tiles)
    def _():
        def body(gi, carry):
            process(gi, cp, None, None)
            return carry
        lax.fori_loop(0, ngb, body, 0, unroll=2)

    @pl.when(tile >= n_p_tiles)
    def _():
        j = tile - n_p_tiles

        def body(gi, carry):
            h0 = h0_ref[gi]
            for d in range(2):
                ph_s[gi, :, d * half:(d + 1) * half] = crot(
                    pw_ref[gi, pl.ds(d * 2, 1), :], pw_ref[gi, pl.ds(d * 2 + 1, 1), :],
                    h0[:, d * half:(d + 1) * half])
            process(gi, cs, h0_ref[gi, pl.ds(j, 1), :], ph_s[gi, pl.ds(j, 1), :])
            return carry
        lax.fori_loop(0, ngb, body, 0, unroll=2)

    for t_hi in range(t_chunk // 8):
        w = _block_transpose8([yg_s[gi, :, t_hi * lanes:(t_hi + 1) * lanes] for gi in range(ngb)], lane)
        for a in range(8):
            g_ref[pl.ds(t_hi * 8 + a, n, stride=t_chunk), :] = w[a]


def _s5_call(u, w_u, w_y, pw, h0, *, l_tile, n_p_tiles, cp, cs):
    m, s5w = u.shape
    n_g, width = w_y.shape[0], w_y.shape[1]
    ngb = min(8, n_g)
    lanes = ngb * S5_GROUP_CH
    assert lanes == 128 and ngb == 8 and S5_CHUNK % 8 == 0 and n_g % ngb == 0 and m % l_tile == 0
    n = l_tile // S5_CHUNK
    n_h0 = h0.shape[1]
    tok = pl.BlockSpec((l_tile, lanes), lambda cb, t: (t, cb))
    grp = lambda shape: pl.BlockSpec((ngb,) + shape, lambda cb, t: (cb, 0, 0))
    return pl.pallas_call(
        functools.partial(_s5_kernel, n_p_tiles=n_p_tiles, cp=cp, cs=cs),
        grid=(n_g // ngb, m // l_tile),
        in_specs=[tok, grp((width, 2 * width)), grp((width, width)), grp(pw.shape[1:]), grp((n_h0, width))],
        out_specs=[tok, pl.BlockSpec((ngb, n, width), lambda cb, t: (cb, jnp.minimum(t, n_p_tiles - 1), 0))],
        out_shape=[jax.ShapeDtypeStruct((m, s5w), F32),
                   jax.ShapeDtypeStruct((n_g, n_p_tiles * n, width), F32)],
        scratch_shapes=[pltpu.VMEM((ngb, n_h0, width), F32),
                        pltpu.VMEM((ngb, n, width), BF16),
                        pltpu.VMEM((ngb, n, width), F32)],
        compiler_params=_cparams(2),
        name="s5_core",
    )(u, w_u, w_y, pw, h0)


def _ctx_attn_kernel(q_ref, k_ref, v_ref, o_ref, *, scale):
    hd = HEAD_DIM
    for h in range(q_ref.shape[1] // hd):
        sl = slice(h * hd, (h + 1) * hd)
        s = _dot_nt(q_ref[:, sl], k_ref[:, sl].astype(BF16)) * scale
        m = jnp.max(s, axis=-1, keepdims=True)
        p = jnp.exp(s - m)
        l = jnp.sum(p, axis=-1, keepdims=True)
        o = _dot(p.astype(BF16), v_ref[:, sl].astype(BF16)) / l
        o_ref[:, sl] = o.astype(o_ref.dtype)


def _ctx_attn_call(q, k, v, *, n_seq, seq_len):
    naw = q.shape[1]
    spec = pl.BlockSpec((seq_len, naw), lambda b: (b, 0))
    return pl.pallas_call(
        functools.partial(_ctx_attn_kernel, scale=HEAD_DIM ** -0.5),
        grid=(n_seq,),
        in_specs=[spec, spec, spec],
        out_specs=spec,
        out_shape=jax.ShapeDtypeStruct((n_seq * seq_len, naw), BF16),
        compiler_params=_cparams(1),
        name="ctx_attn",
    )(q, k, v)


def _na_bias_kernel(rpb_ref, o_ref, t_s):
    h = pl.program_id(0)
    n_dr = 2 * NA_WIN_R - 1
    n_dc = 2 * NA_WIN_C - 1
    w = GRID_W
    qc = lax.broadcasted_iota(jnp.int32, (w, w), 0)
    kc = lax.broadcasted_iota(jnp.int32, (w, w), 1)
    dc = kc - qc + (NA_WIN_C - 1)
    c0 = jnp.clip(qc - NA_WIN_C // 2, 0, w - NA_WIN_C)
    col_ok = (kc >= c0) & (kc < c0 + NA_WIN_C)
    for dr in range(n_dr):
        def pick(j, t, dr=dr):
            return jnp.where(dc == j, rpb_ref[h * (n_dr * n_dc) + dr * n_dc + j], t)
        t = lax.fori_loop(0, n_dc, pick, jnp.zeros((w, w), F32))
        t_s[dr] = jnp.where(col_ok, t, NEG_INF)
    neg = jnp.full((w, w), NEG_INF, F32)
    patterns = [(0, lambda i: 0), (NA_WIN_R // 2, lambda i: i), (NA_WIN_R, lambda i: NA_QROWS)]
    for pat, (r_rel, r0_rel) in enumerate(patterns):
        for i in range(NA_QROWS):
            for kr in range(NA_KROWS):
                dr = kr - i - r_rel + (NA_WIN_R - 1)
                valid = r0_rel(i) <= kr < r0_rel(i) + NA_WIN_R
                o_ref[pat, i * w:(i + 1) * w, kr * w:(kr + 1) * w] = t_s[dr] if valid else neg


def _na_bias_call(rpb):
    n_h = rpb.shape[0]
    nq = NA_QROWS * GRID_W
    nk = NA_KROWS * GRID_W
    return pl.pallas_call(
        _na_bias_kernel,
        grid=(n_h,),
        in_specs=[pl.BlockSpec(memory_space=pltpu.SMEM)],
        out_specs=pl.BlockSpec((None, 3, nq, nk), lambda h: (h, 0, 0, 0)),
        out_shape=jax.ShapeDtypeStruct((n_h, 3, nq, nk), F32),
        scratch_shapes=[pltpu.VMEM((2 * NA_WIN_R - 1, GRID_W, GRID_W), F32)],
        compiler_params=_cparams(1),
        name="na_bias",
    )(rpb.reshape(-1))


def _na_kernel(q_ref, k_ref, v_ref, kc_ref, vc_ref, bb_ref, o_ref, kb_s, vb_s, kcb_s, vcb_s,
               *, rows, scale):
    w = GRID_W
    nq = NA_QROWS * w
    nk = NA_KROWS * w
    n_blk = rows // NA_QROWS
    kb_s[...] = k_ref[...].astype(BF16)
    vb_s[...] = v_ref[...].astype(BF16)
    kcb_s[...] = kc_ref[...].astype(BF16)
    vcb_s[...] = vc_ref[...].astype(BF16)

    def body(blk, carry):
        q_rows = pl.ds(pl.multiple_of(blk * nq, nq), nq)
        k_base = jnp.clip(blk * NA_QROWS - NA_WIN_R // 2, 0, rows - NA_KROWS)
        k_rows = pl.ds(pl.multiple_of(k_base * w, nq), nk)
        pat = jnp.where(blk == 0, 0, jnp.where(blk == n_blk - 1, 2, 1))
        q = q_ref[q_rows, :]
        s_loc = _dot_nt(q, kb_s[k_rows, :]) * scale + bb_ref[pat]
        s_ctx = _dot_nt(q, kcb_s[...]) * scale
        m = jnp.maximum(jnp.max(s_loc, axis=-1, keepdims=True), jnp.max(s_ctx, axis=-1, keepdims=True))
        p_loc = jnp.exp(s_loc - m)
        p_ctx = jnp.exp(s_ctx - m)
        l = jnp.sum(p_loc, axis=-1, keepdims=True) + jnp.sum(p_ctx, axis=-1, keepdims=True)
        o = _dot(p_loc.astype(BF16), vb_s[k_rows, :]) + _dot(p_ctx.astype(BF16), vcb_s[...])
        o_ref[q_rows, :] = (o / l).astype(o_ref.dtype)
        return carry

    lax.fori_loop(0, n_blk, body, 0, unroll=2)


def _na_call(q, k, v, cache_k, cache_v, bias, *, layer_e, n_b, seq_len, row_blk0):
    naw = q.shape[1]
    n_h = naw // HEAD_DIM
    past = cache_k.shape[2]
    rows = seq_len // GRID_W
    assert rows % NA_QROWS == 0 and rows >= NA_KROWS
    tok = pl.BlockSpec((seq_len, HEAD_DIM), lambda b, h: (row_blk0 + b, h))
    ctx = pl.BlockSpec((None, None, past, HEAD_DIM), lambda b, h: (b, layer_e, 0, h))
    return pl.pallas_call(
        functools.partial(_na_kernel, rows=rows, scale=HEAD_DIM ** -0.5),
        grid=(n_b, n_h),
        in_specs=[tok, tok, tok, ctx, ctx,
                  pl.BlockSpec((None,) + bias.shape[1:], lambda b, h: (h, 0, 0, 0))],
        out_specs=pl.BlockSpec((seq_len, HEAD_DIM), lambda b, h: (b, h)),
        out_shape=jax.ShapeDtypeStruct((n_b * seq_len, naw), BF16),
        scratch_shapes=[pltpu.VMEM((seq_len, HEAD_DIM), BF16), pltpu.VMEM((seq_len, HEAD_DIM), BF16),
                        pltpu.VMEM((past, HEAD_DIM), BF16), pltpu.VMEM((past, HEAD_DIM), BF16)],
        compiler_params=_cparams(2),
        name="na_attn",
    )(q, k, v, cache_k, cache_v, bias)


def _about_kernel(*refs, n_x, n_p_tiles, nc, rc):
    g_ref, ap_ref, as_ref = refs[:3]
    x_refs = refs[3:3 + n_x]
    mod_ref, gw_ref, gb_ref, w_ref, o_ref = refs[3 + n_x:]
    s5w = g_ref.shape[1]
    tm, d = o_ref.shape

    def run(part):
        a_ref, x_ref = part
        for r0 in range(0, tm, rc):
            rows = slice(r0, r0 + rc)
            g = g_ref[rows, :]
            gl = _dot(g.astype(BF16), gw_ref[...]) + gb_ref[...]
            s5o = (g * jax.nn.sigmoid(gl)).astype(BF16)
            a = a_ref[rows, :]
            for n0 in range(0, d, nc):
                y = _dot(s5o, w_ref[0:s5w, n0:n0 + nc]) + _dot(a, w_ref[s5w:, n0:n0 + nc])
                o_ref[rows, n0:n0 + nc] = x_ref[rows, n0:n0 + nc] + mod_ref[2:3, n0:n0 + nc] * y

    _for_part(pl.program_id(0), n_p_tiles, [(ap_ref, x_refs[0]), (as_ref, x_refs[-1])], run)


def _about_call(g, attn_p, attn_s, xs, modt, glu_w, glu_b, w_out, *, layer, tm, n_p_tiles):
    m = sum(x.shape[0] for x in xs)
    d = xs[0].shape[1]
    s5w = g.shape[1]
    naw = attn_p.shape[1]
    nc = min(512, d)
    row = lambda i: (i, 0)
    return pl.pallas_call(
        functools.partial(_about_kernel, n_x=len(xs), n_p_tiles=n_p_tiles, nc=nc, rc=min(ROW_CHUNK, tm)),
        grid=(m // tm,),
        in_specs=[pl.BlockSpec((tm, s5w), row)] + _x_specs(2, tm, naw, n_p_tiles)
        + _x_specs(len(xs), tm, d, n_p_tiles) + [
            pl.BlockSpec((None, 8, d), lambda i: (i, 0, 0)),
            _const_spec((None, s5w, s5w), lambda i: (layer, 0, 0)),
            _const_spec((1, s5w), lambda i: (0, 0)),
            _const_spec((None, s5w + naw, d), lambda i: (layer, 0, 0)),
        ],
        out_specs=pl.BlockSpec((tm, d), row),
        out_shape=jax.ShapeDtypeStruct((m, d), F32),
        compiler_params=_cparams(1),
        name="ab_out_proj",
    )(g, attn_p, attn_s, *xs, modt, glu_w, glu_b, w_out)


def _mlp_kernel(x_ref, mod_ref, g_ref, w1_ref, w2_ref, o_ref, h_s, a_s, *, rc, nc1, nc2):
    tm, d = o_ref.shape
    tf = w1_ref.shape[1]
    ta = a_s.shape[1]

    def chunk_dots(rows):
        h = h_s[rows, :]
        for f0 in range(0, tf, ta):
            for c0 in range(0, ta, nc1):
                a = jnp.maximum(_dot(h, w1_ref[:, f0 + c0:f0 + c0 + nc1]), 0.0)
                a_s[rows, c0:c0 + nc1] = (a * a).astype(a_s.dtype)
            a = a_s[rows, :]
            for n0 in range(0, d, nc2):
                o_ref[rows, n0:n0 + nc2] += mod_ref[5:6, n0:n0 + nc2] * _dot(a, w2_ref[f0:f0 + ta, n0:n0 + nc2])

    @pl.when(pl.program_id(1) == 0)
    def _():
        g, shift, scale = g_ref[...], mod_ref[3:4, :], mod_ref[4:5, :]
        for r0 in range(0, tm, rc):
            rows = slice(r0, r0 + rc)
            x = x_ref[rows, :]
            h_s[rows, :] = _modulate(x, g, shift, scale).astype(h_s.dtype)
            o_ref[rows, :] = x
            chunk_dots(rows)

    @pl.when(pl.program_id(1) != 0)
    def _():
        chunk_dots(slice(0, tm))


def _mlp_call(x, modt, g, w1, w2, *, layer, tm, tf, tile0=0, n_tiles=None):
    d = x.shape[1]
    n_tiles = x.shape[0] // tm if n_tiles is None else n_tiles
    m = n_tiles * tm
    d_ff = w1.shape[2]
    rc = min(ROW_CHUNK, tm)
    return pl.pallas_call(
        functools.partial(_mlp_kernel, rc=rc, nc1=min(256, tf), nc2=min(512, d)),
        grid=(n_tiles, d_ff // tf),
        in_specs=[
            pl.BlockSpec((tm, d), lambda i, f: (i + tile0, 0)),
            pl.BlockSpec((None, 8, d), lambda i, f: (i + tile0, 0, 0)),
            _const_spec((1, d), lambda i, f: (0, 0)),
            pl.BlockSpec((None, d, tf), lambda i, f: (layer, 0, f)),
            pl.BlockSpec((None, tf, d), lambda i, f: (layer, f, 0)),
        ],
        out_specs=pl.BlockSpec((tm, d), lambda i, f: (i, 0)),
        out_shape=jax.ShapeDtypeStruct((m, d), F32),
        scratch_shapes=[pltpu.VMEM((tm, d), BF16), pltpu.VMEM((tm, min(512, tf)), BF16)],
        compiler_params=_cparams(2),
        name="mlp",
    )(x, modt, g, w1, w2)


def _convin_kernel(x_ref, mod_ref, g_ref, w_ref, gb_ref, z_ref, h_s, *, rc, nc):
    tm, d = gb_ref.shape
    g, shift, scale = g_ref[...], mod_ref[0:1, :], mod_ref[1:2, :]
    for r0 in range(0, tm, rc):
        rows = slice(r0, r0 + rc)
        h_s[rows, :] = _modulate(x_ref[rows, :], g, shift, scale).astype(h_s.dtype)
        h = h_s[rows, :]
        for n0 in range(0, d, nc):
            gb_ref[rows, n0:n0 + nc] = _dot(h, w_ref[:, n0:n0 + nc]).astype(gb_ref.dtype)
            z_ref[rows, n0:n0 + nc] = (_dot(h, w_ref[:, d + n0:d + n0 + nc])
                                       * _dot(h, w_ref[:, 2 * d + n0:2 * d + n0 + nc])).astype(z_ref.dtype)


def _convin_call(x, modt, g, w_in, *, layer, tm):
    m, d = x.shape
    row = lambda i: (i, 0)
    return pl.pallas_call(
        functools.partial(_convin_kernel, rc=min(ROW_CHUNK, tm), nc=min(512, d)),
        grid=(m // tm,),
        in_specs=[
            pl.BlockSpec((tm, d), row),
            pl.BlockSpec((None, 8, d), lambda i: (i, 0, 0)),
            _const_spec((1, d), lambda i: (0, 0)),
            _const_spec((None, d, 3 * d), lambda i: (layer, 0, 0)),
        ],
        out_specs=[pl.BlockSpec((tm, d), row), pl.BlockSpec((tm, d), row)],
        out_shape=[jax.ShapeDtypeStruct((m, d), BF16), jax.ShapeDtypeStruct((m, d), BF16)],
        scratch_shapes=[pltpu.VMEM((tm, d), BF16)],
        compiler_params=_cparams(1),
        name="conv_in_proj",
    )(x, modt, g, w_in)


def _convout_kernel(gb_ref, z_ref, zp_ref, zn_ref, x_ref, mod_ref, cw_ref, cb_ref, w_ref, o_ref, t_s,
                    *, n_p_tiles, lp, ls, rc, cc, nc):
    i = pl.program_id(0)
    tm, d = z_ref.shape
    halo = zp_ref.shape[0]
    seq_mask = jnp.where(i < n_p_tiles, lp - 1, ls - 1)
    ridx = lax.broadcasted_iota(jnp.int32, (rc, 1), 0)
    chunk_aligned = lp % rc == 0 and ls % rc == 0
    for r0 in range(0, tm, rc):
        rows = slice(r0, r0 + rc)
        if chunk_aligned:
            starts = ((i * tm + r0) & seq_mask) == 0
            ends = ((i * tm + r0 + rc) & seq_mask) == 0
        else:
            pos = (i * tm + r0 + ridx) & seq_mask
            is_start = pos == 0
            is_end = pos == seq_mask
        for c0 in range(0, d, cc):
            cs = slice(c0, c0 + cc)
            z = z_ref[rows, cs].astype(F32)
            before = (zp_ref[halo - 1:halo, cs] if r0 == 0 else z_ref[r0 - 1:r0, cs]).astype(F32)
            after = (zn_ref[0:1, cs] if r0 + rc == tm else z_ref[r0 + rc:r0 + rc + 1, cs]).astype(F32)
            if chunk_aligned:
                before = jnp.where(starts, 0.0, before)
                after = jnp.where(ends, 0.0, after)
            z_prev = jnp.where(ridx == 0, before, pltpu.roll(z, 1, 0))
            z_next = jnp.where(ridx == rc - 1, after, pltpu.roll(z, rc - 1, 0))
            if not chunk_aligned:
                z_prev = jnp.where(is_start, 0.0, z_prev)
                z_next = jnp.where(is_end, 0.0, z_next)
            conv = z_prev * cw_ref[0:1, cs] + z * cw_ref[1:2, cs] + z_next * cw_ref[2:3, cs] + cb_ref[:, cs]
            t_s[rows, cs] = (gb_ref[rows, cs].astype(F32) * conv).astype(t_s.dtype)
        t = t_s[rows, :]
        for n0 in range(0, d, nc):
            o_ref[rows, n0:n0 + nc] = (x_ref[rows, n0:n0 + nc]
                                       + mod_ref[2:3, n0:n0 + nc] * _dot(t, w_ref[:, n0:n0 + nc]))


def _convout_call(gb, z, x, modt, conv_w, conv_b, w_out, *, layer, tm, n_p_tiles, lp, ls):
    m, d = x.shape
    halo = 16
    assert tm % halo == 0 and lp & (lp - 1) == 0 and ls & (ls - 1) == 0
    assert (n_p_tiles * tm) % ls == 0 or True
    hb = tm // halo
    last = m // halo - 1
    row = lambda i: (i, 0)
    return pl.pallas_call(
        functools.partial(_convout_kernel, n_p_tiles=n_p_tiles, lp=lp, ls=ls, rc=min(ROW_CHUNK, tm),
                          cc=min(256, d), nc=min(512, d)),
        grid=(m // tm,),
        in_specs=[
            pl.BlockSpec((tm, d), row),
            pl.BlockSpec((tm, d), row),
            pl.BlockSpec((halo, d), lambda i: (jnp.maximum(i * hb - 1, 0), 0)),
            pl.BlockSpec((halo, d), lambda i: (jnp.minimum((i + 1) * hb, last), 0)),
            pl.BlockSpec((tm, d), row),
            pl.BlockSpec((None, 8, d), lambda i: (i, 0, 0)),
            _const_spec((8, d), lambda i: (0, 0)),
            _const_spec((1, d), lambda i: (0, 0)),
            _const_spec((None, d, d), lambda i: (layer, 0, 0)),
        ],
        out_specs=pl.BlockSpec((tm, d), row),
        out_shape=jax.ShapeDtypeStruct((m, d), F32),
        scratch_shapes=[pltpu.VMEM((tm, d), BF16)],
        compiler_params=_cparams(1),
        name="conv_out_proj",
    )(gb, z, z, z, x, modt, conv_w, conv_b, w_out)


def _tile_rows(m_p, m_s, l_s, n_b, tm):
    assert m_p % tm == 0 and l_s % tm == 0
    return np.concatenate([np.full(m_p // tm, n_b), np.repeat(np.arange(n_b), l_s // tm)])


def _pick_tile(pref, m_p, l_s):
    tm = pref
    while m_p % tm or l_s % tm:
        tm //= 2
    return tm


def kernel(x_prompt, x_sample, c, cache_k, cache_v, state_ssm_re, state_ssm_im, c_ctx, ada_w, ada_b, norm1_g, norm2_g, ab_w_in, ab_w_out, s5_lam_re, s5_lam_im, s5_log_dt, s5_b_re, s5_b_im, s5_c_re, s5_c_im, s5_d, s5_glu_w, s5_glu_b, q_norm_g, k_norm_g, na_rpb, conv_w_in, conv_w, conv_b, conv_w_out, mlp_w1, mlp_w2):
    n_bp, l_p, d = x_prompt.shape
    n_bs, l_s, _ = x_sample.shape
    depth = ada_w.shape[0]
    m_p, m_s = n_bp * l_p, n_bs * l_s
    m = m_p + m_s
    n_g, n_state = s5_lam_re.shape[2], s5_lam_re.shape[3]
    s5w = n_g * S5_GROUP_CH
    naw = (ab_w_in.shape[2] - s5w) // 3
    n_h = naw // HEAD_DIM
    past = cache_k.shape[2]
    d_ff = mlp_w1.shape[2]
    assert m_p % l_s == 0, "latent sequences must start on a sequence-length row block"
    assert l_p % S5_CHUNK == 0 and l_s % S5_CHUNK == 0
    c_p, c_s = l_p // S5_CHUNK, l_s // S5_CHUNK
    assert c_p & (c_p - 1) == 0 and c_s & (c_s - 1) == 0
    r_p, r_s = m_p // S5_CHUNK, m_s // S5_CHUNK
    width = S5_CHUNK * S5_GROUP_CH

    xs = [x_prompt.reshape(m_p, d), x_sample.reshape(m_s, d)]

    n_rows = -(-(n_bs + 1) // 16) * 16
    cvec = jnp.concatenate([c, c_ctx[None], jnp.zeros((n_rows - n_bs - 1, d), F32)], axis=0)
    mod = _ada_call(cvec, ada_w, ada_b)

    def mod_tiles(layer, tm):
        t = mod[layer][_tile_rows(m_p, m_s, l_s, n_bs, tm)].reshape(m // tm, 6, d)
        return jnp.pad(t, ((0, 0), (0, 2), (0, 0)))

    tm_big = _pick_tile(1024, m_p, l_s)
    tm_mid = _pick_tile(512, m_p, l_s)
    tf = min(1024, d_ff)

    n_pw = max(int(math.log2(c_s)), 1)
    w_u, w_y, pw = _s5prep_call(s5_lam_re, s5_lam_im, s5_log_dt, s5_b_re, s5_b_im, s5_c_re, s5_c_im,
                                s5_d, n_pw)
    n_h0 = -(-n_bs // 8) * 8
    h0_all = jnp.stack([state_ssm_re, state_ssm_im], axis=-2)
    h0_all = h0_all.transpose(1, 3, 0, 2, 4, 5).reshape(-1, n_g, n_bs, 4 * n_state)
    h0_all = jnp.pad(h0_all, ((0, 0), (0, 0), (0, n_h0 - n_bs), (0, 0)))
    cache_k4 = cache_k.reshape(n_bs, -1, past, naw)
    cache_v4 = cache_v.reshape(n_bs, -1, past, naw)
    ab_w_in_b, ab_w_out_b, glu_w_b = ab_w_in.astype(BF16), ab_w_out.astype(BF16), s5_glu_w.astype(BF16)
    conv_w_in_b, conv_w_out_b = conv_w_in.astype(BF16), conv_w_out.astype(BF16)
    mlp_w1_b, mlp_w2_b = mlp_w1.astype(BF16), mlp_w2.astype(BF16)

    new_k, new_v, new_re, new_im = [], [], [], []
    for layer in range(depth):
        g1 = norm1_g[layer].reshape(1, d)
        g2 = norm2_g[layer].reshape(1, d)
        if layer % 2 == 0:
            e = layer // 2
            u, q, k, v = _inproj_call(xs, mod_tiles(layer, tm_mid), g1, ab_w_in_b,
                                      q_norm_g[e].reshape(1, HEAD_DIM), k_norm_g[e].reshape(1, HEAD_DIM),
                                      layer=e, tm=tm_mid, s5w=s5w, naw=naw, n_p_tiles=m_p // tm_mid)
            new_k.append(k[:m_p].reshape(n_bp, l_p, n_h, HEAD_DIM))
            new_v.append(v[:m_p].reshape(n_bp, l_p, n_h, HEAD_DIM))
            g_tok, z_all = _s5_call(u, w_u[e], w_y[e], pw[e], h0_all[e],
                                    l_tile=l_s, n_p_tiles=m_p // l_s, cp=c_p, cs=c_s)
            z_p = z_all.reshape(n_g, n_bp, c_p, 2, 2, n_state)
            fin = jnp.stack([z_p[:, :, c_p - 1, 0], z_p[:, :, 0, 1]], axis=2)
            new_re.append(fin[:, :, :, 0].transpose(1, 2, 0, 3))
            new_im.append(fin[:, :, :, 1].transpose(1, 2, 0, 3))
            attn_p = _ctx_attn_call(q, k, v, n_seq=n_bp, seq_len=l_p)
            bias = _na_bias_call(na_rpb[e])
            attn_s = _na_call(q, k, v, cache_k4, cache_v4, bias, layer_e=e, n_b=n_bs, seq_len=l_s,
                              row_blk0=m_p // l_s)
            x = _about_call(g_tok, attn_p, attn_s, xs, mod_tiles(layer, tm_mid), glu_w_b,
                            s5_glu_b[e].reshape(1, s5w), ab_w_out_b, layer=e, tm=tm_mid,
                            n_p_tiles=m_p // tm_mid)
        else:
            x = xs[0]
            o = layer // 2
            gb, z = _convin_call(x, mod_tiles(layer, tm_mid), g1, conv_w_in_b, layer=o, tm=tm_mid)
            cw8 = jnp.pad(conv_w[o], ((0, 5), (0, 0)))
            x = _convout_call(gb, z, x, mod_tiles(layer, tm_mid), cw8, conv_b[o].reshape(1, d),
                              conv_w_out_b, layer=o, tm=tm_mid, n_p_tiles=m_p // tm_mid,
                              lp=l_p, ls=l_s)
        mlp = functools.partial(_mlp_call, x, mod_tiles(layer, tm_big), g2, mlp_w1_b, mlp_w2_b,
                                layer=layer, tm=tm_big, tf=tf)
        if layer < depth - 1:
            xs = [mlp()]
        else:
            y_prompt = mlp(tile0=0, n_tiles=m_p // tm_big).reshape(n_bp, l_p, d)
            y_sample = mlp(tile0=m_p // tm_big, n_tiles=m_s // tm_big).reshape(n_bs, l_s, d)

    return (y_prompt, y_sample, jnp.stack(new_k, axis=1), jnp.stack(new_v, axis=1),
            jnp.stack(new_re, axis=1), jnp.stack(new_im, axis=1))
```

```python
import functools
import math

import numpy as np
import jax
import jax.numpy as jnp
from jax import lax
from jax.experimental import pallas as pl
from jax.experimental.pallas import tpu as pltpu

F32 = jnp.float32
BF16 = jnp.bfloat16

NORM_EPS = 1e-6
NEG_INF = -1e30

S5_GROUP_CH = 16
HEAD_DIM = 128
GRID_W = 64
NA_WIN_R = 8
NA_WIN_C = 16
S5_CHUNK = 16
NA_QROWS = 4
NA_KROWS = NA_QROWS + NA_WIN_R
ROW_CHUNK = 256

VMEM_LIMIT_BYTES = 62 * 1024 * 1024


def _cparams(n_axes):
    return pltpu.CompilerParams(dimension_semantics=("arbitrary",) * n_axes,
                                vmem_limit_bytes=VMEM_LIMIT_BYTES)


def _const_spec(shape, index_map):
    return pl.BlockSpec(shape, index_map, pipeline_mode=pl.Buffered(1))


def _dot(a, b):
    return jnp.dot(a, b, preferred_element_type=F32)


def _dot_nt(a, b):
    return lax.dot_general(a, b, (((1,), (1,)), ((), ())), preferred_element_type=F32)


def _dot_split(a, b):
    a_hi = a.astype(BF16)
    a_lo = (a - a_hi.astype(F32)).astype(BF16)
    b_hi = b.astype(BF16)
    b_lo = (b - b_hi.astype(F32)).astype(BF16)
    return _dot(a_hi, b_hi) + _dot(a_hi, b_lo) + _dot(a_lo, b_hi)


def _modulate(x, g, shift, scale):
    ms = jnp.mean(x * x, axis=-1, keepdims=True)
    y = x * lax.rsqrt(ms + NORM_EPS) * g
    return y * (1.0 + scale) + shift


def _ada_kernel(c_ref, w_ref, b_ref, o_ref, *, nc):
    cv = c_ref[...]
    sc = (cv * jax.nn.sigmoid(cv)).astype(BF16)
    tn = w_ref.shape[1]
    for n0 in range(0, tn, nc):
        w = w_ref[:, n0:n0 + nc].astype(BF16)
        o_ref[:, n0:n0 + nc] = _dot(sc, w) + b_ref[:, n0:n0 + nc]


def _ada_call(cvec, ada_w, ada_b):
    depth, d, n6 = ada_w.shape
    rows = cvec.shape[0]
    tn = 1536 if n6 % 1536 == 0 else n6
    nc = 512 if tn % 512 == 0 else tn
    return pl.pallas_call(
        functools.partial(_ada_kernel, nc=nc),
        grid=(depth, n6 // tn),
        in_specs=[
            pl.BlockSpec((rows, d), lambda l, j: (0, 0)),
            pl.BlockSpec((None, d, tn), lambda l, j: (l, 0, j)),
            pl.BlockSpec((None, 1, tn), lambda l, j: (l, 0, j)),
        ],
        out_specs=pl.BlockSpec((None, rows, tn), lambda l, j: (l, 0, j)),
        out_shape=jax.ShapeDtypeStruct((depth, rows, n6), F32),
        compiler_params=_cparams(2),
        name="ada_params",
    )(cvec, ada_w, ada_b.reshape(depth, 1, n6))


def _x_specs(n_x, tm, d, n_p_tiles):
    if n_x == 1:
        return [pl.BlockSpec((tm, d), lambda i: (i, 0))]
    return [pl.BlockSpec((tm, d), lambda i: (jnp.minimum(i, n_p_tiles - 1), 0)),
            pl.BlockSpec((tm, d), lambda i: (jnp.maximum(i - n_p_tiles, 0), 0))]


def _for_part(i, n_p_tiles, refs, fn):
    if len(refs) == 1:
        fn(refs[0])
    else:
        pl.when(i < n_p_tiles)(lambda: fn(refs[0]))
        pl.when(i >= n_p_tiles)(lambda: fn(refs[1]))


def _inproj_kernel(*refs, n_x, n_p_tiles, s5w, naw, rc):
    x_refs = refs[:n_x]
    mod_ref, g_ref, w_ref, qg_ref, kg_ref, u_ref, q_ref, k_ref, v_ref, h_s = refs[n_x:]
    tm = u_ref.shape[0]
    hd = HEAD_DIM
    cw = min(4 * hd, naw)
    g, shift, scale = g_ref[...], mod_ref[0:1, :], mod_ref[1:2, :]
    qg = qg_ref[...]
    kg = kg_ref[...]

    def head_norm(t, gain):
        return t * lax.rsqrt(jnp.mean(t * t, axis=-1, keepdims=True) + NORM_EPS) * gain

    def run(x_ref):
        for r0 in range(0, tm, rc):
            rows = slice(r0, r0 + rc)
            h_s[rows, :] = _modulate(x_ref[rows, :], g, shift, scale).astype(h_s.dtype)
            h = h_s[rows, :]
            u_ref[rows, :] = _dot(h, w_ref[:, 0:s5w]).astype(u_ref.dtype)
            for c0 in range(0, naw, cw):
                qc = _dot(h, w_ref[:, s5w + c0:s5w + c0 + cw])
                for j in range(cw // hd):
                    q_ref[rows, c0 + j * hd:c0 + (j + 1) * hd] = head_norm(
                        qc[:, j * hd:(j + 1) * hd], qg).astype(q_ref.dtype)
                kc = _dot(h, w_ref[:, s5w + naw + c0:s5w + naw + c0 + cw])
                for j in range(cw // hd):
                    k_ref[rows, c0 + j * hd:c0 + (j + 1) * hd] = head_norm(
                        kc[:, j * hd:(j + 1) * hd], kg).astype(k_ref.dtype)
                v_ref[rows, c0:c0 + cw] = _dot(
                    h, w_ref[:, s5w + 2 * naw + c0:s5w + 2 * naw + c0 + cw]).astype(v_ref.dtype)

    _for_part(pl.program_id(0), n_p_tiles, x_refs, run)


def _inproj_call(xs, modt, g, w_in, qg, kg, *, layer, tm, s5w, naw, n_p_tiles):
    m = sum(x.shape[0] for x in xs)
    d = xs[0].shape[1]
    n_in = w_in.shape[2]
    rc = min(ROW_CHUNK, tm)
    row = lambda i: (i, 0)
    return pl.pallas_call(
        functools.partial(_inproj_kernel, n_x=len(xs), n_p_tiles=n_p_tiles, s5w=s5w, naw=naw, rc=rc),
        grid=(m // tm,),
        in_specs=_x_specs(len(xs), tm, d, n_p_tiles) + [
            pl.BlockSpec((None, 8, d), lambda i: (i, 0, 0)),
            _const_spec((1, d), lambda i: (0, 0)),
            _const_spec((None, d, n_in), lambda i: (layer, 0, 0)),
            _const_spec((1, HEAD_DIM), lambda i: (0, 0)),
            _const_spec((1, HEAD_DIM), lambda i: (0, 0)),
        ],
        out_specs=[
            pl.BlockSpec((tm, s5w), row),
            pl.BlockSpec((tm, naw), row),
            pl.BlockSpec((tm, naw), row),
            pl.BlockSpec((tm, naw), row),
        ],
        out_shape=[
            jax.ShapeDtypeStruct((m, s5w), F32),
            jax.ShapeDtypeStruct((m, naw), BF16),
            jax.ShapeDtypeStruct((m, naw), F32),
            jax.ShapeDtypeStruct((m, naw), F32),
        ],
        scratch_shapes=[pltpu.VMEM((tm, d), BF16)],
        compiler_params=_cparams(1),
        name="ab_in_proj",
    )(*xs, modt, g, w_in, qg, kg)


def _s5prep_kernel(row_ref, row2_ref, col_ref, btr_ref, bti_ref, cer_ref, cei_ref, d_ref,
                   wu_ref, wy_ref, pw_ref, *, n_state, n_pw):
    t_chunk = S5_CHUNK
    width = t_chunk * S5_GROUP_CH
    p = n_state
    lg_ch = int(math.log2(S5_GROUP_CH))
    r_i = lax.broadcasted_iota(jnp.int32, (width, 1), 0)
    c_i = lax.broadcasted_iota(jnp.int32, (1, width), 1)
    s_i = r_i >> lg_ch
    t_i = c_i >> lg_ch
    n_pow = -(-(t_chunk + 1) // 8) * 8

    def cmul(ar, ai, br, bi):
        return ar * br - ai * bi, ar * bi + ai * br

    acc = jnp.zeros((width, width), F32)
    for d in range(2):
        lr = row_ref[d, 0:1, 0:p]
        li = row_ref[d, 1:2, 0:p]
        dt = jnp.exp(row_ref[d, 2:3, 0:p])
        ar = lr * dt
        ai = li * dt
        er = jnp.exp(ar)
        abr = er * jnp.cos(ai)
        abi = er * jnp.sin(ai)
        den = lr * lr + li * li
        nr = abr - 1.0
        f_re = (nr * lr + abi * li) / den
        f_im = (abi * lr - nr * li) / den
        bb_re, bb_im = cmul(f_re, f_im, btr_ref[d], bti_ref[d])

        lrc = col_ref[d, :, 0:1]
        lic = col_ref[d, :, 1:2]
        dtc = jnp.exp(col_ref[d, :, 2:3])
        arc = lrc * dtc
        aic = lic * dtc
        ce_re = cer_ref[d]
        ce_im = cei_ref[d]

        k_r = lax.broadcasted_iota(jnp.int32, (n_pow, 1), 0).astype(F32)
        k_c = lax.broadcasted_iota(jnp.int32, (1, 128), 1).astype(F32)
        mag_r = jnp.exp(k_r * ar)
        tr_re, tr_im = mag_r * jnp.cos(k_r * ai), mag_r * jnp.sin(k_r * ai)
        mag_c = jnp.exp(arc * k_c)
        tc_re, tc_im = mag_c * jnp.cos(aic * k_c), mag_c * jnp.sin(aic * k_c)

        def pow_row(e, e_max):
            re = jnp.zeros((width, p), F32)
            im = jnp.zeros((width, p), F32)
            for kk in range(e_max + 1):
                hit = e == kk
                re = jnp.where(hit, tr_re[kk:kk + 1, :], re)
                im = jnp.where(hit, tr_im[kk:kk + 1, :], im)
            return re, im

        def pow_col(e, e_max):
            re = jnp.zeros((p, width), F32)
            im = jnp.zeros((p, width), F32)
            for kk in range(e_max + 1):
                hit = e == kk
                re = jnp.where(hit, tc_re[:, kk:kk + 1], re)
                im = jnp.where(hit, tc_im[:, kk:kk + 1], im)
            return re, im

        terms = [(jnp.zeros_like(s_i), jnp.zeros_like(t_i), s_i == t_i, 0)]
        blk = 2
        while blk <= t_chunk:
            lb = int(math.log2(blk))
            half = blk // 2
            mid_s = ((s_i >> lb) << lb) + half
            mid_t = ((t_i >> lb) << lb) + half
            same = (s_i >> lb) == (t_i >> lb)
            if d == 0:
                e_s, ok_s = mid_s - s_i, s_i < mid_s
                e_t, ok_t = t_i - mid_t, t_i >= mid_t
            else:
                e_s, ok_s = s_i - mid_s, s_i >= mid_s
                e_t, ok_t = mid_t - t_i, t_i < mid_t
            terms.append((jnp.maximum(e_s, 0), jnp.maximum(e_t, 0), same & ok_s & ok_t, half))
            blk *= 2
        for e_s, e_t, mask, e_max in terms:
            l_re, l_im = cmul(*pow_row(e_s, e_max), bb_re, bb_im)
            r_re, r_im = cmul(*pow_col(e_t, e_max), ce_re, ce_im)
            term = _dot_split(l_re, r_re) - _dot_split(l_im, r_im)
            acc = acc + jnp.where(mask, term, 0.0)

        if d == 0:
            ws_re, ws_im = cmul(*pow_row(t_chunk - 1 - s_i, t_chunk), bb_re, bb_im)
            ca_re, ca_im = cmul(*pow_col(t_i + 1, t_chunk), ce_re, ce_im)
        else:
            ws_re, ws_im = cmul(*pow_row(s_i, t_chunk), bb_re, bb_im)
            ca_re, ca_im = cmul(*pow_col(t_chunk - t_i, t_chunk), ce_re, ce_im)
        wu_ref[:, width + d * p:width + (d + 1) * p] = ws_re.astype(wu_ref.dtype)
        wu_ref[:, width + (2 + d) * p:width + (3 + d) * p] = ws_im.astype(wu_ref.dtype)
        wy_ref[d * p:(d + 1) * p, :] = ca_re.astype(wy_ref.dtype)
        wy_ref[(2 + d) * p:(3 + d) * p, :] = (-ca_im).astype(wy_ref.dtype)

    lr2 = row2_ref[0:1, :]
    li2 = row2_ref[1:2, :]
    dt2 = jnp.exp(row2_ref[2:3, :])
    k16 = float(t_chunk)
    mag = jnp.exp(k16 * (lr2 * dt2))
    pr = mag * jnp.cos(k16 * (li2 * dt2))
    pi = mag * jnp.sin(k16 * (li2 * dt2))
    for i in range(n_pw):
        pw_ref[2 * i:2 * i + 1, :] = pr
        pw_ref[2 * i + 1:2 * i + 2, :] = pi
        pr, pi = pr * pr - pi * pi, 2.0 * pr * pi

    wu_ref[:, 0:width] = (acc + jnp.where(r_i == c_i, d_ref[...], 0.0)).astype(wu_ref.dtype)


def _s5prep_call(lam_re, lam_im, log_dt, b_re, b_im, c_re, c_im, d_skip, n_pw):
    n_e, _, n_g, p = lam_re.shape
    n_ch = S5_GROUP_CH
    width = S5_CHUNK * n_ch
    assert 2 * p == 128, "state rows are packed as [fwd | bwd] in one 128-lane tile"
    n_pw_rows = -(-2 * n_pw // 8) * 8
    dtb = jnp.broadcast_to(log_dt[..., None], lam_re.shape)
    zeros = jnp.zeros_like(lam_re)
    rowp = jnp.stack([lam_re, lam_im, dtb] + [zeros] * 5, axis=-2)
    row2 = jnp.concatenate([rowp[:, 0], rowp[:, 1]], axis=-1)
    colp = jnp.stack([lam_re, lam_im, dtb] + [zeros] * 5, axis=-1)
    bt = lambda b: jnp.tile(jnp.swapaxes(b, -1, -2), (1, 1, 1, S5_CHUNK, 1))
    ce = lambda c: jnp.tile(jnp.swapaxes(c, -1, -2), (1, 1, 1, 1, S5_CHUNK))
    d_row = jnp.tile(d_skip.reshape(n_e, n_g, 1, n_ch), (1, 1, 1, S5_CHUNK))

    def dspec(shape):
        return pl.BlockSpec((None, 2, None) + shape, lambda e, g: (e, 0, g, 0, 0))

    def ospec(shape):
        return pl.BlockSpec((None, None) + shape, lambda e, g: (e, g, 0, 0))

    return pl.pallas_call(
        functools.partial(_s5prep_kernel, n_state=p, n_pw=n_pw),
        grid=(n_e, n_g),
        in_specs=[dspec((8, p)), ospec((8, 2 * p)), dspec((p, 8)), dspec((width, p)), dspec((width, p)),
                  dspec((p, width)), dspec((p, width)), ospec((1, width))],
        out_specs=[ospec((width, 2 * width)), ospec((width, width)), ospec((n_pw_rows, 2 * p))],
        out_shape=[
            jax.ShapeDtypeStruct((n_e, n_g, width, 2 * width), BF16),
            jax.ShapeDtypeStruct((n_e, n_g, width, width), BF16),
            jax.ShapeDtypeStruct((n_e, n_g, n_pw_rows, 2 * p), F32),
        ],
        compiler_params=_cparams(2),
        name="s5_prep",
    )(rowp, row2, colp, bt(b_re), bt(b_im), ce(c_re), ce(c_im), d_row)


def _gelu_tanh(y):
    return 0.5 * y * (1.0 + jnp.tanh(0.7978845608028654 * (y + 0.044715 * (y * y * y))))


def _block_transpose8(v, lane):
    for dist in (4, 2, 1):
        width = dist * S5_GROUP_CH
        low = (lane & width) == 0
        out = list(v)
        for a in range(8):
            if a & dist == 0:
                lo, hi = v[a], v[a + dist]
                out[a] = jnp.where(low, lo, pltpu.roll(hi, width, 1))
                out[a + dist] = jnp.where(low, pltpu.roll(lo, 128 - width, 1), hi)
        v = out
    return v


def _s5_kernel(u_ref, wu_ref, wy_ref, pw_ref, h0_ref, g_ref, z_ref, zs_s, ug_s, yg_s,
               *, n_p_tiles, cp, cs):
    tile = pl.program_id(1)
    t_chunk = S5_CHUNK
    n = u_ref.shape[0] // t_chunk
    ngb, width = wy_ref.shape[0], wy_ref.shape[1]
    half = width // 2
    lanes = u_ref.shape[1]

    is_fwd = lax.broadcasted_iota(jnp.int32, (1, half), 1) < half // 2
    is_fwd2 = (lax.broadcasted_iota(jnp.int32, (1, width), 1) & (half - 1)) < half // 2

    lane = lax.broadcasted_iota(jnp.int32, (1, lanes), 1)
    for s_hi in range(t_chunk // 8):
        w = _block_transpose8([u_ref[pl.ds(s_hi * 8 + a, n, stride=t_chunk), :] for a in range(8)], lane)
        for gi in range(ngb):
            ug_s[gi, :, s_hi * lanes:(s_hi + 1) * lanes] = w[gi].astype(ug_s.dtype)

    def reversal(cseq):
        r = lax.broadcasted_iota(jnp.int32, (n, n), 0)
        c = lax.broadcasted_iota(jnp.int32, (n, n), 1)
        lg = int(math.log2(cseq))
        hit = ((r >> lg) == (c >> lg)) & ((r & (cseq - 1)) + (c & (cseq - 1)) == cseq - 1)
        return jnp.where(hit, 1.0, 0.0).astype(BF16)

    def run(cseq, j):
        rev = reversal(cseq)
        for gi in range(ngb):
            ug = ug_s[gi]
            wu = wu_ref[gi]
            ys = _dot(ug, wu)
            ug_rev = _dot(rev, ug).astype(BF16)
            ys_rev = _dot(ug_rev, wu[:, width:])
            yg_s[gi] = ys[:, 0:width]
            zs_s[0, gi] = jnp.where(is_fwd, ys[:, width:width + half], ys_rev[:, 0:half])
            zs_s[1, gi] = jnp.where(is_fwd, ys[:, width + half:], ys_rev[:, half:])

        zr, zi = zs_s[0], zs_s[1]
        pos = lax.broadcasted_iota(jnp.int32, (1, n, 1), 1) & (cseq - 1)
        edge = pos == 0
        if j is not None:
            h0 = h0_ref[:, pl.ds(j, 1), :]
            h0r, h0i = h0[:, :, 0:half], h0[:, :, half:]
            pr, pi = pw_ref[:, 0:1, :], pw_ref[:, 1:2, :]
            zr = zr + jnp.where(edge, pr * h0r - pi * h0i, 0.0)
            zi = zi + jnp.where(edge, pr * h0i + pi * h0r, 0.0)
        k = 1
        i = 0
        while k < cseq:
            ar = pw_ref[:, 2 * i:2 * i + 1, :]
            ai = pw_ref[:, 2 * i + 1:2 * i + 2, :]
            sr, si = pltpu.roll(zr, k, 1), pltpu.roll(zi, k, 1)
            valid = pos >= k
            zr = zr + jnp.where(valid, ar * sr - ai * si, 0.0)
            zi = zi + jnp.where(valid, ar * si + ai * sr, 0.0)
            k *= 2
            i += 1
        if j is None:
            z_ref[...] = jnp.concatenate([zr, zi], axis=2)
        xr, xi = pltpu.roll(zr, 1, 1), pltpu.roll(zi, 1, 1)
        if j is None:
            zs_s[0], zs_s[1] = jnp.where(edge, 0.0, xr), jnp.where(edge, 0.0, xi)
        else:
            zs_s[0], zs_s[1] = jnp.where(edge, h0r, xr), jnp.where(edge, h0i, xi)

        for gi in range(ngb):
            x = jnp.concatenate([zs_s[0, gi], zs_s[1, gi]], axis=1)
            x_rev = _dot(rev, x.astype(BF16))
            x = jnp.where(is_fwd2, x, x_rev).astype(BF16)
            yg_s[gi] = _gelu_tanh(yg_s[gi] + _dot(x, wy_ref[gi]))

    pl.when(tile < n_p_tiles)(lambda: run(cp, None))
    pl.when(tile >= n_p_tiles)(lambda: run(cs, tile - n_p_tiles))

    for t_hi in range(t_chunk // 8):
        w = _block_transpose8([yg_s[gi, :, t_hi * lanes:(t_hi + 1) * lanes] for gi in range(ngb)], lane)
        for a in range(8):
            g_ref[pl.ds(t_hi * 8 + a, n, stride=t_chunk), :] = w[a]


def _s5_call(u, w_u, w_y, pw, h0, *, l_tile, n_p_tiles, cp, cs):
    m, s5w = u.shape
    n_g, width = w_y.shape[0], w_y.shape[1]
    ngb = min(8, n_g)
    lanes = ngb * S5_GROUP_CH
    assert lanes == 128 and ngb == 8 and S5_CHUNK % 8 == 0 and n_g % ngb == 0 and m % l_tile == 0
    n = l_tile // S5_CHUNK
    n_h0 = h0.shape[1]
    tok = pl.BlockSpec((l_tile, lanes), lambda cb, t: (t, cb))
    grp = lambda shape: pl.BlockSpec((ngb,) + shape, lambda cb, t: (cb, 0, 0))
    return pl.pallas_call(
        functools.partial(_s5_kernel, n_p_tiles=n_p_tiles, cp=cp, cs=cs),
        grid=(n_g // ngb, m // l_tile),
        in_specs=[tok, grp((width, 2 * width)), grp((width, width)), grp(pw.shape[1:]), grp((n_h0, width))],
        out_specs=[tok, pl.BlockSpec((ngb, n, width), lambda cb, t: (cb, jnp.minimum(t, n_p_tiles - 1), 0))],
        out_shape=[jax.ShapeDtypeStruct((m, s5w), F32),
                   jax.ShapeDtypeStruct((n_g, n_p_tiles * n, width), F32)],
        scratch_shapes=[pltpu.VMEM((2, ngb, n, width // 2), F32),
                        pltpu.VMEM((ngb, n, width), BF16),
                        pltpu.VMEM((ngb, n, width), F32)],
        compiler_params=_cparams(2),
        name="s5_core",
    )(u, w_u, w_y, pw, h0)


def _ctx_attn_kernel(q_ref, k_ref, v_ref, o_ref, *, scale):
    hd = HEAD_DIM
    for h in range(q_ref.shape[1] // hd):
        sl = slice(h * hd, (h + 1) * hd)
        s = _dot_nt(q_ref[:, sl], k_ref[:, sl].astype(BF16)) * scale
        m = jnp.max(s, axis=-1, keepdims=True)
        p = jnp.exp(s - m)
        l = jnp.sum(p, axis=-1, keepdims=True)
        o = _dot(p.astype(BF16), v_ref[:, sl].astype(BF16)) / l
        o_ref[:, sl] = o.astype(o_ref.dtype)


def _ctx_attn_call(q, k, v, *, n_seq, seq_len):
    naw = q.shape[1]
    spec = pl.BlockSpec((seq_len, naw), lambda b: (b, 0))
    return pl.pallas_call(
        functools.partial(_ctx_attn_kernel, scale=HEAD_DIM ** -0.5),
        grid=(n_seq,),
        in_specs=[spec, spec, spec],
        out_specs=spec,
        out_shape=jax.ShapeDtypeStruct((n_seq * seq_len, naw), BF16),
        compiler_params=_cparams(1),
        name="ctx_attn",
    )(q, k, v)


def _na_bias_kernel(rpb_ref, o_ref, t_s):
    h = pl.program_id(0)
    n_dr = 2 * NA_WIN_R - 1
    n_dc = 2 * NA_WIN_C - 1
    w = GRID_W
    qc = lax.broadcasted_iota(jnp.int32, (w, w), 0)
    kc = lax.broadcasted_iota(jnp.int32, (w, w), 1)
    dc = kc - qc + (NA_WIN_C - 1)
    c0 = jnp.clip(qc - NA_WIN_C // 2, 0, w - NA_WIN_C)
    col_ok = (kc >= c0) & (kc < c0 + NA_WIN_C)
    for dr in range(n_dr):
        def pick(j, t, dr=dr):
            return jnp.where(dc == j, rpb_ref[h * (n_dr * n_dc) + dr * n_dc + j], t)
        t = lax.fori_loop(0, n_dc, pick, jnp.zeros((w, w), F32))
        t_s[dr] = jnp.where(col_ok, t, NEG_INF)
    neg = jnp.full((w, w), NEG_INF, F32)
    patterns = [(0, lambda i: 0), (NA_WIN_R // 2, lambda i: i), (NA_WIN_R, lambda i: NA_QROWS)]
    for pat, (r_rel, r0_rel) in enumerate(patterns):
        for i in range(NA_QROWS):
            for kr in range(NA_KROWS):
                dr = kr - i - r_rel + (NA_WIN_R - 1)
                valid = r0_rel(i) <= kr < r0_rel(i) + NA_WIN_R
                o_ref[pat, i * w:(i + 1) * w, kr * w:(kr + 1) * w] = t_s[dr] if valid else neg


def _na_bias_call(rpb):
    n_h = rpb.shape[0]
    nq = NA_QROWS * GRID_W
    nk = NA_KROWS * GRID_W
    return pl.pallas_call(
        _na_bias_kernel,
        grid=(n_h,),
        in_specs=[pl.BlockSpec(memory_space=pltpu.SMEM)],
        out_specs=pl.BlockSpec((None, 3, nq, nk), lambda h: (h, 0, 0, 0)),
        out_shape=jax.ShapeDtypeStruct((n_h, 3, nq, nk), F32),
        scratch_shapes=[pltpu.VMEM((2 * NA_WIN_R - 1, GRID_W, GRID_W), F32)],
        compiler_params=_cparams(1),
        name="na_bias",
    )(rpb.reshape(-1))


def _na_kernel(q_ref, k_ref, v_ref, kc_ref, vc_ref, bb_ref, o_ref, kb_s, vb_s, kcb_s, vcb_s,
               *, rows, scale):
    w = GRID_W
    nq = NA_QROWS * w
    nk = NA_KROWS * w
    n_blk = rows // NA_QROWS
    kb_s[...] = k_ref[...].astype(BF16)
    vb_s[...] = v_ref[...].astype(BF16)
    kcb_s[...] = kc_ref[...].astype(BF16)
    vcb_s[...] = vc_ref[...].astype(BF16)

    def body(blk, carry):
        q_rows = pl.ds(pl.multiple_of(blk * nq, nq), nq)
        k_base = jnp.clip(blk * NA_QROWS - NA_WIN_R // 2, 0, rows - NA_KROWS)
        k_rows = pl.ds(pl.multiple_of(k_base * w, nq), nk)
        pat = jnp.where(blk == 0, 0, jnp.where(blk == n_blk - 1, 2, 1))
        q = q_ref[q_rows, :]
        s_loc = _dot_nt(q, kb_s[k_rows, :]) * scale + bb_ref[pat]
        s_ctx = _dot_nt(q, kcb_s[...]) * scale
        m = jnp.maximum(jnp.max(s_loc, axis=-1, keepdims=True), jnp.max(s_ctx, axis=-1, keepdims=True))
        p_loc = jnp.exp(s_loc - m)
        p_ctx = jnp.exp(s_ctx - m)
        l = jnp.sum(p_loc, axis=-1, keepdims=True) + jnp.sum(p_ctx, axis=-1, keepdims=True)
        o = _dot(p_loc.astype(BF16), vb_s[k_rows, :]) + _dot(p_ctx.astype(BF16), vcb_s[...])
        o_ref[q_rows, :] = (o / l).astype(o_ref.dtype)
        return carry

    lax.fori_loop(0, n_blk, body, 0, unroll=2)


def _na_call(q, k, v, cache_k, cache_v, bias, *, layer_e, n_b, seq_len, row_blk0):
    naw = q.shape[1]
    n_h = naw // HEAD_DIM
    past = cache_k.shape[2]
    rows = seq_len // GRID_W
    assert rows % NA_QROWS == 0 and rows >= NA_KROWS
    tok = pl.BlockSpec((seq_len, HEAD_DIM), lambda b, h: (row_blk0 + b, h))
    ctx = pl.BlockSpec((None, None, past, HEAD_DIM), lambda b, h: (b, layer_e, 0, h))
    return pl.pallas_call(
        functools.partial(_na_kernel, rows=rows, scale=HEAD_DIM ** -0.5),
        grid=(n_b, n_h),
        in_specs=[tok, tok, tok, ctx, ctx,
                  pl.BlockSpec((None,) + bias.shape[1:], lambda b, h: (h, 0, 0, 0))],
        out_specs=pl.BlockSpec((seq_len, HEAD_DIM), lambda b, h: (b, h)),
        out_shape=jax.ShapeDtypeStruct((n_b * seq_len, naw), BF16),
        scratch_shapes=[pltpu.VMEM((seq_len, HEAD_DIM), BF16), pltpu.VMEM((seq_len, HEAD_DIM), BF16),
                        pltpu.VMEM((past, HEAD_DIM), BF16), pltpu.VMEM((past, HEAD_DIM), BF16)],
        compiler_params=_cparams(2),
        name="na_attn",
    )(q, k, v, cache_k, cache_v, bias)


def _about_kernel(*refs, n_x, n_p_tiles, nc, rc):
    g_ref, ap_ref, as_ref = refs[:3]
    x_refs = refs[3:3 + n_x]
    mod_ref, gw_ref, gb_ref, w_ref, o_ref = refs[3 + n_x:]
    s5w = g_ref.shape[1]
    tm, d = o_ref.shape

    def run(part):
        a_ref, x_ref = part
        for r0 in range(0, tm, rc):
            rows = slice(r0, r0 + rc)
            g = g_ref[rows, :]
            gl = _dot(g.astype(BF16), gw_ref[...]) + gb_ref[...]
            s5o = (g * jax.nn.sigmoid(gl)).astype(BF16)
            a = a_ref[rows, :]
            for n0 in range(0, d, nc):
                y = _dot(s5o, w_ref[0:s5w, n0:n0 + nc]) + _dot(a, w_ref[s5w:, n0:n0 + nc])
                o_ref[rows, n0:n0 + nc] = x_ref[rows, n0:n0 + nc] + mod_ref[2:3, n0:n0 + nc] * y

    _for_part(pl.program_id(0), n_p_tiles, [(ap_ref, x_refs[0]), (as_ref, x_refs[-1])], run)


def _about_call(g, attn_p, attn_s, xs, modt, glu_w, glu_b, w_out, *, layer, tm, n_p_tiles):
    m = sum(x.shape[0] for x in xs)
    d = xs[0].shape[1]
    s5w = g.shape[1]
    naw = attn_p.shape[1]
    nc = min(512, d)
    row = lambda i: (i, 0)
    return pl.pallas_call(
        functools.partial(_about_kernel, n_x=len(xs), n_p_tiles=n_p_tiles, nc=nc, rc=min(ROW_CHUNK, tm)),
        grid=(m // tm,),
        in_specs=[pl.BlockSpec((tm, s5w), row)] + _x_specs(2, tm, naw, n_p_tiles)
        + _x_specs(len(xs), tm, d, n_p_tiles) + [
            pl.BlockSpec((None, 8, d), lambda i: (i, 0, 0)),
            _const_spec((None, s5w, s5w), lambda i: (layer, 0, 0)),
            _const_spec((1, s5w), lambda i: (0, 0)),
            _const_spec((None, s5w + naw, d), lambda i: (layer, 0, 0)),
        ],
        out_specs=pl.BlockSpec((tm, d), row),
        out_shape=jax.ShapeDtypeStruct((m, d), F32),
        compiler_params=_cparams(1),
        name="ab_out_proj",
    )(g, attn_p, attn_s, *xs, modt, glu_w, glu_b, w_out)


def _mlp_kernel(x_ref, mod_ref, g_ref, w1_ref, w2_ref, o_ref, h_s, a_s, *, rc, nc1, nc2):
    tm, d = o_ref.shape
    tf = w1_ref.shape[1]
    ta = a_s.shape[1]

    def chunk_dots(rows):
        h = h_s[rows, :]
        for f0 in range(0, tf, ta):
            for c0 in range(0, ta, nc1):
                a = jnp.maximum(_dot(h, w1_ref[:, f0 + c0:f0 + c0 + nc1]), 0.0)
                a_s[rows, c0:c0 + nc1] = (a * a).astype(a_s.dtype)
            a = a_s[rows, :]
            for n0 in range(0, d, nc2):
                o_ref[rows, n0:n0 + nc2] += mod_ref[5:6, n0:n0 + nc2] * _dot(a, w2_ref[f0:f0 + ta, n0:n0 + nc2])

    @pl.when(pl.program_id(1) == 0)
    def _():
        g, shift, scale = g_ref[...], mod_ref[3:4, :], mod_ref[4:5, :]
        for r0 in range(0, tm, rc):
            rows = slice(r0, r0 + rc)
            x = x_ref[rows, :]
            h_s[rows, :] = _modulate(x, g, shift, scale).astype(h_s.dtype)
            o_ref[rows, :] = x
            chunk_dots(rows)

    @pl.when(pl.program_id(1) != 0)
    def _():
        chunk_dots(slice(0, tm))


def _mlp_call(x, modt, g, w1, w2, *, layer, tm, tf, tile0=0, n_tiles=None):
    d = x.shape[1]
    n_tiles = x.shape[0] // tm if n_tiles is None else n_tiles
    m = n_tiles * tm
    d_ff = w1.shape[2]
    rc = min(ROW_CHUNK, tm)
    return pl.pallas_call(
        functools.partial(_mlp_kernel, rc=rc, nc1=min(256, tf), nc2=min(512, d)),
        grid=(n_tiles, d_ff // tf),
        in_specs=[
            pl.BlockSpec((tm, d), lambda i, f: (i + tile0, 0)),
            pl.BlockSpec((None, 8, d), lambda i, f: (i + tile0, 0, 0)),
            _const_spec((1, d), lambda i, f: (0, 0)),
            pl.BlockSpec((None, d, tf), lambda i, f: (layer, 0, f)),
            pl.BlockSpec((None, tf, d), lambda i, f: (layer, f, 0)),
        ],
        out_specs=pl.BlockSpec((tm, d), lambda i, f: (i, 0)),
        out_shape=jax.ShapeDtypeStruct((m, d), F32),
        scratch_shapes=[pltpu.VMEM((tm, d), BF16), pltpu.VMEM((tm, min(512, tf)), BF16)],
        compiler_params=_cparams(2),
        name="mlp",
    )(x, modt, g, w1, w2)


def _convin_kernel(x_ref, mod_ref, g_ref, w_ref, gb_ref, z_ref, h_s, *, rc, nc):
    tm, d = gb_ref.shape
    g, shift, scale = g_ref[...], mod_ref[0:1, :], mod_ref[1:2, :]
    for r0 in range(0, tm, rc):
        rows = slice(r0, r0 + rc)
        h_s[rows, :] = _modulate(x_ref[rows, :], g, shift, scale).astype(h_s.dtype)
        h = h_s[rows, :]
        for n0 in range(0, d, nc):
            gb_ref[rows, n0:n0 + nc] = _dot(h, w_ref[:, n0:n0 + nc]).astype(gb_ref.dtype)
            z_ref[rows, n0:n0 + nc] = (_dot(h, w_ref[:, d + n0:d + n0 + nc])
                                       * _dot(h, w_ref[:, 2 * d + n0:2 * d + n0 + nc])).astype(z_ref.dtype)


def _convin_call(x, modt, g, w_in, *, layer, tm):
    m, d = x.shape
    row = lambda i: (i, 0)
    return pl.pallas_call(
        functools.partial(_convin_kernel, rc=min(ROW_CHUNK, tm), nc=min(512, d)),
        grid=(m // tm,),
        in_specs=[
            pl.BlockSpec((tm, d), row),
            pl.BlockSpec((None, 8, d), lambda i: (i, 0, 0)),
            _const_spec((1, d), lambda i: (0, 0)),
            _const_spec((None, d, 3 * d), lambda i: (layer, 0, 0)),
        ],
        out_specs=[pl.BlockSpec((tm, d), row), pl.BlockSpec((tm, d), row)],
        out_shape=[jax.ShapeDtypeStruct((m, d), BF16), jax.ShapeDtypeStruct((m, d), BF16)],
        scratch_shapes=[pltpu.VMEM((tm, d), BF16)],
        compiler_params=_cparams(1),
        name="conv_in_proj",
    )(x, modt, g, w_in)


def _convout_kernel(gb_ref, z_ref, zp_ref, zn_ref, x_ref, mod_ref, cw_ref, cb_ref, w_ref, o_ref, t_s,
                    *, n_p_tiles, lp, ls, rc, cc, nc):
    i = pl.program_id(0)
    tm, d = z_ref.shape
    halo = zp_ref.shape[0]
    seq_mask = jnp.where(i < n_p_tiles, lp - 1, ls - 1)
    ridx = lax.broadcasted_iota(jnp.int32, (rc, 1), 0)
    chunk_aligned = lp % rc == 0 and ls % rc == 0
    for r0 in range(0, tm, rc):
        rows = slice(r0, r0 + rc)
        if chunk_aligned:
            starts = ((i * tm + r0) & seq_mask) == 0
            ends = ((i * tm + r0 + rc) & seq_mask) == 0
        else:
            pos = (i * tm + r0 + ridx) & seq_mask
            is_start = pos == 0
            is_end = pos == seq_mask
        for c0 in range(0, d, cc):
            cs = slice(c0, c0 + cc)
            z = z_ref[rows, cs].astype(F32)
            before = (zp_ref[halo - 1:halo, cs] if r0 == 0 else z_ref[r0 - 1:r0, cs]).astype(F32)
            after = (zn_ref[0:1, cs] if r0 + rc == tm else z_ref[r0 + rc:r0 + rc + 1, cs]).astype(F32)
            if chunk_aligned:
                before = jnp.where(starts, 0.0, before)
                after = jnp.where(ends, 0.0, after)
            z_prev = jnp.where(ridx == 0, before, pltpu.roll(z, 1, 0))
            z_next = jnp.where(ridx == rc - 1, after, pltpu.roll(z, rc - 1, 0))
            if not chunk_aligned:
                z_prev = jnp.where(is_start, 0.0, z_prev)
                z_next = jnp.where(is_end, 0.0, z_next)
            conv = z_prev * cw_ref[0:1, cs] + z * cw_ref[1:2, cs] + z_next * cw_ref[2:3, cs] + cb_ref[:, cs]
            t_s[rows, cs] = (gb_ref[rows, cs].astype(F32) * conv).astype(t_s.dtype)
        t = t_s[rows, :]
        for n0 in range(0, d, nc):
            o_ref[rows, n0:n0 + nc] = (x_ref[rows, n0:n0 + nc]
                                       + mod_ref[2:3, n0:n0 + nc] * _dot(t, w_ref[:, n0:n0 + nc]))


def _convout_call(gb, z, x, modt, conv_w, conv_b, w_out, *, layer, tm, n_p_tiles, lp, ls):
    m, d = x.shape
    halo = 16
    assert tm % halo == 0 and lp & (lp - 1) == 0 and ls & (ls - 1) == 0
    assert (n_p_tiles * tm) % ls == 0 or True
    hb = tm // halo
    last = m // halo - 1
    row = lambda i: (i, 0)
    return pl.pallas_call(
        functools.partial(_convout_kernel, n_p_tiles=n_p_tiles, lp=lp, ls=ls, rc=min(ROW_CHUNK, tm),
                          cc=min(256, d), nc=min(512, d)),
        grid=(m // tm,),
        in_specs=[
            pl.BlockSpec((tm, d), row),
            pl.BlockSpec((tm, d), row),
            pl.BlockSpec((halo, d), lambda i: (jnp.maximum(i * hb - 1, 0), 0)),
            pl.BlockSpec((halo, d), lambda i: (jnp.minimum((i + 1) * hb, last), 0)),
            pl.BlockSpec((tm, d), row),
            pl.BlockSpec((None, 8, d), lambda i: (i, 0, 0)),
            _const_spec((8, d), lambda i: (0, 0)),
            _const_spec((1, d), lambda i: (0, 0)),
            _const_spec((None, d, d), lambda i: (layer, 0, 0)),
        ],
        out_specs=pl.BlockSpec((tm, d), row),
        out_shape=jax.ShapeDtypeStruct((m, d), F32),
        scratch_shapes=[pltpu.VMEM((tm, d), BF16)],
        compiler_params=_cparams(1),
        name="conv_out_proj",
    )(gb, z, z, z, x, modt, conv_w, conv_b, w_out)


def _tile_rows(m_p, m_s, l_s, n_b, tm):
    assert m_p % tm == 0 and l_s % tm == 0
    return np.concatenate([np.full(m_p // tm, n_b), np.repeat(np.arange(n_b), l_s // tm)])


def _pick_tile(pref, m_p, l_s):
    tm = pref
    while m_p % tm or l_s % tm:
        tm //= 2
    return tm


def kernel(x_prompt, x_sample, c, cache_k, cache_v, state_ssm_re, state_ssm_im, c_ctx, ada_w, ada_b, norm1_g, norm2_g, ab_w_in, ab_w_out, s5_lam_re, s5_lam_im, s5_log_dt, s5_b_re, s5_b_im, s5_c_re, s5_c_im, s5_d, s5_glu_w, s5_glu_b, q_norm_g, k_norm_g, na_rpb, conv_w_in, conv_w, conv_b, conv_w_out, mlp_w1, mlp_w2):
    n_bp, l_p, d = x_prompt.shape
    n_bs, l_s, _ = x_sample.shape
    depth = ada_w.shape[0]
    m_p, m_s = n_bp * l_p, n_bs * l_s
    m = m_p + m_s
    n_g, n_state = s5_lam_re.shape[2], s5_lam_re.shape[3]
    s5w = n_g * S5_GROUP_CH
    naw = (ab_w_in.shape[2] - s5w) // 3
    n_h = naw // HEAD_DIM
    past = cache_k.shape[2]
    d_ff = mlp_w1.shape[2]
    assert m_p % l_s == 0, "latent sequences must start on a sequence-length row block"
    assert l_p % S5_CHUNK == 0 and l_s % S5_CHUNK == 0
    c_p, c_s = l_p // S5_CHUNK, l_s // S5_CHUNK
    assert c_p & (c_p - 1) == 0 and c_s & (c_s - 1) == 0
    r_p, r_s = m_p // S5_CHUNK, m_s // S5_CHUNK
    width = S5_CHUNK * S5_GROUP_CH

    xs = [x_prompt.reshape(m_p, d), x_sample.reshape(m_s, d)]

    n_rows = -(-(n_bs + 1) // 16) * 16
    cvec = jnp.concatenate([c, c_ctx[None], jnp.zeros((n_rows - n_bs - 1, d), F32)], axis=0)
    mod = _ada_call(cvec, ada_w, ada_b)

    def mod_tiles(layer, tm):
        t = mod[layer][_tile_rows(m_p, m_s, l_s, n_bs, tm)].reshape(m // tm, 6, d)
        return jnp.pad(t, ((0, 0), (0, 2), (0, 0)))

    tm_big = _pick_tile(1024, m_p, l_s)
    tm_mid = _pick_tile(512, m_p, l_s)
    tf = min(1024, d_ff)

    n_pw = max(int(math.log2(c_s)), 1)
    w_u, w_y, pw = _s5prep_call(s5_lam_re, s5_lam_im, s5_log_dt, s5_b_re, s5_b_im, s5_c_re, s5_c_im,
                                s5_d, n_pw)
    n_h0 = -(-n_bs // 8) * 8
    h0_all = jnp.stack([state_ssm_re, state_ssm_im], axis=2)
    h0_all = h0_all.transpose(1, 4, 0, 2, 3, 5).reshape(-1, n_g, n_bs, 4 * n_state)
    h0_all = jnp.pad(h0_all, ((0, 0), (0, 0), (0, n_h0 - n_bs), (0, 0)))
    cache_k4 = cache_k.reshape(n_bs, -1, past, naw)
    cache_v4 = cache_v.reshape(n_bs, -1, past, naw)
    ab_w_in_b, ab_w_out_b, glu_w_b = ab_w_in.astype(BF16), ab_w_out.astype(BF16), s5_glu_w.astype(BF16)
    conv_w_in_b, conv_w_out_b = conv_w_in.astype(BF16), conv_w_out.astype(BF16)
    mlp_w1_b, mlp_w2_b = mlp_w1.astype(BF16), mlp_w2.astype(BF16)

    new_k, new_v, new_re, new_im = [], [], [], []
    for layer in range(depth):
        g1 = norm1_g[layer].reshape(1, d)
        g2 = norm2_g[layer].reshape(1, d)
        if layer % 2 == 0:
            e = layer // 2
            u, q, k, v = _inproj_call(xs, mod_tiles(layer, tm_mid), g1, ab_w_in_b,
                                      q_norm_g[e].reshape(1, HEAD_DIM), k_norm_g[e].reshape(1, HEAD_DIM),
                                      layer=e, tm=tm_mid, s5w=s5w, naw=naw, n_p_tiles=m_p // tm_mid)
            new_k.append(k[:m_p].reshape(n_bp, l_p, n_h, HEAD_DIM))
            new_v.append(v[:m_p].reshape(n_bp, l_p, n_h, HEAD_DIM))
            g_tok, z_all = _s5_call(u, w_u[e], w_y[e], pw[e], h0_all[e],
                                    l_tile=l_s, n_p_tiles=m_p // l_s, cp=c_p, cs=c_s)
            z_p = z_all.reshape(n_g, n_bp, c_p, 4 * n_state)
            fin = z_p[:, :, c_p - 1].reshape(n_g, n_bp, 2, 2, n_state)
            new_re.append(fin[:, :, 0].transpose(1, 2, 0, 3))
            new_im.append(fin[:, :, 1].transpose(1, 2, 0, 3))
            attn_p = _ctx_attn_call(q, k, v, n_seq=n_bp, seq_len=l_p)
            bias = _na_bias_call(na_rpb[e])
            attn_s = _na_call(q, k, v, cache_k4, cache_v4, bias, layer_e=e, n_b=n_bs, seq_len=l_s,
                              row_blk0=m_p // l_s)
            x = _about_call(g_tok, attn_p, attn_s, xs, mod_tiles(layer, tm_mid), glu_w_b,
                            s5_glu_b[e].reshape(1, s5w), ab_w_out_b, layer=e, tm=tm_mid,
                            n_p_tiles=m_p // tm_mid)
        else:
            x = xs[0]
            o = layer // 2
            gb, z = _convin_call(x, mod_tiles(layer, tm_mid), g1, conv_w_in_b, layer=o, tm=tm_mid)
            cw8 = jnp.pad(conv_w[o], ((0, 5), (0, 0)))
            x = _convout_call(gb, z, x, mod_tiles(layer, tm_mid), cw8, conv_b[o].reshape(1, d),
                              conv_w_out_b, layer=o, tm=tm_mid, n_p_tiles=m_p // tm_mid,
                              lp=l_p, ls=l_s)
        mlp = functools.partial(_mlp_call, x, mod_tiles(layer, tm_big), g2, mlp_w1_b, mlp_w2_b,
                                layer=layer, tm=tm_big, tf=tf)
        if layer < depth - 1:
            xs = [mlp()]
        else:
            y_prompt = mlp(tile0=0, n_tiles=m_p // tm_big).reshape(n_bp, l_p, d)
            y_sample = mlp(tile0=m_p // tm_big, n_tiles=m_s // tm_big).reshape(n_bs, l_s, d)

    return (y_prompt, y_sample, jnp.stack(new_k, axis=1), jnp.stack(new_v, axis=1),
            jnp.stack(new_re, axis=1), jnp.stack(new_im, axis=1))
```

```python
import functools
import math

import numpy as np
import jax
import jax.numpy as jnp
from jax import lax
from jax.experimental import pallas as pl
from jax.experimental.pallas import tpu as pltpu

F32 = jnp.float32
BF16 = jnp.bfloat16

NORM_EPS = 1e-6
NEG_INF = -1e30

S5_GROUP_CH = 16
HEAD_DIM = 128
GRID_W = 64
NA_WIN_R = 8
NA_WIN_C = 16
S5_CHUNK = 16
NA_QROWS = 4
NA_KROWS = NA_QROWS + NA_WIN_R
ROW_CHUNK = 256

VMEM_LIMIT_BYTES = 62 * 1024 * 1024


def _cparams(n_axes):
    return pltpu.CompilerParams(dimension_semantics=("arbitrary",) * n_axes,
                                vmem_limit_bytes=VMEM_LIMIT_BYTES)


def _const_spec(shape, index_map):
    return pl.BlockSpec(shape, index_map, pipeline_mode=pl.Buffered(1))


def _dot(a, b):
    return jnp.dot(a, b, preferred_element_type=F32)


def _dot_nt(a, b):
    return lax.dot_general(a, b, (((1,), (1,)), ((), ())), preferred_element_type=F32)


def _dot_split(a, b):
    a_hi = a.astype(BF16)
    a_lo = (a - a_hi.astype(F32)).astype(BF16)
    b_hi = b.astype(BF16)
    b_lo = (b - b_hi.astype(F32)).astype(BF16)
    return _dot(a_hi, b_hi) + _dot(a_hi, b_lo) + _dot(a_lo, b_hi)


def _modulate(x, g, shift, scale):
    ms = jnp.mean(x * x, axis=-1, keepdims=True)
    y = x * lax.rsqrt(ms + NORM_EPS) * g
    return y * (1.0 + scale) + shift


def _ada_kernel(c_ref, w_ref, b_ref, o_ref, *, nc):
    cv = c_ref[...]
    sc = (cv * jax.nn.sigmoid(cv)).astype(BF16)
    tn = w_ref.shape[1]
    for n0 in range(0, tn, nc):
        w = w_ref[:, n0:n0 + nc].astype(BF16)
        o_ref[:, n0:n0 + nc] = _dot(sc, w) + b_ref[:, n0:n0 + nc]


def _ada_call(cvec, ada_w, ada_b):
    depth, d, n6 = ada_w.shape
    rows = cvec.shape[0]
    tn = 1536 if n6 % 1536 == 0 else n6
    nc = 512 if tn % 512 == 0 else tn
    return pl.pallas_call(
        functools.partial(_ada_kernel, nc=nc),
        grid=(depth, n6 // tn),
        in_specs=[
            pl.BlockSpec((rows, d), lambda l, j: (0, 0)),
            pl.BlockSpec((None, d, tn), lambda l, j: (l, 0, j)),
            pl.BlockSpec((None, 1, tn), lambda l, j: (l, 0, j)),
        ],
        out_specs=pl.BlockSpec((None, rows, tn), lambda l, j: (l, 0, j)),
        out_shape=jax.ShapeDtypeStruct((depth, rows, n6), F32),
        compiler_params=_cparams(2),
        name="ada_params",
    )(cvec, ada_w, ada_b.reshape(depth, 1, n6))


def _x_specs(n_x, tm, d, n_p_tiles):
    if n_x == 1:
        return [pl.BlockSpec((tm, d), lambda i: (i, 0))]
    return [pl.BlockSpec((tm, d), lambda i: (jnp.minimum(i, n_p_tiles - 1), 0)),
            pl.BlockSpec((tm, d), lambda i: (jnp.maximum(i - n_p_tiles, 0), 0))]


def _for_part(i, n_p_tiles, refs, fn):
    if len(refs) == 1:
        fn(refs[0])
    else:
        pl.when(i < n_p_tiles)(lambda: fn(refs[0]))
        pl.when(i >= n_p_tiles)(lambda: fn(refs[1]))


def _inproj_kernel(*refs, n_x, n_p_tiles, s5w, naw, rc):
    x_refs = refs[:n_x]
    mod_ref, g_ref, w_ref, qg_ref, kg_ref, u_ref, q_ref, k_ref, v_ref, h_s = refs[n_x:]
    tm = u_ref.shape[0]
    hd = HEAD_DIM
    cw = min(4 * hd, naw)
    g, shift, scale = g_ref[...], mod_ref[0:1, :], mod_ref[1:2, :]
    qg = qg_ref[...]
    kg = kg_ref[...]

    def head_norm(t, gain):
        return t * lax.rsqrt(jnp.mean(t * t, axis=-1, keepdims=True) + NORM_EPS) * gain

    def run(x_ref):
        for r0 in range(0, tm, rc):
            rows = slice(r0, r0 + rc)
            h_s[rows, :] = _modulate(x_ref[rows, :], g, shift, scale).astype(h_s.dtype)
            h = h_s[rows, :]
            u_ref[rows, :] = _dot(h, w_ref[:, 0:s5w]).astype(u_ref.dtype)
            for c0 in range(0, naw, cw):
                qc = _dot(h, w_ref[:, s5w + c0:s5w + c0 + cw])
                for j in range(cw // hd):
                    q_ref[rows, c0 + j * hd:c0 + (j + 1) * hd] = head_norm(
                        qc[:, j * hd:(j + 1) * hd], qg).astype(q_ref.dtype)
                kc = _dot(h, w_ref[:, s5w + naw + c0:s5w + naw + c0 + cw])
                for j in range(cw // hd):
                    k_ref[rows, c0 + j * hd:c0 + (j + 1) * hd] = head_norm(
                        kc[:, j * hd:(j + 1) * hd], kg).astype(k_ref.dtype)
                v_ref[rows, c0:c0 + cw] = _dot(
                    h, w_ref[:, s5w + 2 * naw + c0:s5w + 2 * naw + c0 + cw]).astype(v_ref.dtype)

    _for_part(pl.program_id(0), n_p_tiles, x_refs, run)


def _inproj_call(xs, modt, g, w_in, qg, kg, *, layer, tm, s5w, naw, n_p_tiles):
    m = sum(x.shape[0] for x in xs)
    d = xs[0].shape[1]
    n_in = w_in.shape[2]
    rc = min(ROW_CHUNK, tm)
    row = lambda i: (i, 0)
    return pl.pallas_call(
        functools.partial(_inproj_kernel, n_x=len(xs), n_p_tiles=n_p_tiles, s5w=s5w, naw=naw, rc=rc),
        grid=(m // tm,),
        in_specs=_x_specs(len(xs), tm, d, n_p_tiles) + [
            pl.BlockSpec((None, 8, d), lambda i: (i, 0, 0)),
            _const_spec((1, d), lambda i: (0, 0)),
            _const_spec((None, d, n_in), lambda i: (layer, 0, 0)),
            _const_spec((1, HEAD_DIM), lambda i: (0, 0)),
            _const_spec((1, HEAD_DIM), lambda i: (0, 0)),
        ],
        out_specs=[
            pl.BlockSpec((tm, s5w), row),
            pl.BlockSpec((tm, naw), row),
            pl.BlockSpec((tm, naw), row),
            pl.BlockSpec((tm, naw), row),
        ],
        out_shape=[
            jax.ShapeDtypeStruct((m, s5w), F32),
            jax.ShapeDtypeStruct((m, naw), BF16),
            jax.ShapeDtypeStruct((m, naw), F32),
            jax.ShapeDtypeStruct((m, naw), F32),
        ],
        scratch_shapes=[pltpu.VMEM((tm, d), BF16)],
        compiler_params=_cparams(1),
        name="ab_in_proj",
    )(*xs, modt, g, w_in, qg, kg)


def _s5prep_kernel(row_ref, row2_ref, col_ref, btr_ref, bti_ref, cer_ref, cei_ref, d_ref,
                   wu_ref, wy_ref, pw_ref, *, n_state, n_pw):
    t_chunk = S5_CHUNK
    width = t_chunk * S5_GROUP_CH
    p = n_state
    lg_ch = int(math.log2(S5_GROUP_CH))
    r_i = lax.broadcasted_iota(jnp.int32, (width, 1), 0)
    c_i = lax.broadcasted_iota(jnp.int32, (1, width), 1)
    s_i = r_i >> lg_ch
    t_i = c_i >> lg_ch
    n_pow = -(-(t_chunk + 1) // 8) * 8

    def cmul(ar, ai, br, bi):
        return ar * br - ai * bi, ar * bi + ai * br

    acc = jnp.zeros((width, width), F32)
    for d in range(2):
        lr = row_ref[d, 0:1, 0:p]
        li = row_ref[d, 1:2, 0:p]
        dt = jnp.exp(row_ref[d, 2:3, 0:p])
        ar = lr * dt
        ai = li * dt
        er = jnp.exp(ar)
        abr = er * jnp.cos(ai)
        abi = er * jnp.sin(ai)
        den = lr * lr + li * li
        nr = abr - 1.0
        f_re = (nr * lr + abi * li) / den
        f_im = (abi * lr - nr * li) / den
        bb_re, bb_im = cmul(f_re, f_im, btr_ref[d], bti_ref[d])

        lrc = col_ref[d, :, 0:1]
        lic = col_ref[d, :, 1:2]
        dtc = jnp.exp(col_ref[d, :, 2:3])
        arc = lrc * dtc
        aic = lic * dtc
        ce_re = cer_ref[d]
        ce_im = cei_ref[d]

        k_r = lax.broadcasted_iota(jnp.int32, (n_pow, 1), 0).astype(F32)
        k_c = lax.broadcasted_iota(jnp.int32, (1, 128), 1).astype(F32)
        mag_r = jnp.exp(k_r * ar)
        tr_re, tr_im = mag_r * jnp.cos(k_r * ai), mag_r * jnp.sin(k_r * ai)
        mag_c = jnp.exp(arc * k_c)
        tc_re, tc_im = mag_c * jnp.cos(aic * k_c), mag_c * jnp.sin(aic * k_c)

        def pow_row(e, e_max):
            re = jnp.zeros((width, p), F32)
            im = jnp.zeros((width, p), F32)
            for kk in range(e_max + 1):
                hit = e == kk
                re = jnp.where(hit, tr_re[kk:kk + 1, :], re)
                im = jnp.where(hit, tr_im[kk:kk + 1, :], im)
            return re, im

        def pow_col(e, e_max):
            re = jnp.zeros((p, width), F32)
            im = jnp.zeros((p, width), F32)
            for kk in range(e_max + 1):
                hit = e == kk
                re = jnp.where(hit, tc_re[:, kk:kk + 1], re)
                im = jnp.where(hit, tc_im[:, kk:kk + 1], im)
            return re, im

        terms = [(jnp.zeros_like(s_i), jnp.zeros_like(t_i), s_i == t_i, 0)]
        blk = 2
        while blk <= t_chunk:
            lb = int(math.log2(blk))
            half = blk // 2
            mid_s = ((s_i >> lb) << lb) + half
            mid_t = ((t_i >> lb) << lb) + half
            same = (s_i >> lb) == (t_i >> lb)
            if d == 0:
                e_s, ok_s = mid_s - s_i, s_i < mid_s
                e_t, ok_t = t_i - mid_t, t_i >= mid_t
            else:
                e_s, ok_s = s_i - mid_s, s_i >= mid_s
                e_t, ok_t = mid_t - t_i, t_i < mid_t
            terms.append((jnp.maximum(e_s, 0), jnp.maximum(e_t, 0), same & ok_s & ok_t, half))
            blk *= 2
        for e_s, e_t, mask, e_max in terms:
            l_re, l_im = cmul(*pow_row(e_s, e_max), bb_re, bb_im)
            r_re, r_im = cmul(*pow_col(e_t, e_max), ce_re, ce_im)
            term = _dot_split(l_re, r_re) - _dot_split(l_im, r_im)
            acc = acc + jnp.where(mask, term, 0.0)

        if d == 0:
            ws_re, ws_im = cmul(*pow_row(t_chunk - 1 - s_i, t_chunk), bb_re, bb_im)
            ca_re, ca_im = cmul(*pow_col(t_i + 1, t_chunk), ce_re, ce_im)
        else:
            ws_re, ws_im = cmul(*pow_row(s_i, t_chunk), bb_re, bb_im)
            ca_re, ca_im = cmul(*pow_col(t_chunk - t_i, t_chunk), ce_re, ce_im)
        wu_ref[:, width + d * p:width + (d + 1) * p] = ws_re.astype(wu_ref.dtype)
        wu_ref[:, width + (2 + d) * p:width + (3 + d) * p] = ws_im.astype(wu_ref.dtype)
        wy_ref[d * p:(d + 1) * p, :] = ca_re.astype(wy_ref.dtype)
        wy_ref[(2 + d) * p:(3 + d) * p, :] = (-ca_im).astype(wy_ref.dtype)

    lr2 = row2_ref[0:1, :]
    li2 = row2_ref[1:2, :]
    dt2 = jnp.exp(row2_ref[2:3, :])
    k16 = float(t_chunk)
    mag = jnp.exp(k16 * (lr2 * dt2))
    pr = mag * jnp.cos(k16 * (li2 * dt2))
    pi = mag * jnp.sin(k16 * (li2 * dt2))
    for i in range(n_pw):
        pw_ref[2 * i:2 * i + 1, :] = pr
        pw_ref[2 * i + 1:2 * i + 2, :] = pi
        pr, pi = pr * pr - pi * pi, 2.0 * pr * pi

    wu_ref[:, 0:width] = (acc + jnp.where(r_i == c_i, d_ref[...], 0.0)).astype(wu_ref.dtype)


def _s5prep_call(lam_re, lam_im, log_dt, b_re, b_im, c_re, c_im, d_skip, n_pw):
    n_e, _, n_g, p = lam_re.shape
    n_ch = S5_GROUP_CH
    width = S5_CHUNK * n_ch
    assert 2 * p == 128, "state rows are packed as [fwd | bwd] in one 128-lane tile"
    n_pw_rows = -(-2 * n_pw // 8) * 8
    dtb = jnp.broadcast_to(log_dt[..., None], lam_re.shape)
    zeros = jnp.zeros_like(lam_re)
    rowp = jnp.stack([lam_re, lam_im, dtb] + [zeros] * 5, axis=-2)
    row2 = jnp.concatenate([rowp[:, 0], rowp[:, 1]], axis=-1)
    colp = jnp.stack([lam_re, lam_im, dtb] + [zeros] * 5, axis=-1)
    bt = lambda b: jnp.tile(jnp.swapaxes(b, -1, -2), (1, 1, 1, S5_CHUNK, 1))
    ce = lambda c: jnp.tile(jnp.swapaxes(c, -1, -2), (1, 1, 1, 1, S5_CHUNK))
    d_row = jnp.tile(d_skip.reshape(n_e, n_g, 1, n_ch), (1, 1, 1, S5_CHUNK))

    def dspec(shape):
        return pl.BlockSpec((None, 2, None) + shape, lambda e, g: (e, 0, g, 0, 0))

    def ospec(shape):
        return pl.BlockSpec((None, None) + shape, lambda e, g: (e, g, 0, 0))

    return pl.pallas_call(
        functools.partial(_s5prep_kernel, n_state=p, n_pw=n_pw),
        grid=(n_e, n_g),
        in_specs=[dspec((8, p)), ospec((8, 2 * p)), dspec((p, 8)), dspec((width, p)), dspec((width, p)),
                  dspec((p, width)), dspec((p, width)), ospec((1, width))],
        out_specs=[ospec((width, 2 * width)), ospec((width, width)), ospec((n_pw_rows, 2 * p))],
        out_shape=[
            jax.ShapeDtypeStruct((n_e, n_g, width, 2 * width), BF16),
            jax.ShapeDtypeStruct((n_e, n_g, width, width), BF16),
            jax.ShapeDtypeStruct((n_e, n_g, n_pw_rows, 2 * p), F32),
        ],
        compiler_params=_cparams(2),
        name="s5_prep",
    )(rowp, row2, colp, bt(b_re), bt(b_im), ce(c_re), ce(c_im), d_row)


def _gelu_tanh(y):
    return 0.5 * y * (1.0 + jnp.tanh(0.7978845608028654 * (y + 0.044715 * (y * y * y))))


def _block_transpose8(v, lane):
    for dist in (4, 2, 1):
        width = dist * S5_GROUP_CH
        low = (lane & width) == 0
        out = list(v)
        for a in range(8):
            if a & dist == 0:
                lo, hi = v[a], v[a + dist]
                out[a] = jnp.where(low, lo, pltpu.roll(hi, width, 1))
                out[a + dist] = jnp.where(low, pltpu.roll(lo, 128 - width, 1), hi)
        v = out
    return v


def _s5_kernel(u_ref, wu_ref, wy_ref, pw_ref, h0_ref, g_ref, z_ref, zs_s, ug_s, yg_s,
               *, n_p_tiles, cp, cs):
    tile = pl.program_id(1)
    t_chunk = S5_CHUNK
    n = u_ref.shape[0] // t_chunk
    ngb, width = wy_ref.shape[0], wy_ref.shape[1]
    half = width // 2
    lanes = u_ref.shape[1]

    is_fwd = lax.broadcasted_iota(jnp.int32, (1, half), 1) < half // 2
    is_fwd2 = (lax.broadcasted_iota(jnp.int32, (1, width), 1) & (half - 1)) < half // 2

    lane = lax.broadcasted_iota(jnp.int32, (1, lanes), 1)
    for s_hi in range(t_chunk // 8):
        w = _block_transpose8([u_ref[pl.ds(s_hi * 8 + a, n, stride=t_chunk), :] for a in range(8)], lane)
        for gi in range(ngb):
            ug_s[gi, :, s_hi * lanes:(s_hi + 1) * lanes] = w[gi].astype(ug_s.dtype)

    def reversal(cseq):
        r = lax.broadcasted_iota(jnp.int32, (n, n), 0)
        c = lax.broadcasted_iota(jnp.int32, (n, n), 1)
        lg = int(math.log2(cseq))
        hit = ((r >> lg) == (c >> lg)) & ((r & (cseq - 1)) + (c & (cseq - 1)) == cseq - 1)
        return jnp.where(hit, 1.0, 0.0).astype(BF16)

    def run(cseq, j):
        rev = reversal(cseq)
        for gi in range(ngb):
            ug = ug_s[gi]
            wu = wu_ref[gi]
            ys = _dot(ug, wu)
            ug_rev = _dot(rev, ug).astype(BF16)
            ys_rev = _dot(ug_rev, wu[:, width:])
            yg_s[gi] = ys[:, 0:width]
            zs_s[0, gi] = jnp.where(is_fwd, ys[:, width:width + half], ys_rev[:, 0:half])
            zs_s[1, gi] = jnp.where(is_fwd, ys[:, width + half:], ys_rev[:, half:])

        zr, zi = zs_s[0], zs_s[1]
        pos = lax.broadcasted_iota(jnp.int32, (1, n, 1), 1) & (cseq - 1)
        edge = pos == 0
        if j is not None:
            h0 = h0_ref[:, pl.ds(j, 1), :]
            h0r, h0i = h0[:, :, 0:half], h0[:, :, half:]
            pr, pi = pw_ref[:, 0:1, :], pw_ref[:, 1:2, :]
            zr = zr + jnp.where(edge, pr * h0r - pi * h0i, 0.0)
            zi = zi + jnp.where(edge, pr * h0i + pi * h0r, 0.0)
        k = 1
        i = 0
        while k < cseq:
            ar = pw_ref[:, 2 * i:2 * i + 1, :]
            ai = pw_ref[:, 2 * i + 1:2 * i + 2, :]
            sr, si = pltpu.roll(zr, k, 1), pltpu.roll(zi, k, 1)
            valid = pos >= k
            zr = zr + jnp.where(valid, ar * sr - ai * si, 0.0)
            zi = zi + jnp.where(valid, ar * si + ai * sr, 0.0)
            k *= 2
            i += 1
        if j is None:
            z_ref[...] = jnp.concatenate([zr, zi], axis=2)
        xr, xi = pltpu.roll(zr, 1, 1), pltpu.roll(zi, 1, 1)
        if j is None:
            zs_s[0], zs_s[1] = jnp.where(edge, 0.0, xr), jnp.where(edge, 0.0, xi)
        else:
            zs_s[0], zs_s[1] = jnp.where(edge, h0r, xr), jnp.where(edge, h0i, xi)

        for gi in range(ngb):
            x = jnp.concatenate([zs_s[0, gi], zs_s[1, gi]], axis=1)
            x_rev = _dot(rev, x.astype(BF16))
            x = jnp.where(is_fwd2, x, x_rev).astype(BF16)
            yg_s[gi] = _gelu_tanh(yg_s[gi] + _dot(x, wy_ref[gi]))

    pl.when(tile < n_p_tiles)(lambda: run(cp, None))
    pl.when(tile >= n_p_tiles)(lambda: run(cs, tile - n_p_tiles))

    for t_hi in range(t_chunk // 8):
        w = _block_transpose8([yg_s[gi, :, t_hi * lanes:(t_hi + 1) * lanes] for gi in range(ngb)], lane)
        for a in range(8):
            g_ref[pl.ds(t_hi * 8 + a, n, stride=t_chunk), :] = w[a]


def _s5_call(u, w_u, w_y, pw, h0, *, l_tile, n_p_tiles, cp, cs):
    m, s5w = u.shape
    n_g, width = w_y.shape[0], w_y.shape[1]
    ngb = min(8, n_g)
    lanes = ngb * S5_GROUP_CH
    assert lanes == 128 and ngb == 8 and S5_CHUNK % 8 == 0 and n_g % ngb == 0 and m % l_tile == 0
    n = l_tile // S5_CHUNK
    n_h0 = h0.shape[1]
    tok = pl.BlockSpec((l_tile, lanes), lambda cb, t: (t, cb))
    grp = lambda shape: pl.BlockSpec((ngb,) + shape, lambda cb, t: (cb, 0, 0))
    return pl.pallas_call(
        functools.partial(_s5_kernel, n_p_tiles=n_p_tiles, cp=cp, cs=cs),
        grid=(n_g // ngb, m // l_tile),
        in_specs=[tok, grp((width, 2 * width)), grp((width, width)), grp(pw.shape[1:]), grp((n_h0, width))],
        out_specs=[tok, pl.BlockSpec((ngb, n, width), lambda cb, t: (cb, jnp.minimum(t, n_p_tiles - 1), 0))],
        out_shape=[jax.ShapeDtypeStruct((m, s5w), F32),
                   jax.ShapeDtypeStruct((n_g, n_p_tiles * n, width), F32)],
        scratch_shapes=[pltpu.VMEM((2, ngb, n, width // 2), F32),
                        pltpu.VMEM((ngb, n, width), BF16),
                        pltpu.VMEM((ngb, n, width), F32)],
        compiler_params=_cparams(2),
        name="s5_core",
    )(u, w_u, w_y, pw, h0)


def _ctx_attn_kernel(q_ref, k_ref, v_ref, o_ref, *, scale):
    hd = HEAD_DIM
    for h in range(q_ref.shape[1] // hd):
        sl = slice(h * hd, (h + 1) * hd)
        s = _dot_nt(q_ref[:, sl], k_ref[:, sl].astype(BF16)) * scale
        m = jnp.max(s, axis=-1, keepdims=True)
        p = jnp.exp(s - m)
        l = jnp.sum(p, axis=-1, keepdims=True)
        o = _dot(p.astype(BF16), v_ref[:, sl].astype(BF16)) / l
        o_ref[:, sl] = o.astype(o_ref.dtype)


def _ctx_attn_call(q, k, v, *, n_seq, seq_len):
    naw = q.shape[1]
    spec = pl.BlockSpec((seq_len, naw), lambda b: (b, 0))
    return pl.pallas_call(
        functools.partial(_ctx_attn_kernel, scale=HEAD_DIM ** -0.5),
        grid=(n_seq,),
        in_specs=[spec, spec, spec],
        out_specs=spec,
        out_shape=jax.ShapeDtypeStruct((n_seq * seq_len, naw), BF16),
        compiler_params=_cparams(1),
        name="ctx_attn",
    )(q, k, v)


def _na_bias_kernel(rpb_ref, o_ref, t_s):
    inv_scale = HEAD_DIM ** 0.5
    h = pl.program_id(0)
    n_dr = 2 * NA_WIN_R - 1
    n_dc = 2 * NA_WIN_C - 1
    w = GRID_W
    qc = lax.broadcasted_iota(jnp.int32, (w, w), 0)
    kc = lax.broadcasted_iota(jnp.int32, (w, w), 1)
    dc = kc - qc + (NA_WIN_C - 1)
    c0 = jnp.clip(qc - NA_WIN_C // 2, 0, w - NA_WIN_C)
    col_ok = (kc >= c0) & (kc < c0 + NA_WIN_C)
    for dr in range(n_dr):
        def pick(j, t, dr=dr):
            return jnp.where(dc == j, rpb_ref[h * (n_dr * n_dc) + dr * n_dc + j] * inv_scale, t)
        t = lax.fori_loop(0, n_dc, pick, jnp.zeros((w, w), F32))
        t_s[dr] = jnp.where(col_ok, t, NEG_INF)
    neg = jnp.full((w, w), NEG_INF, F32)
    patterns = [(0, lambda i: 0), (NA_WIN_R // 2, lambda i: i), (NA_WIN_R, lambda i: NA_QROWS)]
    for pat, (r_rel, r0_rel) in enumerate(patterns):
        for i in range(NA_QROWS):
            for kr in range(NA_KROWS):
                dr = kr - i - r_rel + (NA_WIN_R - 1)
                valid = r0_rel(i) <= kr < r0_rel(i) + NA_WIN_R
                o_ref[pat, i * w:(i + 1) * w, kr * w:(kr + 1) * w] = t_s[dr] if valid else neg


def _na_bias_call(rpb):
    n_h = rpb.shape[0]
    nq = NA_QROWS * GRID_W
    nk = NA_KROWS * GRID_W
    return pl.pallas_call(
        _na_bias_kernel,
        grid=(n_h,),
        in_specs=[pl.BlockSpec(memory_space=pltpu.SMEM)],
        out_specs=pl.BlockSpec((None, 3, nq, nk), lambda h: (h, 0, 0, 0)),
        out_shape=jax.ShapeDtypeStruct((n_h, 3, nq, nk), F32),
        scratch_shapes=[pltpu.VMEM((2 * NA_WIN_R - 1, GRID_W, GRID_W), F32)],
        compiler_params=_cparams(1),
        name="na_bias",
    )(rpb.reshape(-1))


def _na_kernel(q_ref, k_ref, v_ref, kc_ref, vc_ref, bb_ref, o_ref, kb_s, vb_s, kcb_s, vcb_s, s_s, p_s, l_s,
               *, rows, scale):
    w = GRID_W
    nq = NA_QROWS * w
    nk = NA_KROWS * w
    n_blk = rows // NA_QROWS
    kb_s[...] = k_ref[...].astype(BF16)
    vb_s[...] = v_ref[...].astype(BF16)
    kcb_s[...] = kc_ref[...].astype(BF16)
    vcb_s[...] = vc_ref[...].astype(BF16)

    def q_rows(blk):
        return pl.ds(pl.multiple_of(jnp.int32(blk) * nq, nq), nq)

    def k_rows(blk):
        k_base = jnp.clip(jnp.int32(blk) * NA_QROWS - NA_WIN_R // 2, 0, rows - NA_KROWS)
        return pl.ds(pl.multiple_of(k_base * w, nq), nk)

    def scores(blk, slot):
        pat = jnp.where(blk == 0, 0, jnp.where(blk == n_blk - 1, 2, 1))
        q = q_ref[q_rows(blk), :]
        s_s[slot, :, 0:nk] = _dot_nt(q, kb_s[k_rows(blk), :]) + bb_ref[pat]
        s_s[slot, :, nk:] = _dot_nt(q, kcb_s[...])

    def softmax(slot):
        s = s_s[slot]
        p = jnp.exp2((s - jnp.max(s, axis=-1, keepdims=True)) * (scale * math.log2(math.e)))
        l_s[slot] = jnp.sum(p, axis=-1, keepdims=True)
        p_s[slot] = p.astype(p_s.dtype)

    def values(blk, slot):
        o = _dot(p_s[slot, :, 0:nk], vb_s[k_rows(blk), :]) + _dot(p_s[slot, :, nk:], vcb_s[...])
        o_ref[q_rows(blk), :] = (o / l_s[slot]).astype(o_ref.dtype)

    scores(0, 0)
    softmax(0)
    scores(1, 1)

    def body(jj, carry):
        values(2 * jj, 0)
        softmax(1)
        scores(2 * jj + 2, 0)
        values(2 * jj + 1, 1)
        softmax(0)
        scores(2 * jj + 3, 1)
        return carry

    lax.fori_loop(0, (n_blk - 2) // 2, body, 0)
    values(n_blk - 2, 0)
    softmax(1)
    values(n_blk - 1, 1)


def _na_call(q, k, v, cache_k, cache_v, bias, *, layer_e, n_b, seq_len, row_blk0):
    naw = q.shape[1]
    n_h = naw // HEAD_DIM
    past = cache_k.shape[2]
    rows = seq_len // GRID_W
    assert rows % (2 * NA_QROWS) == 0 and rows >= NA_KROWS
    nq, nk = NA_QROWS * GRID_W, NA_KROWS * GRID_W
    tok =pl.BlockSpec((seq_len, HEAD_DIM), lambda b, h: (row_blk0 + b, h))
    ctx = pl.BlockSpec((None, None, past, HEAD_DIM), lambda b, h: (b, layer_e, 0, h))
    return pl.pallas_call(
        functools.partial(_na_kernel, rows=rows, scale=HEAD_DIM ** -0.5),
        grid=(n_b, n_h),
        in_specs=[tok, tok, tok, ctx, ctx,
                  pl.BlockSpec((None,) + bias.shape[1:], lambda b, h: (h, 0, 0, 0))],
        out_specs=pl.BlockSpec((seq_len, HEAD_DIM), lambda b, h: (b, h)),
        out_shape=jax.ShapeDtypeStruct((n_b * seq_len, naw), BF16),
        scratch_shapes=[pltpu.VMEM((seq_len, HEAD_DIM), BF16), pltpu.VMEM((seq_len, HEAD_DIM), BF16),
                        pltpu.VMEM((past, HEAD_DIM), BF16), pltpu.VMEM((past, HEAD_DIM), BF16),
                        pltpu.VMEM((2, nq, nk + past), F32), pltpu.VMEM((2, nq, nk + past), BF16),
                        pltpu.VMEM((2, nq, 1), F32)],
        compiler_params=_cparams(2),
        name="na_attn",
    )(q, k, v, cache_k, cache_v, bias)


def _about_kernel(*refs, n_x, n_p_tiles, nc, rc):
    g_ref, ap_ref, as_ref = refs[:3]
    x_refs = refs[3:3 + n_x]
    mod_ref, gw_ref, gb_ref, w_ref, o_ref = refs[3 + n_x:]
    s5w = g_ref.shape[1]
    tm, d = o_ref.shape

    def run(part):
        a_ref, x_ref = part
        for r0 in range(0, tm, rc):
            rows = slice(r0, r0 + rc)
            g = g_ref[rows, :]
            gl = _dot(g.astype(BF16), gw_ref[...]) + gb_ref[...]
            s5o = (g * jax.nn.sigmoid(gl)).astype(BF16)
            a = a_ref[rows, :]
            for n0 in range(0, d, nc):
                y = _dot(s5o, w_ref[0:s5w, n0:n0 + nc]) + _dot(a, w_ref[s5w:, n0:n0 + nc])
                o_ref[rows, n0:n0 + nc] = x_ref[rows, n0:n0 + nc] + mod_ref[2:3, n0:n0 + nc] * y

    _for_part(pl.program_id(0), n_p_tiles, [(ap_ref, x_refs[0]), (as_ref, x_refs[-1])], run)


def _about_call(g, attn_p, attn_s, xs, modt, glu_w, glu_b, w_out, *, layer, tm, n_p_tiles):
    m = sum(x.shape[0] for x in xs)
    d = xs[0].shape[1]
    s5w = g.shape[1]
    naw = attn_p.shape[1]
    nc = min(512, d)
    row = lambda i: (i, 0)
    return pl.pallas_call(
        functools.partial(_about_kernel, n_x=len(xs), n_p_tiles=n_p_tiles, nc=nc, rc=min(ROW_CHUNK, tm)),
        grid=(m // tm,),
        in_specs=[pl.BlockSpec((tm, s5w), row)] + _x_specs(2, tm, naw, n_p_tiles)
        + _x_specs(len(xs), tm, d, n_p_tiles) + [
            pl.BlockSpec((None, 8, d), lambda i: (i, 0, 0)),
            _const_spec((None, s5w, s5w), lambda i: (layer, 0, 0)),
            _const_spec((1, s5w), lambda i: (0, 0)),
            _const_spec((None, s5w + naw, d), lambda i: (layer, 0, 0)),
        ],
        out_specs=pl.BlockSpec((tm, d), row),
        out_shape=jax.ShapeDtypeStruct((m, d), F32),
        compiler_params=_cparams(1),
        name="ab_out_proj",
    )(g, attn_p, attn_s, *xs, modt, glu_w, glu_b, w_out)


def _mlp_kernel(x_ref, mod_ref, g_ref, w1_ref, w2_ref, o_ref, h_s, a_s, *, rc, nc1, nc2):
    tm, d = o_ref.shape
    tf = w1_ref.shape[1]
    ta = a_s.shape[1]

    def chunk_dots(rows):
        h = h_s[rows, :]
        for f0 in range(0, tf, ta):
            for c0 in range(0, ta, nc1):
                a = jnp.maximum(_dot(h, w1_ref[:, f0 + c0:f0 + c0 + nc1]), 0.0)
                a_s[rows, c0:c0 + nc1] = (a * a).astype(a_s.dtype)
            a = a_s[rows, :]
            for n0 in range(0, d, nc2):
                o_ref[rows, n0:n0 + nc2] += mod_ref[5:6, n0:n0 + nc2] * _dot(a, w2_ref[f0:f0 + ta, n0:n0 + nc2])

    @pl.when(pl.program_id(1) == 0)
    def _():
        g, shift, scale = g_ref[...], mod_ref[3:4, :], mod_ref[4:5, :]
        for r0 in range(0, tm, rc):
            rows = slice(r0, r0 + rc)
            x = x_ref[rows, :]
            h_s[rows, :] = _modulate(x, g, shift, scale).astype(h_s.dtype)
            o_ref[rows, :] = x
            chunk_dots(rows)

    @pl.when(pl.program_id(1) != 0)
    def _():
        chunk_dots(slice(0, tm))


def _mlp_call(x, modt, g, w1, w2, *, layer, tm, tf, tile0=0, n_tiles=None):
    d = x.shape[1]
    n_tiles = x.shape[0] // tm if n_tiles is None else n_tiles
    m = n_tiles * tm
    d_ff = w1.shape[2]
    rc = min(ROW_CHUNK, tm)
    return pl.pallas_call(
        functools.partial(_mlp_kernel, rc=rc, nc1=min(256, tf), nc2=min(512, d)),
        grid=(n_tiles, d_ff // tf),
        in_specs=[
            pl.BlockSpec((tm, d), lambda i, f: (i + tile0, 0)),
            pl.BlockSpec((None, 8, d), lambda i, f: (i + tile0, 0, 0)),
            _const_spec((1, d), lambda i, f: (0, 0)),
            pl.BlockSpec((None, d, tf), lambda i, f: (layer, 0, f)),
            pl.BlockSpec((None, tf, d), lambda i, f: (layer, f, 0)),
        ],
        out_specs=pl.BlockSpec((tm, d), lambda i, f: (i, 0)),
        out_shape=jax.ShapeDtypeStruct((m, d), F32),
        scratch_shapes=[pltpu.VMEM((tm, d), BF16), pltpu.VMEM((tm, min(512, tf)), BF16)],
        compiler_params=_cparams(2),
        name="mlp",
    )(x, modt, g, w1, w2)


def _convin_kernel(x_ref, mod_ref, g_ref, w_ref, gb_ref, z_ref, h_s, *, rc, nc):
    tm, d = gb_ref.shape
    g, shift, scale = g_ref[...], mod_ref[0:1, :], mod_ref[1:2, :]
    for r0 in range(0, tm, rc):
        rows = slice(r0, r0 + rc)
        h_s[rows, :] = _modulate(x_ref[rows, :], g, shift, scale).astype(h_s.dtype)
        h = h_s[rows, :]
        for n0 in range(0, d, nc):
            gb_ref[rows, n0:n0 + nc] = _dot(h, w_ref[:, n0:n0 + nc]).astype(gb_ref.dtype)
            z_ref[rows, n0:n0 + nc] = (_dot(h, w_ref[:, d + n0:d + n0 + nc])
                                       * _dot(h, w_ref[:, 2 * d + n0:2 * d + n0 + nc])).astype(z_ref.dtype)


def _convin_call(x, modt, g, w_in, *, layer, tm):
    m, d = x.shape
    row = lambda i: (i, 0)
    return pl.pallas_call(
        functools.partial(_convin_kernel, rc=min(ROW_CHUNK, tm), nc=min(512, d)),
        grid=(m // tm,),
        in_specs=[
            pl.BlockSpec((tm, d), row),
            pl.BlockSpec((None, 8, d), lambda i: (i, 0, 0)),
            _const_spec((1, d), lambda i: (0, 0)),
            _const_spec((None, d, 3 * d), lambda i: (layer, 0, 0)),
        ],
        out_specs=[pl.BlockSpec((tm, d), row), pl.BlockSpec((tm, d), row)],
        out_shape=[jax.ShapeDtypeStruct((m, d), BF16), jax.ShapeDtypeStruct((m, d), BF16)],
        scratch_shapes=[pltpu.VMEM((tm, d), BF16)],
        compiler_params=_cparams(1),
        name="conv_in_proj",
    )(x, modt, g, w_in)


def _convout_kernel(gb_ref, z_ref, zp_ref, zn_ref, x_ref, mod_ref, cw_ref, cb_ref, w_ref, o_ref, t_s,
                    *, n_p_tiles, lp, ls, rc, cc, nc):
    i = pl.program_id(0)
    tm, d = z_ref.shape
    halo = zp_ref.shape[0]
    seq_mask = jnp.where(i < n_p_tiles, lp - 1, ls - 1)
    ridx = lax.broadcasted_iota(jnp.int32, (rc, 1), 0)
    chunk_aligned = lp % rc == 0 and ls % rc == 0
    for r0 in range(0, tm, rc):
        rows = slice(r0, r0 + rc)
        if chunk_aligned:
            starts = ((i * tm + r0) & seq_mask) == 0
            ends = ((i * tm + r0 + rc) & seq_mask) == 0
        else:
            pos = (i * tm + r0 + ridx) & seq_mask
            is_start = pos == 0
            is_end = pos == seq_mask
        for c0 in range(0, d, cc):
            cs = slice(c0, c0 + cc)
            z = z_ref[rows, cs].astype(F32)
            before = (zp_ref[halo - 1:halo, cs] if r0 == 0 else z_ref[r0 - 1:r0, cs]).astype(F32)
            after = (zn_ref[0:1, cs] if r0 + rc == tm else z_ref[r0 + rc:r0 + rc + 1, cs]).astype(F32)
            if chunk_aligned:
                before = jnp.where(starts, 0.0, before)
                after = jnp.where(ends, 0.0, after)
            z_prev = jnp.where(ridx == 0, before, pltpu.roll(z, 1, 0))
            z_next = jnp.where(ridx == rc - 1, after, pltpu.roll(z, rc - 1, 0))
            if not chunk_aligned:
                z_prev = jnp.where(is_start, 0.0, z_prev)
                z_next = jnp.where(is_end, 0.0, z_next)
            conv = z_prev * cw_ref[0:1, cs] + z * cw_ref[1:2, cs] + z_next * cw_ref[2:3, cs] + cb_ref[:, cs]
            t_s[rows, cs] = (gb_ref[rows, cs].astype(F32) * conv).astype(t_s.dtype)
        t = t_s[rows, :]
        for n0 in range(0, d, nc):
            o_ref[rows, n0:n0 + nc] = (x_ref[rows, n0:n0 + nc]
                                       + mod_ref[2:3, n0:n0 + nc] * _dot(t, w_ref[:, n0:n0 + nc]))


def _convout_call(gb, z, x, modt, conv_w, conv_b, w_out, *, layer, tm, n_p_tiles, lp, ls):
    m, d = x.shape
    halo = 16
    assert tm % halo == 0 and lp & (lp - 1) == 0 and ls & (ls - 1) == 0
    assert (n_p_tiles * tm) % ls == 0 or True
    hb = tm // halo
    last = m // halo - 1
    row = lambda i: (i, 0)
    return pl.pallas_call(
        functools.partial(_convout_kernel, n_p_tiles=n_p_tiles, lp=lp, ls=ls, rc=min(ROW_CHUNK, tm),
                          cc=min(256, d), nc=min(512, d)),
        grid=(m // tm,),
        in_specs=[
            pl.BlockSpec((tm, d), row),
            pl.BlockSpec((tm, d), row),
            pl.BlockSpec((halo, d), lambda i: (jnp.maximum(i * hb - 1, 0), 0)),
            pl.BlockSpec((halo, d), lambda i: (jnp.minimum((i + 1) * hb, last), 0)),
            pl.BlockSpec((tm, d), row),
            pl.BlockSpec((None, 8, d), lambda i: (i, 0, 0)),
            _const_spec((8, d), lambda i: (0, 0)),
            _const_spec((1, d), lambda i: (0, 0)),
            _const_spec((None, d, d), lambda i: (layer, 0, 0)),
        ],
        out_specs=pl.BlockSpec((tm, d), row),
        out_shape=jax.ShapeDtypeStruct((m, d), F32),
        scratch_shapes=[pltpu.VMEM((tm, d), BF16)],
        compiler_params=_cparams(1),
        name="conv_out_proj",
    )(gb, z, z, z, x, modt, conv_w, conv_b, w_out)


def _tile_rows(m_p, m_s, l_s, n_b, tm):
    assert m_p % tm == 0 and l_s % tm == 0
    return np.concatenate([np.full(m_p // tm, n_b), np.repeat(np.arange(n_b), l_s // tm)])


def _pick_tile(pref, m_p, l_s):
    tm = pref
    while m_p % tm or l_s % tm:
        tm //= 2
    return tm


def kernel(x_prompt, x_sample, c, cache_k, cache_v, state_ssm_re, state_ssm_im, c_ctx, ada_w, ada_b, norm1_g, norm2_g, ab_w_in, ab_w_out, s5_lam_re, s5_lam_im, s5_log_dt, s5_b_re, s5_b_im, s5_c_re, s5_c_im, s5_d, s5_glu_w, s5_glu_b, q_norm_g, k_norm_g, na_rpb, conv_w_in, conv_w, conv_b, conv_w_out, mlp_w1, mlp_w2):
    n_bp, l_p, d = x_prompt.shape
    n_bs, l_s, _ = x_sample.shape
    depth = ada_w.shape[0]
    m_p, m_s = n_bp * l_p, n_bs * l_s
    m = m_p + m_s
    n_g, n_state = s5_lam_re.shape[2], s5_lam_re.shape[3]
    s5w = n_g * S5_GROUP_CH
    naw = (ab_w_in.shape[2] - s5w) // 3
    n_h = naw // HEAD_DIM
    past = cache_k.shape[2]
    d_ff = mlp_w1.shape[2]
    assert m_p % l_s == 0, "latent sequences must start on a sequence-length row block"
    assert l_p % S5_CHUNK == 0 and l_s % S5_CHUNK == 0
    c_p, c_s = l_p // S5_CHUNK, l_s // S5_CHUNK
    assert c_p & (c_p - 1) == 0 and c_s & (c_s - 1) == 0
    r_p, r_s = m_p // S5_CHUNK, m_s // S5_CHUNK
    width = S5_CHUNK * S5_GROUP_CH

    xs = [x_prompt.reshape(m_p, d), x_sample.reshape(m_s, d)]

    n_rows = -(-(n_bs + 1) // 16) * 16
    cvec = jnp.concatenate([c, c_ctx[None], jnp.zeros((n_rows - n_bs - 1, d), F32)], axis=0)
    mod = _ada_call(cvec, ada_w, ada_b)

    def mod_tiles(layer, tm):
        t = mod[layer][_tile_rows(m_p, m_s, l_s, n_bs, tm)].reshape(m // tm, 6, d)
        return jnp.pad(t, ((0, 0), (0, 2), (0, 0)))

    tm_big = _pick_tile(1024, m_p, l_s)
    tm_mid = _pick_tile(512, m_p, l_s)
    tf = min(1024, d_ff)

    n_pw = max(int(math.log2(c_s)), 1)
    w_u, w_y, pw = _s5prep_call(s5_lam_re, s5_lam_im, s5_log_dt, s5_b_re, s5_b_im, s5_c_re, s5_c_im,
                                s5_d, n_pw)
    n_h0 = -(-n_bs // 8) * 8
    h0_all = jnp.stack([state_ssm_re, state_ssm_im], axis=2)
    h0_all = h0_all.transpose(1, 4, 0, 2, 3, 5).reshape(-1, n_g, n_bs, 4 * n_state)
    h0_all = jnp.pad(h0_all, ((0, 0), (0, 0), (0, n_h0 - n_bs), (0, 0)))
    cache_k4 = cache_k.reshape(n_bs, -1, past, naw)
    cache_v4 = cache_v.reshape(n_bs, -1, past, naw)
    ab_w_in_b, ab_w_out_b, glu_w_b = ab_w_in.astype(BF16), ab_w_out.astype(BF16), s5_glu_w.astype(BF16)
    conv_w_in_b, conv_w_out_b = conv_w_in.astype(BF16), conv_w_out.astype(BF16)
    mlp_w1_b, mlp_w2_b = mlp_w1.astype(BF16), mlp_w2.astype(BF16)

    new_k, new_v, new_re, new_im = [], [], [], []
    for layer in range(depth):
        g1 = norm1_g[layer].reshape(1, d)
        g2 = norm2_g[layer].reshape(1, d)
        if layer % 2 == 0:
            e = layer // 2
            u, q, k, v = _inproj_call(xs, mod_tiles(layer, tm_mid), g1, ab_w_in_b,
                                      q_norm_g[e].reshape(1, HEAD_DIM), k_norm_g[e].reshape(1, HEAD_DIM),
                                      layer=e, tm=tm_mid, s5w=s5w, naw=naw, n_p_tiles=m_p // tm_mid)
            new_k.append(k[:m_p].reshape(n_bp, l_p, n_h, HEAD_DIM))
            new_v.append(v[:m_p].reshape(n_bp, l_p, n_h, HEAD_DIM))
            g_tok, z_all = _s5_call(u, w_u[e], w_y[e], pw[e], h0_all[e],
                                    l_tile=l_s, n_p_tiles=m_p // l_s, cp=c_p, cs=c_s)
            z_p = z_all.reshape(n_g, n_bp, c_p, 4 * n_state)
            fin = z_p[:, :, c_p - 1].reshape(n_g, n_bp, 2, 2, n_state)
            new_re.append(fin[:, :, 0].transpose(1, 2, 0, 3))
            new_im.append(fin[:, :, 1].transpose(1, 2, 0, 3))
            attn_p = _ctx_attn_call(q, k, v, n_seq=n_bp, seq_len=l_p)
            bias = _na_bias_call(na_rpb[e])
            attn_s = _na_call(q, k, v, cache_k4, cache_v4, bias, layer_e=e, n_b=n_bs, seq_len=l_s,
                              row_blk0=m_p // l_s)
            x = _about_call(g_tok, attn_p, attn_s, xs, mod_tiles(layer, tm_mid), glu_w_b,
                            s5_glu_b[e].reshape(1, s5w), ab_w_out_b, layer=e, tm=tm_mid,
                            n_p_tiles=m_p // tm_mid)
        else:
            x = xs[0]
            o = layer // 2
            gb, z = _convin_call(x, mod_tiles(layer, tm_mid), g1, conv_w_in_b, layer=o, tm=tm_mid)
            cw8 = jnp.pad(conv_w[o], ((0, 5), (0, 0)))
            x = _convout_call(gb, z, x, mod_tiles(layer, tm_mid), cw8, conv_b[o].reshape(1, d),
                              conv_w_out_b, layer=o, tm=tm_mid, n_p_tiles=m_p // tm_mid,
                              lp=l_p, ls=l_s)
        mlp = functools.partial(_mlp_call, x, mod_tiles(layer, tm_big), g2, mlp_w1_b, mlp_w2_b,
                                layer=layer, tm=tm_big, tf=tf)
        if layer < depth - 1:
            xs = [mlp()]
        else:
            y_prompt = mlp(tile0=0, n_tiles=m_p // tm_big).reshape(n_bp, l_p, d)
            y_sample = mlp(tile0=m_p // tm_big, n_tiles=m_s // tm_big).reshape(n_bs, l_s, d)

    return (y_prompt, y_sample, jnp.stack(new_k, axis=1), jnp.stack(new_v, axis=1),
            jnp.stack(new_re, axis=1), jnp.stack(new_im, axis=1))
```

```python
import functools
import math

import numpy as np
import jax
import jax.numpy as jnp
from jax import lax
from jax.experimental import pallas as pl
from jax.experimental.pallas import tpu as pltpu

F32 = jnp.float32
BF16 = jnp.bfloat16

NORM_EPS = 1e-6
NEG_INF = -1e30

S5_GROUP_CH = 16
HEAD_DIM = 128
GRID_W = 64
NA_WIN_R = 8
NA_WIN_C = 16
S5_CHUNK = 16
NA_QROWS = 4
NA_KROWS = NA_QROWS + NA_WIN_R
ROW_CHUNK = 256

VMEM_LIMIT_BYTES = 62 * 1024 * 1024


def _cparams(n_axes):
    return pltpu.CompilerParams(dimension_semantics=("arbitrary",) * n_axes,
                                vmem_limit_bytes=VMEM_LIMIT_BYTES)


def _const_spec(shape, index_map):
    return pl.BlockSpec(shape, index_map, pipeline_mode=pl.Buffered(1))


def _dot(a, b):
    return jnp.dot(a, b, preferred_element_type=F32)


def _dot_nt(a, b):
    return lax.dot_general(a, b, (((1,), (1,)), ((), ())), preferred_element_type=F32)


def _dot_split(a, b):
    a_hi = a.astype(BF16)
    a_lo = (a - a_hi.astype(F32)).astype(BF16)
    b_hi = b.astype(BF16)
    b_lo = (b - b_hi.astype(F32)).astype(BF16)
    return _dot(a_hi, b_hi) + _dot(a_hi, b_lo) + _dot(a_lo, b_hi)


def _modulate(x, g, shift, scale):
    ms = jnp.mean(x * x, axis=-1, keepdims=True)
    y = x * lax.rsqrt(ms + NORM_EPS) * g
    return y * (1.0 + scale) + shift


def _ada_kernel(c_ref, w_ref, b_ref, o_ref, *, nc):
    cv = c_ref[...]
    sc = (cv * jax.nn.sigmoid(cv)).astype(BF16)
    tn = w_ref.shape[1]
    for n0 in range(0, tn, nc):
        w = w_ref[:, n0:n0 + nc].astype(BF16)
        o_ref[:, n0:n0 + nc] = _dot(sc, w) + b_ref[:, n0:n0 + nc]


def _ada_call(cvec, ada_w, ada_b):
    depth, d, n6 = ada_w.shape
    rows = cvec.shape[0]
    tn = 1536 if n6 % 1536 == 0 else n6
    nc = 512 if tn % 512 == 0 else tn
    return pl.pallas_call(
        functools.partial(_ada_kernel, nc=nc),
        grid=(depth, n6 // tn),
        in_specs=[
            pl.BlockSpec((rows, d), lambda l, j: (0, 0)),
            pl.BlockSpec((None, d, tn), lambda l, j: (l, 0, j)),
            pl.BlockSpec((None, 1, tn), lambda l, j: (l, 0, j)),
        ],
        out_specs=pl.BlockSpec((None, rows, tn), lambda l, j: (l, 0, j)),
        out_shape=jax.ShapeDtypeStruct((depth, rows, n6), F32),
        compiler_params=_cparams(2),
        name="ada_params",
    )(cvec, ada_w, ada_b.reshape(depth, 1, n6))


def _x_specs(n_x, tm, d, n_p_tiles):
    if n_x == 1:
        return [pl.BlockSpec((tm, d), lambda i: (i, 0))]
    return [pl.BlockSpec((tm, d), lambda i: (jnp.minimum(i, n_p_tiles - 1), 0)),
            pl.BlockSpec((tm, d), lambda i: (jnp.maximum(i - n_p_tiles, 0), 0))]


def _for_part(i, n_p_tiles, refs, fn):
    if len(refs) == 1:
        fn(refs[0])
    else:
        pl.when(i < n_p_tiles)(lambda: fn(refs[0]))
        pl.when(i >= n_p_tiles)(lambda: fn(refs[1]))


def _inproj_kernel(*refs, n_x, n_alias, n_p_tiles, l_p, s5w, naw, rc):
    x_refs = refs[:n_x]
    mod_ref, g_ref, w_ref, qg_ref, kg_ref = refs[n_x:n_x + 5]
    u_ref, q_ref, k_ref, v_ref, ck_ref, cv_ref, h_s = refs[n_x + 5 + n_alias:]
    i = pl.program_id(0)
    tm = u_ref.shape[0]
    n_heads = naw // HEAD_DIM

    def cache_store(c_ref, r0, col0, val):
        head = col0 // HEAD_DIM
        piece = min(rc, l_p)
        for t0 in range(r0, r0 + rc, piece):
            c_ref[t0 // l_p, pl.ds((t0 % l_p) * n_heads + head, piece, stride=n_heads), :] = (
                val[t0 - r0:t0 - r0 + piece])
    hd = HEAD_DIM
    cw = min(4 * hd, naw)
    g, shift, scale = g_ref[...], mod_ref[0:1, :], mod_ref[1:2, :]
    qg = qg_ref[...]
    kg = kg_ref[...]

    def head_norm(t, gain):
        return t * lax.rsqrt(jnp.mean(t * t, axis=-1, keepdims=True) + NORM_EPS) * gain

    def maybe_cache(prompt, fn):
        if prompt:
            fn()

    def run(x_ref, prompt):
        for r0 in range(0, tm, rc):
            rows = slice(r0, r0 + rc)
            h_s[rows, :] = _modulate(x_ref[rows, :], g, shift, scale).astype(h_s.dtype)
            h = h_s[rows, :]
            u_ref[rows, :] = _dot(h, w_ref[:, 0:s5w]).astype(u_ref.dtype)
            for c0 in range(0, naw, cw):
                qc = _dot(h, w_ref[:, s5w + c0:s5w + c0 + cw])
                for j in range(cw // hd):
                    q_ref[rows, c0 + j * hd:c0 + (j + 1) * hd] = head_norm(
                        qc[:, j * hd:(j + 1) * hd], qg).astype(q_ref.dtype)
                kc = _dot(h, w_ref[:, s5w + naw + c0:s5w + naw + c0 + cw])
                for j in range(cw // hd):
                    kh = head_norm(kc[:, j * hd:(j + 1) * hd], kg)
                    k_ref[rows, c0 + j * hd:c0 + (j + 1) * hd] = kh.astype(k_ref.dtype)
                    maybe_cache(prompt, functools.partial(cache_store, ck_ref, r0, c0 + j * hd, kh))
                vc = _dot(h, w_ref[:, s5w + 2 * naw + c0:s5w + 2 * naw + c0 + cw])
                v_ref[rows, c0:c0 + cw] = vc.astype(v_ref.dtype)
                for j in range(cw // hd):
                    maybe_cache(prompt, functools.partial(cache_store, cv_ref, r0, c0 + j * hd,
                                                          vc[:, j * hd:(j + 1) * hd]))

    pl.when(i < n_p_tiles)(lambda: run(x_refs[0], True))
    pl.when(i >= n_p_tiles)(lambda: run(x_refs[-1], False))


def _inproj_call(xs, modt, g, w_in, qg, kg, caches, *, layer, n_e, l_p, tm, s5w, naw, n_p_tiles):
    m = sum(x.shape[0] for x in xs)
    d = xs[0].shape[1]
    n_in = w_in.shape[2]
    rc = min(ROW_CHUNK, tm)
    n_heads = naw // HEAD_DIM
    assert tm % l_p == 0 and (rc % l_p == 0 or l_p % rc == 0)
    seqs = tm // l_p
    cache_shape = jax.ShapeDtypeStruct((n_p_tiles * seqs, n_e, l_p * n_heads, HEAD_DIM), F32)
    cache_spec = pl.BlockSpec((seqs, None, l_p * n_heads, HEAD_DIM),
                              lambda i: (jnp.minimum(i, n_p_tiles - 1), layer, 0, 0))
    n_alias = 0 if caches is None else 2
    alias_args = [] if caches is None else list(caches)
    n_in_args = len(xs) + 5
    row = lambda i: (i, 0)
    return pl.pallas_call(
        functools.partial(_inproj_kernel, n_x=len(xs), n_alias=n_alias, n_p_tiles=n_p_tiles, l_p=l_p,
                          s5w=s5w, naw=naw, rc=rc),
        grid=(m // tm,),
        in_specs=_x_specs(len(xs), tm, d, n_p_tiles) + [
            pl.BlockSpec((None, 8, d), lambda i: (i, 0, 0)),
            _const_spec((1, d), lambda i: (0, 0)),
            _const_spec((None, d, n_in), lambda i: (layer, 0, 0)),
            _const_spec((1, HEAD_DIM), lambda i: (0, 0)),
            _const_spec((1, HEAD_DIM), lambda i: (0, 0)),
        ] + [pl.BlockSpec(memory_space=pl.ANY)] * n_alias,
        out_specs=[
            pl.BlockSpec((tm, s5w), row),
            pl.BlockSpec((tm, naw), row),
            pl.BlockSpec((tm, naw), row),
            pl.BlockSpec((tm, naw), row),
            cache_spec, cache_spec,
        ],
        out_shape=[
            jax.ShapeDtypeStruct((m, s5w), F32),
            jax.ShapeDtypeStruct((m, naw), BF16),
            jax.ShapeDtypeStruct((m, naw), BF16),
            jax.ShapeDtypeStruct((m, naw), BF16),
            cache_shape, cache_shape,
        ],
        input_output_aliases={n_in_args + a: 4 + a for a in range(n_alias)},
        scratch_shapes=[pltpu.VMEM((tm, d), BF16)],
        compiler_params=_cparams(1),
        name="ab_in_proj",
    )(*xs, modt, g, w_in, qg, kg, *alias_args)


def _s5prep_kernel(row_ref, row2_ref, col_ref, btr_ref, bti_ref, cer_ref, cei_ref, d_ref,
                   wu_ref, wy_ref, pw_ref, *, n_state, n_pw):
    t_chunk = S5_CHUNK
    width = t_chunk * S5_GROUP_CH
    p = n_state
    lg_ch = int(math.log2(S5_GROUP_CH))
    r_i = lax.broadcasted_iota(jnp.int32, (width, 1), 0)
    c_i = lax.broadcasted_iota(jnp.int32, (1, width), 1)
    s_i = r_i >> lg_ch
    t_i = c_i >> lg_ch
    n_pow = -(-(t_chunk + 1) // 8) * 8

    def cmul(ar, ai, br, bi):
        return ar * br - ai * bi, ar * bi + ai * br

    acc = jnp.zeros((width, width), F32)
    for d in range(2):
        lr = row_ref[d, 0:1, 0:p]
        li = row_ref[d, 1:2, 0:p]
        dt = jnp.exp(row_ref[d, 2:3, 0:p])
        ar = lr * dt
        ai = li * dt
        er = jnp.exp(ar)
        abr = er * jnp.cos(ai)
        abi = er * jnp.sin(ai)
        den = lr * lr + li * li
        nr = abr - 1.0
        f_re = (nr * lr + abi * li) / den
        f_im = (abi * lr - nr * li) / den
        bb_re, bb_im = cmul(f_re, f_im, btr_ref[d], bti_ref[d])

        lrc = col_ref[d, :, 0:1]
        lic = col_ref[d, :, 1:2]
        dtc = jnp.exp(col_ref[d, :, 2:3])
        arc = lrc * dtc
        aic = lic * dtc
        ce_re = cer_ref[d]
        ce_im = cei_ref[d]

        k_r = lax.broadcasted_iota(jnp.int32, (n_pow, 1), 0).astype(F32)
        k_c = lax.broadcasted_iota(jnp.int32, (1, 128), 1).astype(F32)
        mag_r = jnp.exp(k_r * ar)
        tr_re, tr_im = mag_r * jnp.cos(k_r * ai), mag_r * jnp.sin(k_r * ai)
        mag_c = jnp.exp(arc * k_c)
        tc_re, tc_im = mag_c * jnp.cos(aic * k_c), mag_c * jnp.sin(aic * k_c)

        def pow_row(e, e_max):
            re = jnp.zeros((width, p), F32)
            im = jnp.zeros((width, p), F32)
            for kk in range(e_max + 1):
                hit = e == kk
                re = jnp.where(hit, tr_re[kk:kk + 1, :], re)
                im = jnp.where(hit, tr_im[kk:kk + 1, :], im)
            return re, im

        def pow_col(e, e_max):
            re = jnp.zeros((p, width), F32)
            im = jnp.zeros((p, width), F32)
            for kk in range(e_max + 1):
                hit = e == kk
                re = jnp.where(hit, tc_re[:, kk:kk + 1], re)
                im = jnp.where(hit, tc_im[:, kk:kk + 1], im)
            return re, im

        terms = [(jnp.zeros_like(s_i), jnp.zeros_like(t_i), s_i == t_i, 0)]
        blk = 2
        while blk <= t_chunk:
            lb = int(math.log2(blk))
            half = blk // 2
            mid_s = ((s_i >> lb) << lb) + half
            mid_t = ((t_i >> lb) << lb) + half
            same = (s_i >> lb) == (t_i >> lb)
            if d == 0:
                e_s, ok_s = mid_s - s_i, s_i < mid_s
                e_t, ok_t = t_i - mid_t, t_i >= mid_t
            else:
                e_s, ok_s = s_i - mid_s, s_i >= mid_s
                e_t, ok_t = mid_t - t_i, t_i < mid_t
            terms.append((jnp.maximum(e_s, 0), jnp.maximum(e_t, 0), same & ok_s & ok_t, half))
            blk *= 2
        for e_s, e_t, mask, e_max in terms:
            l_re, l_im = cmul(*pow_row(e_s, e_max), bb_re, bb_im)
            r_re, r_im = cmul(*pow_col(e_t, e_max), ce_re, ce_im)
            term = _dot_split(l_re, r_re) - _dot_split(l_im, r_im)
            acc = acc + jnp.where(mask, term, 0.0)

        if d == 0:
            ws_re, ws_im = cmul(*pow_row(t_chunk - 1 - s_i, t_chunk), bb_re, bb_im)
            ca_re, ca_im = cmul(*pow_col(t_i + 1, t_chunk), ce_re, ce_im)
        else:
            ws_re, ws_im = cmul(*pow_row(s_i, t_chunk), bb_re, bb_im)
            ca_re, ca_im = cmul(*pow_col(t_chunk - t_i, t_chunk), ce_re, ce_im)
        wu_ref[:, width + d * p:width + (d + 1) * p] = ws_re.astype(wu_ref.dtype)
        wu_ref[:, width + (2 + d) * p:width + (3 + d) * p] = ws_im.astype(wu_ref.dtype)
        wy_ref[d * p:(d + 1) * p, :] = ca_re.astype(wy_ref.dtype)
        wy_ref[(2 + d) * p:(3 + d) * p, :] = (-ca_im).astype(wy_ref.dtype)

    lr2 = row2_ref[0:1, :]
    li2 = row2_ref[1:2, :]
    dt2 = jnp.exp(row2_ref[2:3, :])
    k16 = float(t_chunk)
    mag = jnp.exp(k16 * (lr2 * dt2))
    pr = mag * jnp.cos(k16 * (li2 * dt2))
    pi = mag * jnp.sin(k16 * (li2 * dt2))
    for i in range(n_pw):
        pw_ref[2 * i:2 * i + 1, :] = pr
        pw_ref[2 * i + 1:2 * i + 2, :] = pi
        pr, pi = pr * pr - pi * pi, 2.0 * pr * pi

    wu_ref[:, 0:width] = (acc + jnp.where(r_i == c_i, d_ref[...], 0.0)).astype(wu_ref.dtype)


def _s5prep_call(lam_re, lam_im, log_dt, b_re, b_im, c_re, c_im, d_skip, n_pw):
    n_e, _, n_g, p = lam_re.shape
    n_ch = S5_GROUP_CH
    width = S5_CHUNK * n_ch
    assert 2 * p == 128, "state rows are packed as [fwd | bwd] in one 128-lane tile"
    n_pw_rows = -(-2 * n_pw // 8) * 8
    dtb = jnp.broadcast_to(log_dt[..., None], lam_re.shape)
    zeros = jnp.zeros_like(lam_re)
    rowp = jnp.stack([lam_re, lam_im, dtb] + [zeros] * 5, axis=-2)
    row2 = jnp.concatenate([rowp[:, 0], rowp[:, 1]], axis=-1)
    colp = jnp.stack([lam_re, lam_im, dtb] + [zeros] * 5, axis=-1)
    bt = lambda b: jnp.tile(jnp.swapaxes(b, -1, -2), (1, 1, 1, S5_CHUNK, 1))
    ce = lambda c: jnp.tile(jnp.swapaxes(c, -1, -2), (1, 1, 1, 1, S5_CHUNK))
    d_row = jnp.tile(d_skip.reshape(n_e, n_g, 1, n_ch), (1, 1, 1, S5_CHUNK))

    def dspec(shape):
        return pl.BlockSpec((None, 2, None) + shape, lambda e, g: (e, 0, g, 0, 0))

    def ospec(shape):
        return pl.BlockSpec((None, None) + shape, lambda e, g: (e, g, 0, 0))

    return pl.pallas_call(
        functools.partial(_s5prep_kernel, n_state=p, n_pw=n_pw),
        grid=(n_e, n_g),
        in_specs=[dspec((8, p)), ospec((8, 2 * p)), dspec((p, 8)), dspec((width, p)), dspec((width, p)),
                  dspec((p, width)), dspec((p, width)), ospec((1, width))],
        out_specs=[ospec((width, 2 * width)), ospec((width, width)), ospec((n_pw_rows, 2 * p))],
        out_shape=[
            jax.ShapeDtypeStruct((n_e, n_g, width, 2 * width), BF16),
            jax.ShapeDtypeStruct((n_e, n_g, width, width), BF16),
            jax.ShapeDtypeStruct((n_e, n_g, n_pw_rows, 2 * p), F32),
        ],
        compiler_params=_cparams(2),
        name="s5_prep",
    )(rowp, row2, colp, bt(b_re), bt(b_im), ce(c_re), ce(c_im), d_row)


def _gelu_tanh(y):
    return 0.5 * y * (1.0 + jnp.tanh(0.7978845608028654 * (y + 0.044715 * (y * y * y))))


def _block_transpose8(v, lane):
    for dist in (4, 2, 1):
        width = dist * S5_GROUP_CH
        low = (lane & width) == 0
        out = list(v)
        for a in range(8):
            if a & dist == 0:
                lo, hi = v[a], v[a + dist]
                out[a] = jnp.where(low, lo, pltpu.roll(hi, width, 1))
                out[a + dist] = jnp.where(low, pltpu.roll(lo, 128 - width, 1), hi)
        v = out
    return v


def _s5_kernel(u_ref, wu_ref, wy_ref, pw_ref, h0_ref, g_ref, z_ref, zs_s, ug_s, yg_s,
               *, n_p_tiles, cp, cs):
    tile = pl.program_id(1)
    t_chunk = S5_CHUNK
    n = u_ref.shape[0] // t_chunk
    ngb, width = wy_ref.shape[0], wy_ref.shape[1]
    half = width // 2
    lanes = u_ref.shape[1]

    is_fwd = lax.broadcasted_iota(jnp.int32, (1, half), 1) < half // 2
    is_fwd2 = (lax.broadcasted_iota(jnp.int32, (1, width), 1) & (half - 1)) < half // 2

    lane = lax.broadcasted_iota(jnp.int32, (1, lanes), 1)
    for s_hi in range(t_chunk // 8):
        w = _block_transpose8([u_ref[pl.ds(s_hi * 8 + a, n, stride=t_chunk), :] for a in range(8)], lane)
        for gi in range(ngb):
            ug_s[gi, :, s_hi * lanes:(s_hi + 1) * lanes] = w[gi].astype(ug_s.dtype)

    def reversal(cseq):
        r = lax.broadcasted_iota(jnp.int32, (n, n), 0)
        c = lax.broadcasted_iota(jnp.int32, (n, n), 1)
        lg = int(math.log2(cseq))
        hit = ((r >> lg) == (c >> lg)) & ((r & (cseq - 1)) + (c & (cseq - 1)) == cseq - 1)
        return jnp.where(hit, 1.0, 0.0).astype(BF16)

    def run(cseq, j):
        rev = reversal(cseq)
        for gi in range(ngb):
            ug = ug_s[gi]
            wu = wu_ref[gi]
            ys = _dot(ug, wu)
            ug_rev = _dot(rev, ug).astype(BF16)
            ys_rev = _dot(ug_rev, wu[:, width:])
            yg_s[gi] = ys[:, 0:width]
            zs_s[0, gi] = jnp.where(is_fwd, ys[:, width:width + half], ys_rev[:, 0:half])
            zs_s[1, gi] = jnp.where(is_fwd, ys[:, width + half:], ys_rev[:, half:])

        zr, zi = zs_s[0], zs_s[1]
        pos = lax.broadcasted_iota(jnp.int32, (1, n, 1), 1) & (cseq - 1)
        edge = pos == 0
        if j is not None:
            h0 = h0_ref[:, pl.ds(j, 1), :]
            h0r, h0i = h0[:, :, 0:half], h0[:, :, half:]
            pr, pi = pw_ref[:, 0:1, :], pw_ref[:, 1:2, :]
            zr = zr + jnp.where(edge, pr * h0r - pi * h0i, 0.0)
            zi = zi + jnp.where(edge, pr * h0i + pi * h0r, 0.0)
        k = 1
        i = 0
        while k < cseq:
            ar = pw_ref[:, 2 * i:2 * i + 1, :]
            ai = pw_ref[:, 2 * i + 1:2 * i + 2, :]
            sr, si = pltpu.roll(zr, k, 1), pltpu.roll(zi, k, 1)
            valid = pos >= k
            zr = zr + jnp.where(valid, ar * sr - ai * si, 0.0)
            zi = zi + jnp.where(valid, ar * si + ai * sr, 0.0)
            k *= 2
            i += 1
        if j is None:
            z_ref[...] = jnp.concatenate([zr, zi], axis=2)
        xr, xi = pltpu.roll(zr, 1, 1), pltpu.roll(zi, 1, 1)
        if j is None:
            zs_s[0], zs_s[1] = jnp.where(edge, 0.0, xr), jnp.where(edge, 0.0, xi)
        else:
            zs_s[0], zs_s[1] = jnp.where(edge, h0r, xr), jnp.where(edge, h0i, xi)

        for gi in range(ngb):
            x = jnp.concatenate([zs_s[0, gi], zs_s[1, gi]], axis=1)
            x_rev = _dot(rev, x.astype(BF16))
            x = jnp.where(is_fwd2, x, x_rev).astype(BF16)
            yg_s[gi] = _gelu_tanh(yg_s[gi] + _dot(x, wy_ref[gi]))

    pl.when(tile < n_p_tiles)(lambda: run(cp, None))
    pl.when(tile >= n_p_tiles)(lambda: run(cs, tile - n_p_tiles))

    for t_hi in range(t_chunk // 8):
        w = _block_transpose8([yg_s[gi, :, t_hi * lanes:(t_hi + 1) * lanes] for gi in range(ngb)], lane)
        for a in range(8):
            g_ref[pl.ds(t_hi * 8 + a, n, stride=t_chunk), :] = w[a]


def _s5_call(u, w_u, w_y, pw, h0, *, l_tile, n_p_tiles, cp, cs):
    m, s5w = u.shape
    n_g, width = w_y.shape[0], w_y.shape[1]
    ngb = min(8, n_g)
    lanes = ngb * S5_GROUP_CH
    assert lanes == 128 and ngb == 8 and S5_CHUNK % 8 == 0 and n_g % ngb == 0 and m % l_tile == 0
    n = l_tile // S5_CHUNK
    n_h0 = h0.shape[1]
    tok = pl.BlockSpec((l_tile, lanes), lambda cb, t: (t, cb))
    grp = lambda shape: pl.BlockSpec((ngb,) + shape, lambda cb, t: (cb, 0, 0))
    return pl.pallas_call(
        functools.partial(_s5_kernel, n_p_tiles=n_p_tiles, cp=cp, cs=cs),
        grid=(n_g // ngb, m // l_tile),
        in_specs=[tok, grp((width, 2 * width)), grp((width, width)), grp(pw.shape[1:]), grp((n_h0, width))],
        out_specs=[tok, pl.BlockSpec((ngb, n, width), lambda cb, t: (cb, jnp.minimum(t, n_p_tiles - 1), 0))],
        out_shape=[jax.ShapeDtypeStruct((m, s5w), F32),
                   jax.ShapeDtypeStruct((n_g, n_p_tiles * n, width), F32)],
        scratch_shapes=[pltpu.VMEM((2, ngb, n, width // 2), F32),
                        pltpu.VMEM((ngb, n, width), BF16),
                        pltpu.VMEM((ngb, n, width), F32)],
        compiler_params=_cparams(2),
        name="s5_core",
    )(u, w_u, w_y, pw, h0)


def _ctx_attn_kernel(q_ref, k_ref, v_ref, o_ref, *, scale):
    hd = HEAD_DIM
    for h in range(q_ref.shape[1] // hd):
        sl = slice(h * hd, (h + 1) * hd)
        s = _dot_nt(q_ref[:, sl], k_ref[:, sl]) * scale
        m = jnp.max(s, axis=-1, keepdims=True)
        p = jnp.exp(s - m)
        l = jnp.sum(p, axis=-1, keepdims=True)
        o = _dot(p.astype(BF16), v_ref[:, sl]) / l
        o_ref[:, sl] = o.astype(o_ref.dtype)


def _ctx_attn_call(q, k, v, *, n_seq, seq_len):
    naw = q.shape[1]
    spec = pl.BlockSpec((seq_len, naw), lambda b: (b, 0))
    return pl.pallas_call(
        functools.partial(_ctx_attn_kernel, scale=HEAD_DIM ** -0.5),
        grid=(n_seq,),
        in_specs=[spec, spec, spec],
        out_specs=spec,
        out_shape=jax.ShapeDtypeStruct((n_seq * seq_len, naw), BF16),
        compiler_params=_cparams(1),
        name="ctx_attn",
    )(q, k, v)


def _na_bias_kernel(rpb_ref, o_ref, t_s):
    inv_scale = HEAD_DIM ** 0.5
    h = pl.program_id(0)
    n_dr = 2 * NA_WIN_R - 1
    n_dc = 2 * NA_WIN_C - 1
    w = GRID_W
    qc = lax.broadcasted_iota(jnp.int32, (w, w), 0)
    kc = lax.broadcasted_iota(jnp.int32, (w, w), 1)
    dc = kc - qc + (NA_WIN_C - 1)
    c0 = jnp.clip(qc - NA_WIN_C // 2, 0, w - NA_WIN_C)
    col_ok = (kc >= c0) & (kc < c0 + NA_WIN_C)
    for dr in range(n_dr):
        def pick(j, t, dr=dr):
            return jnp.where(dc == j, rpb_ref[h * (n_dr * n_dc) + dr * n_dc + j] * inv_scale, t)
        t = lax.fori_loop(0, n_dc, pick, jnp.zeros((w, w), F32))
        t_s[dr] = jnp.where(col_ok, t, NEG_INF)
    neg = jnp.full((w, w), NEG_INF, F32)
    patterns = [(0, lambda i: 0), (NA_WIN_R // 2, lambda i: i), (NA_WIN_R, lambda i: NA_QROWS)]
    for pat, (r_rel, r0_rel) in enumerate(patterns):
        for i in range(NA_QROWS):
            for kr in range(NA_KROWS):
                dr = kr - i - r_rel + (NA_WIN_R - 1)
                valid = r0_rel(i) <= kr < r0_rel(i) + NA_WIN_R
                o_ref[pat, i * w:(i + 1) * w, kr * w:(kr + 1) * w] = t_s[dr] if valid else neg


def _na_bias_call(rpb):
    n_h = rpb.shape[0]
    nq = NA_QROWS * GRID_W
    nk = NA_KROWS * GRID_W
    return pl.pallas_call(
        _na_bias_kernel,
        grid=(n_h,),
        in_specs=[pl.BlockSpec(memory_space=pltpu.SMEM)],
        out_specs=pl.BlockSpec((None, 3, nq, nk), lambda h: (h, 0, 0, 0)),
        out_shape=jax.ShapeDtypeStruct((n_h, 3, nq, nk), F32),
        scratch_shapes=[pltpu.VMEM((2 * NA_WIN_R - 1, GRID_W, GRID_W), F32)],
        compiler_params=_cparams(1),
        name="na_bias",
    )(rpb.reshape(-1))


def _na_kernel(q_ref, k_ref, v_ref, kc_ref, vc_ref, bb_ref, o_ref, kcb_s, vcb_s, s_s, p_s, l_s,
               *, rows, scale):
    w = GRID_W
    nq = NA_QROWS * w
    nk = NA_KROWS * w
    n_blk = rows // NA_QROWS
    kcb_s[...] = kc_ref[...].astype(BF16)
    vcb_s[...] = vc_ref[...].astype(BF16)

    def q_rows(blk):
        return pl.ds(pl.multiple_of(jnp.int32(blk) * nq, nq), nq)

    def k_rows(blk):
        k_base = jnp.clip(jnp.int32(blk) * NA_QROWS - NA_WIN_R // 2, 0, rows - NA_KROWS)
        return pl.ds(pl.multiple_of(k_base * w, nq), nk)

    def scores(blk, slot):
        pat = jnp.where(blk == 0, 0, jnp.where(blk == n_blk - 1, 2, 1))
        q = q_ref[q_rows(blk), :]
        s_s[slot, :, 0:nk] = _dot_nt(q, k_ref[k_rows(blk), :]) + bb_ref[pat]
        s_s[slot, :, nk:] = _dot_nt(q, kcb_s[...])

    def softmax(slot):
        s = s_s[slot]
        p = jnp.exp2((s - jnp.max(s, axis=-1, keepdims=True)) * (scale * math.log2(math.e)))
        l_s[slot] = jnp.sum(p, axis=-1, keepdims=True)
        p_s[slot] = p.astype(p_s.dtype)

    def values(blk, slot):
        o = _dot(p_s[slot, :, 0:nk], v_ref[k_rows(blk), :]) + _dot(p_s[slot, :, nk:], vcb_s[...])
        o_ref[q_rows(blk), :] = (o / l_s[slot]).astype(o_ref.dtype)

    scores(0, 0)
    softmax(0)
    scores(1, 1)

    def body(jj, carry):
        values(2 * jj, 0)
        softmax(1)
        scores(2 * jj + 2, 0)
        values(2 * jj + 1, 1)
        softmax(0)
        scores(2 * jj + 3, 1)
        return carry

    lax.fori_loop(0, (n_blk - 2) // 2, body, 0)
    values(n_blk - 2, 0)
    softmax(1)
    values(n_blk - 1, 1)


def _na_call(q, k, v, cache_k, cache_v, bias, *, layer_e, n_b, seq_len, row_blk0):
    naw = q.shape[1]
    n_h = naw // HEAD_DIM
    past = cache_k.shape[2]
    rows = seq_len // GRID_W
    assert rows % (2 * NA_QROWS) == 0 and rows >= NA_KROWS
    nq, nk = NA_QROWS * GRID_W, NA_KROWS * GRID_W
    tok =pl.BlockSpec((seq_len, HEAD_DIM), lambda b, h: (row_blk0 + b, h))
    ctx = pl.BlockSpec((None, None, past, HEAD_DIM), lambda b, h: (b, layer_e, 0, h))
    return pl.pallas_call(
        functools.partial(_na_kernel, rows=rows, scale=HEAD_DIM ** -0.5),
        grid=(n_b, n_h),
        in_specs=[tok, tok, tok, ctx, ctx,
                  pl.BlockSpec((None,) + bias.shape[1:], lambda b, h: (h, 0, 0, 0))],
        out_specs=pl.BlockSpec((seq_len, HEAD_DIM), lambda b, h: (b, h)),
        out_shape=jax.ShapeDtypeStruct((n_b * seq_len, naw), BF16),
        scratch_shapes=[pltpu.VMEM((past, HEAD_DIM), BF16), pltpu.VMEM((past, HEAD_DIM), BF16),
                        pltpu.VMEM((2, nq, nk + past), F32), pltpu.VMEM((2, nq, nk + past), BF16),
                        pltpu.VMEM((2, nq, 1), F32)],
        compiler_params=_cparams(2),
        name="na_attn",
    )(q, k, v, cache_k, cache_v, bias)


def _about_kernel(*refs, n_x, n_p_tiles, nc, rc):
    g_ref, ap_ref, as_ref = refs[:3]
    x_refs = refs[3:3 + n_x]
    mod_ref, gw_ref, gb_ref, w_ref, o_ref = refs[3 + n_x:]
    s5w = g_ref.shape[1]
    tm, d = o_ref.shape

    def run(part):
        a_ref, x_ref = part
        for r0 in range(0, tm, rc):
            rows = slice(r0, r0 + rc)
            g = g_ref[rows, :]
            gl = _dot(g.astype(BF16), gw_ref[...]) + gb_ref[...]
            s5o = (g * jax.nn.sigmoid(gl)).astype(BF16)
            a = a_ref[rows, :]
            for n0 in range(0, d, nc):
                y = _dot(s5o, w_ref[0:s5w, n0:n0 + nc]) + _dot(a, w_ref[s5w:, n0:n0 + nc])
                o_ref[rows, n0:n0 + nc] = x_ref[rows, n0:n0 + nc] + mod_ref[2:3, n0:n0 + nc] * y

    _for_part(pl.program_id(0), n_p_tiles, [(ap_ref, x_refs[0]), (as_ref, x_refs[-1])], run)


def _about_call(g, attn_p, attn_s, xs, modt, glu_w, glu_b, w_out, *, layer, tm, n_p_tiles):
    m = sum(x.shape[0] for x in xs)
    d = xs[0].shape[1]
    s5w = g.shape[1]
    naw = attn_p.shape[1]
    nc = min(512, d)
    row = lambda i: (i, 0)
    return pl.pallas_call(
        functools.partial(_about_kernel, n_x=len(xs), n_p_tiles=n_p_tiles, nc=nc, rc=min(ROW_CHUNK, tm)),
        grid=(m // tm,),
        in_specs=[pl.BlockSpec((tm, s5w), row)] + _x_specs(2, tm, naw, n_p_tiles)
        + _x_specs(len(xs), tm, d, n_p_tiles) + [
            pl.BlockSpec((None, 8, d), lambda i: (i, 0, 0)),
            _const_spec((None, s5w, s5w), lambda i: (layer, 0, 0)),
            _const_spec((1, s5w), lambda i: (0, 0)),
            _const_spec((None, s5w + naw, d), lambda i: (layer, 0, 0)),
        ],
        out_specs=pl.BlockSpec((tm, d), row),
        out_shape=jax.ShapeDtypeStruct((m, d), F32),
        compiler_params=_cparams(1),
        name="ab_out_proj",
    )(g, attn_p, attn_s, *xs, modt, glu_w, glu_b, w_out)


def _mlp_kernel(x_ref, mod_ref, g_ref, w1_ref, w2_ref, o_ref, h_s, a_s, *, rc, nc1, nc2):
    tm, d = o_ref.shape
    tf = w1_ref.shape[1]
    ta = a_s.shape[1]

    def chunk_dots(rows):
        h = h_s[rows, :]
        for f0 in range(0, tf, ta):
            for c0 in range(0, ta, nc1):
                a = jnp.maximum(_dot(h, w1_ref[:, f0 + c0:f0 + c0 + nc1]), 0.0)
                a_s[rows, c0:c0 + nc1] = (a * a).astype(a_s.dtype)
            a = a_s[rows, :]
            for n0 in range(0, d, nc2):
                o_ref[rows, n0:n0 + nc2] += mod_ref[5:6, n0:n0 + nc2] * _dot(a, w2_ref[f0:f0 + ta, n0:n0 + nc2])

    @pl.when(pl.program_id(1) == 0)
    def _():
        g, shift, scale = g_ref[...], mod_ref[3:4, :], mod_ref[4:5, :]
        for r0 in range(0, tm, rc):
            rows = slice(r0, r0 + rc)
            x = x_ref[rows, :]
            h_s[rows, :] = _modulate(x, g, shift, scale).astype(h_s.dtype)
            o_ref[rows, :] = x
            chunk_dots(rows)

    @pl.when(pl.program_id(1) != 0)
    def _():
        chunk_dots(slice(0, tm))


def _mlp_call(x, modt, g, w1, w2, *, layer, tm, tf, tile0=0, n_tiles=None):
    d = x.shape[1]
    n_tiles = x.shape[0] // tm if n_tiles is None else n_tiles
    m = n_tiles * tm
    d_ff = w1.shape[2]
    rc = min(ROW_CHUNK, tm)
    return pl.pallas_call(
        functools.partial(_mlp_kernel, rc=rc, nc1=min(256, tf), nc2=min(512, d)),
        grid=(n_tiles, d_ff // tf),
        in_specs=[
            pl.BlockSpec((tm, d), lambda i, f: (i + tile0, 0)),
            pl.BlockSpec((None, 8, d), lambda i, f: (i + tile0, 0, 0)),
            _const_spec((1, d), lambda i, f: (0, 0)),
            pl.BlockSpec((None, d, tf), lambda i, f: (layer, 0, f)),
            pl.BlockSpec((None, tf, d), lambda i, f: (layer, f, 0)),
        ],
        out_specs=pl.BlockSpec((tm, d), lambda i, f: (i, 0)),
        out_shape=jax.ShapeDtypeStruct((m, d), F32),
        scratch_shapes=[pltpu.VMEM((tm, d), BF16), pltpu.VMEM((tm, min(512, tf)), BF16)],
        compiler_params=_cparams(2),
        name="mlp",
    )(x, modt, g, w1, w2)


def _convin_kernel(x_ref, mod_ref, g_ref, w_ref, gb_ref, z_ref, h_s, *, rc, nc):
    tm, d = gb_ref.shape
    g, shift, scale = g_ref[...], mod_ref[0:1, :], mod_ref[1:2, :]
    for r0 in range(0, tm, rc):
        rows = slice(r0, r0 + rc)
        h_s[rows, :] = _modulate(x_ref[rows, :], g, shift, scale).astype(h_s.dtype)
        h = h_s[rows, :]
        for n0 in range(0, d, nc):
            gb_ref[rows, n0:n0 + nc] = _dot(h, w_ref[:, n0:n0 + nc]).astype(gb_ref.dtype)
            z_ref[rows, n0:n0 + nc] = (_dot(h, w_ref[:, d + n0:d + n0 + nc])
                                       * _dot(h, w_ref[:, 2 * d + n0:2 * d + n0 + nc])).astype(z_ref.dtype)


def _convin_call(x, modt, g, w_in, *, layer, tm):
    m, d = x.shape
    row = lambda i: (i, 0)
    return pl.pallas_call(
        functools.partial(_convin_kernel, rc=min(ROW_CHUNK, tm), nc=min(512, d)),
        grid=(m // tm,),
        in_specs=[
            pl.BlockSpec((tm, d), row),
            pl.BlockSpec((None, 8, d), lambda i: (i, 0, 0)),
            _const_spec((1, d), lambda i: (0, 0)),
            _const_spec((None, d, 3 * d), lambda i: (layer, 0, 0)),
        ],
        out_specs=[pl.BlockSpec((tm, d), row), pl.BlockSpec((tm, d), row)],
        out_shape=[jax.ShapeDtypeStruct((m, d), BF16), jax.ShapeDtypeStruct((m, d), BF16)],
        scratch_shapes=[pltpu.VMEM((tm, d), BF16)],
        compiler_params=_cparams(1),
        name="conv_in_proj",
    )(x, modt, g, w_in)


def _convout_kernel(gb_ref, z_ref, zp_ref, zn_ref, x_ref, mod_ref, cw_ref, cb_ref, w_ref, o_ref, t_s,
                    *, n_p_tiles, lp, ls, rc, cc, nc):
    i = pl.program_id(0)
    tm, d = z_ref.shape
    halo = zp_ref.shape[0]
    seq_mask = jnp.where(i < n_p_tiles, lp - 1, ls - 1)
    ridx = lax.broadcasted_iota(jnp.int32, (rc, 1), 0)
    chunk_aligned = lp % rc == 0 and ls % rc == 0
    for r0 in range(0, tm, rc):
        rows = slice(r0, r0 + rc)
        if chunk_aligned:
            starts = ((i * tm + r0) & seq_mask) == 0
            ends = ((i * tm + r0 + rc) & seq_mask) == 0
        else:
            pos = (i * tm + r0 + ridx) & seq_mask
            is_start = pos == 0
            is_end = pos == seq_mask
        for c0 in range(0, d, cc):
            cs = slice(c0, c0 + cc)
            z = z_ref[rows, cs].astype(F32)
            before = (zp_ref[halo - 1:halo, cs] if r0 == 0 else z_ref[r0 - 1:r0, cs]).astype(F32)
            after = (zn_ref[0:1, cs] if r0 + rc == tm else z_ref[r0 + rc:r0 + rc + 1, cs]).astype(F32)
            if chunk_aligned:
                before = jnp.where(starts, 0.0, before)
                after = jnp.where(ends, 0.0, after)
            z_prev = jnp.where(ridx == 0, before, pltpu.roll(z, 1, 0))
            z_next = jnp.where(ridx == rc - 1, after, pltpu.roll(z, rc - 1, 0))
            if not chunk_aligned:
                z_prev = jnp.where(is_start, 0.0, z_prev)
                z_next = jnp.where(is_end, 0.0, z_next)
            conv = z_prev * cw_ref[0:1, cs] + z * cw_ref[1:2, cs] + z_next * cw_ref[2:3, cs] + cb_ref[:, cs]
            t_s[rows, cs] = (gb_ref[rows, cs].astype(F32) * conv).astype(t_s.dtype)
        t = t_s[rows, :]
        for n0 in range(0, d, nc):
            o_ref[rows, n0:n0 + nc] = (x_ref[rows, n0:n0 + nc]
                                       + mod_ref[2:3, n0:n0 + nc] * _dot(t, w_ref[:, n0:n0 + nc]))


def _convout_call(gb, z, x, modt, conv_w, conv_b, w_out, *, layer, tm, n_p_tiles, lp, ls):
    m, d = x.shape
    halo = 16
    assert tm % halo == 0 and lp & (lp - 1) == 0 and ls & (ls - 1) == 0
    assert (n_p_tiles * tm) % ls == 0 or True
    hb = tm // halo
    last = m // halo - 1
    row = lambda i: (i, 0)
    return pl.pallas_call(
        functools.partial(_convout_kernel, n_p_tiles=n_p_tiles, lp=lp, ls=ls, rc=min(ROW_CHUNK, tm),
                          cc=min(256, d), nc=min(512, d)),
        grid=(m // tm,),
        in_specs=[
            pl.BlockSpec((tm, d), row),
            pl.BlockSpec((tm, d), row),
            pl.BlockSpec((halo, d), lambda i: (jnp.maximum(i * hb - 1, 0), 0)),
            pl.BlockSpec((halo, d), lambda i: (jnp.minimum((i + 1) * hb, last), 0)),
            pl.BlockSpec((tm, d), row),
            pl.BlockSpec((None, 8, d), lambda i: (i, 0, 0)),
            _const_spec((8, d), lambda i: (0, 0)),
            _const_spec((1, d), lambda i: (0, 0)),
            _const_spec((None, d, d), lambda i: (layer, 0, 0)),
        ],
        out_specs=pl.BlockSpec((tm, d), row),
        out_shape=jax.ShapeDtypeStruct((m, d), F32),
        scratch_shapes=[pltpu.VMEM((tm, d), BF16)],
        compiler_params=_cparams(1),
        name="conv_out_proj",
    )(gb, z, z, z, x, modt, conv_w, conv_b, w_out)


def _tile_rows(m_p, m_s, l_s, n_b, tm):
    assert m_p % tm == 0 and l_s % tm == 0
    return np.concatenate([np.full(m_p // tm, n_b), np.repeat(np.arange(n_b), l_s // tm)])


def _pick_tile(pref, m_p, l_s):
    tm = pref
    while m_p % tm or l_s % tm:
        tm //= 2
    return tm


def kernel(x_prompt, x_sample, c, cache_k, cache_v, state_ssm_re, state_ssm_im, c_ctx, ada_w, ada_b, norm1_g, norm2_g, ab_w_in, ab_w_out, s5_lam_re, s5_lam_im, s5_log_dt, s5_b_re, s5_b_im, s5_c_re, s5_c_im, s5_d, s5_glu_w, s5_glu_b, q_norm_g, k_norm_g, na_rpb, conv_w_in, conv_w, conv_b, conv_w_out, mlp_w1, mlp_w2):
    n_bp, l_p, d = x_prompt.shape
    n_bs, l_s, _ = x_sample.shape
    depth = ada_w.shape[0]
    m_p, m_s = n_bp * l_p, n_bs * l_s
    m = m_p + m_s
    n_g, n_state = s5_lam_re.shape[2], s5_lam_re.shape[3]
    s5w = n_g * S5_GROUP_CH
    naw = (ab_w_in.shape[2] - s5w) // 3
    n_h = naw // HEAD_DIM
    past = cache_k.shape[2]
    d_ff = mlp_w1.shape[2]
    assert m_p % l_s == 0, "latent sequences must start on a sequence-length row block"
    assert l_p % S5_CHUNK == 0 and l_s % S5_CHUNK == 0
    c_p, c_s = l_p // S5_CHUNK, l_s // S5_CHUNK
    assert c_p & (c_p - 1) == 0 and c_s & (c_s - 1) == 0
    r_p, r_s = m_p // S5_CHUNK, m_s // S5_CHUNK
    width = S5_CHUNK * S5_GROUP_CH

    xs = [x_prompt.reshape(m_p, d), x_sample.reshape(m_s, d)]

    n_rows = -(-(n_bs + 1) // 16) * 16
    cvec = jnp.concatenate([c, c_ctx[None], jnp.zeros((n_rows - n_bs - 1, d), F32)], axis=0)
    mod = _ada_call(cvec, ada_w, ada_b)

    def mod_tiles(layer, tm):
        t = mod[layer][_tile_rows(m_p, m_s, l_s, n_bs, tm)].reshape(m // tm, 6, d)
        return jnp.pad(t, ((0, 0), (0, 2), (0, 0)))

    tm_big = _pick_tile(1024, m_p, l_s)
    tm_mid = _pick_tile(512, m_p, l_s)
    tf = min(1024, d_ff)

    n_pw = max(int(math.log2(c_s)), 1)
    w_u, w_y, pw = _s5prep_call(s5_lam_re, s5_lam_im, s5_log_dt, s5_b_re, s5_b_im, s5_c_re, s5_c_im,
                                s5_d, n_pw)
    n_h0 = -(-n_bs // 8) * 8
    h0_all = jnp.stack([state_ssm_re, state_ssm_im], axis=2)
    h0_all = h0_all.transpose(1, 4, 0, 2, 3, 5).reshape(-1, n_g, n_bs, 4 * n_state)
    h0_all = jnp.pad(h0_all, ((0, 0), (0, 0), (0, n_h0 - n_bs), (0, 0)))
    cache_k4 = cache_k.reshape(n_bs, -1, past, naw)
    cache_v4 = cache_v.reshape(n_bs, -1, past, naw)
    ab_w_in_b, ab_w_out_b, glu_w_b = ab_w_in.astype(BF16), ab_w_out.astype(BF16), s5_glu_w.astype(BF16)
    conv_w_in_b, conv_w_out_b = conv_w_in.astype(BF16), conv_w_out.astype(BF16)
    mlp_w1_b, mlp_w2_b = mlp_w1.astype(BF16), mlp_w2.astype(BF16)

    caches = None
    new_re, new_im = [], []
    for layer in range(depth):
        g1 = norm1_g[layer].reshape(1, d)
        g2 = norm2_g[layer].reshape(1, d)
        if layer % 2 == 0:
            e = layer // 2
            u, q, k, v, *caches = _inproj_call(
                xs, mod_tiles(layer, tm_mid), g1, ab_w_in_b,
                q_norm_g[e].reshape(1, HEAD_DIM), k_norm_g[e].reshape(1, HEAD_DIM), caches,
                layer=e, n_e=ab_w_in.shape[0], l_p=l_p, tm=tm_mid, s5w=s5w, naw=naw, n_p_tiles=m_p // tm_mid)
            g_tok, z_all = _s5_call(u, w_u[e], w_y[e], pw[e], h0_all[e],
                                    l_tile=l_s, n_p_tiles=m_p // l_s, cp=c_p, cs=c_s)
            z_p = z_all.reshape(n_g, n_bp, c_p, 4 * n_state)
            fin = z_p[:, :, c_p - 1].reshape(n_g, n_bp, 2, 2, n_state)
            new_re.append(fin[:, :, 0].transpose(1, 2, 0, 3))
            new_im.append(fin[:, :, 1].transpose(1, 2, 0, 3))
            attn_p = _ctx_attn_call(q, k, v, n_seq=n_bp, seq_len=l_p)
            bias = _na_bias_call(na_rpb[e])
            attn_s = _na_call(q, k, v, cache_k4, cache_v4, bias, layer_e=e, n_b=n_bs, seq_len=l_s,
                              row_blk0=m_p // l_s)
            x = _about_call(g_tok, attn_p, attn_s, xs, mod_tiles(layer, tm_mid), glu_w_b,
                            s5_glu_b[e].reshape(1, s5w), ab_w_out_b, layer=e, tm=tm_mid,
                            n_p_tiles=m_p // tm_mid)
        else:
            x = xs[0]
            o = layer // 2
            gb, z = _convin_call(x, mod_tiles(layer, tm_mid), g1, conv_w_in_b, layer=o, tm=tm_mid)
            cw8 = jnp.pad(conv_w[o], ((0, 5), (0, 0)))
            x = _convout_call(gb, z, x, mod_tiles(layer, tm_mid), cw8, conv_b[o].reshape(1, d),
                              conv_w_out_b, layer=o, tm=tm_mid, n_p_tiles=m_p // tm_mid,
                              lp=l_p, ls=l_s)
        mlp = functools.partial(_mlp_call, x, mod_tiles(layer, tm_big), g2, mlp_w1_b, mlp_w2_b,
                                layer=layer, tm=tm_big, tf=tf)
        if layer < depth - 1:
            xs = [mlp()]
        else:
            y_prompt = mlp(tile0=0, n_tiles=m_p // tm_big).reshape(n_bp, l_p, d)
            y_sample = mlp(tile0=m_p // tm_big, n_tiles=m_s // tm_big).reshape(n_bs, l_s, d)

    new_k, new_v = (t.reshape(n_bp, -1, l_p, n_h, HEAD_DIM) for t in caches)
    return (y_prompt, y_sample, new_k, new_v, jnp.stack(new_re, axis=1), jnp.stack(new_im, axis=1))
```

```python
import functools
import math

import numpy as np
import jax
import jax.numpy as jnp
from jax import lax
from jax.experimental import pallas as pl
from jax.experimental.pallas import tpu as pltpu

F32 = jnp.float32
BF16 = jnp.bfloat16

NORM_EPS = 1e-6
NEG_INF = -1e30

S5_GROUP_CH = 16
HEAD_DIM = 128
GRID_W = 64
NA_WIN_R = 8
NA_WIN_C = 16
S5_CHUNK = 16
NA_QROWS = 4
NA_KROWS = NA_QROWS + NA_WIN_R
ROW_CHUNK = 256

VMEM_LIMIT_BYTES = 62 * 1024 * 1024


def _cparams(n_axes):
    return pltpu.CompilerParams(dimension_semantics=("arbitrary",) * n_axes,
                                vmem_limit_bytes=VMEM_LIMIT_BYTES)


def _const_spec(shape, index_map):
    return pl.BlockSpec(shape, index_map, pipeline_mode=pl.Buffered(1))


def _dot(a, b):
    return jnp.dot(a, b, preferred_element_type=F32)


def _dot_nt(a, b):
    return lax.dot_general(a, b, (((1,), (1,)), ((), ())), preferred_element_type=F32)


def _dot_split(a, b):
    a_hi = a.astype(BF16)
    a_lo = (a - a_hi.astype(F32)).astype(BF16)
    b_hi = b.astype(BF16)
    b_lo = (b - b_hi.astype(F32)).astype(BF16)
    return _dot(a_hi, b_hi) + _dot(a_hi, b_lo) + _dot(a_lo, b_hi)


def _modulate(x, g, shift, scale):
    ms = jnp.mean(x * x, axis=-1, keepdims=True)
    y = x * lax.rsqrt(ms + NORM_EPS) * g
    return y * (1.0 + scale) + shift


def _ada_kernel(c_ref, w_ref, b_ref, o_ref, *, nc):
    cv = c_ref[...]
    sc = (cv * jax.nn.sigmoid(cv)).astype(BF16)
    tn = w_ref.shape[1]
    for n0 in range(0, tn, nc):
        w = w_ref[:, n0:n0 + nc].astype(BF16)
        o_ref[:, n0:n0 + nc] = _dot(sc, w) + b_ref[:, n0:n0 + nc]


def _ada_call(cvec, ada_w, ada_b):
    depth, d, n6 = ada_w.shape
    rows = cvec.shape[0]
    tn = 1536 if n6 % 1536 == 0 else n6
    nc = 512 if tn % 512 == 0 else tn
    return pl.pallas_call(
        functools.partial(_ada_kernel, nc=nc),
        grid=(depth, n6 // tn),
        in_specs=[
            pl.BlockSpec((rows, d), lambda l, j: (0, 0)),
            pl.BlockSpec((None, d, tn), lambda l, j: (l, 0, j)),
            pl.BlockSpec((None, 1, tn), lambda l, j: (l, 0, j)),
        ],
        out_specs=pl.BlockSpec((None, rows, tn), lambda l, j: (l, 0, j)),
        out_shape=jax.ShapeDtypeStruct((depth, rows, n6), F32),
        compiler_params=_cparams(2),
        name="ada_params",
    )(cvec, ada_w, ada_b.reshape(depth, 1, n6))


def _x_specs(n_x, tm, d, n_p_tiles):
    if n_x == 1:
        return [pl.BlockSpec((tm, d), lambda i: (i, 0))]
    return [pl.BlockSpec((tm, d), lambda i: (jnp.minimum(i, n_p_tiles - 1), 0)),
            pl.BlockSpec((tm, d), lambda i: (jnp.maximum(i - n_p_tiles, 0), 0))]


def _for_part(i, n_p_tiles, refs, fn):
    if len(refs) == 1:
        fn(refs[0])
    else:
        pl.when(i < n_p_tiles)(lambda: fn(refs[0]))
        pl.when(i >= n_p_tiles)(lambda: fn(refs[1]))


def _inproj_kernel(*refs, n_x, n_alias, n_p_tiles, l_p, s5w, naw, rc):
    x_refs = refs[:n_x]
    mod_ref, g_ref, w_ref, qg_ref, kg_ref = refs[n_x:n_x + 5]
    u_ref, q_ref, k_ref, v_ref, ck_ref, cv_ref, h_s = refs[n_x + 5 + n_alias:]
    i = pl.program_id(0)
    tm = u_ref.shape[0]
    n_heads = naw // HEAD_DIM

    def cache_store(c_ref, r0, col0, val):
        head = col0 // HEAD_DIM
        piece = min(rc, l_p)
        for t0 in range(r0, r0 + rc, piece):
            c_ref[t0 // l_p, pl.ds((t0 % l_p) * n_heads + head, piece, stride=n_heads), :] = (
                val[t0 - r0:t0 - r0 + piece])
    hd = HEAD_DIM
    cw = min(4 * hd, naw)
    g, shift, scale = g_ref[...], mod_ref[0:1, :], mod_ref[1:2, :]
    qg = qg_ref[...]
    kg = kg_ref[...]

    def head_norm(t, gain):
        return t * lax.rsqrt(jnp.mean(t * t, axis=-1, keepdims=True) + NORM_EPS) * gain

    def maybe_cache(prompt, fn):
        if prompt:
            fn()

    def run(x_ref, prompt):
        for r0 in range(0, tm, rc):
            rows = slice(r0, r0 + rc)
            h_s[rows, :] = _modulate(x_ref[rows, :], g, shift, scale).astype(h_s.dtype)
            h = h_s[rows, :]
            u_ref[rows, :] = _dot(h, w_ref[:, 0:s5w]).astype(u_ref.dtype)
            for c0 in range(0, naw, cw):
                qc = _dot(h, w_ref[:, s5w + c0:s5w + c0 + cw])
                for j in range(cw // hd):
                    q_ref[rows, c0 + j * hd:c0 + (j + 1) * hd] = head_norm(
                        qc[:, j * hd:(j + 1) * hd], qg).astype(q_ref.dtype)
                kc = _dot(h, w_ref[:, s5w + naw + c0:s5w + naw + c0 + cw])
                for j in range(cw // hd):
                    kh = head_norm(kc[:, j * hd:(j + 1) * hd], kg)
                    k_ref[rows, c0 + j * hd:c0 + (j + 1) * hd] = kh.astype(k_ref.dtype)
                    maybe_cache(prompt, functools.partial(cache_store, ck_ref, r0, c0 + j * hd, kh))
                vc = _dot(h, w_ref[:, s5w + 2 * naw + c0:s5w + 2 * naw + c0 + cw])
                v_ref[rows, c0:c0 + cw] = vc.astype(v_ref.dtype)
                for j in range(cw // hd):
                    maybe_cache(prompt, functools.partial(cache_store, cv_ref, r0, c0 + j * hd,
                                                          vc[:, j * hd:(j + 1) * hd]))

    pl.when(i < n_p_tiles)(lambda: run(x_refs[0], True))
    pl.when(i >= n_p_tiles)(lambda: run(x_refs[-1], False))


def _inproj_call(xs, modt, g, w_in, qg, kg, caches, *, layer, n_e, l_p, tm, s5w, naw, n_p_tiles):
    m = sum(x.shape[0] for x in xs)
    d = xs[0].shape[1]
    n_in = w_in.shape[2]
    rc = min(ROW_CHUNK, tm)
    n_heads = naw // HEAD_DIM
    assert tm % l_p == 0 and (rc % l_p == 0 or l_p % rc == 0)
    seqs = tm // l_p
    cache_shape = jax.ShapeDtypeStruct((n_p_tiles * seqs, n_e, l_p * n_heads, HEAD_DIM), F32)
    cache_spec = pl.BlockSpec((seqs, None, l_p * n_heads, HEAD_DIM),
                              lambda i: (jnp.minimum(i, n_p_tiles - 1), layer, 0, 0))
    n_alias = 0 if caches is None else 2
    alias_args = [] if caches is None else list(caches)
    n_in_args = len(xs) + 5
    row = lambda i: (i, 0)
    return pl.pallas_call(
        functools.partial(_inproj_kernel, n_x=len(xs), n_alias=n_alias, n_p_tiles=n_p_tiles, l_p=l_p,
                          s5w=s5w, naw=naw, rc=rc),
        grid=(m // tm,),
        in_specs=_x_specs(len(xs), tm, d, n_p_tiles) + [
            pl.BlockSpec((None, 8, d), lambda i: (i, 0, 0)),
            _const_spec((1, d), lambda i: (0, 0)),
            _const_spec((None, d, n_in), lambda i: (layer, 0, 0)),
            _const_spec((1, HEAD_DIM), lambda i: (0, 0)),
            _const_spec((1, HEAD_DIM), lambda i: (0, 0)),
        ] + [pl.BlockSpec(memory_space=pl.ANY)] * n_alias,
        out_specs=[
            pl.BlockSpec((tm, s5w), row),
            pl.BlockSpec((tm, naw), row),
            pl.BlockSpec((tm, naw), row),
            pl.BlockSpec((tm, naw), row),
            cache_spec, cache_spec,
        ],
        out_shape=[
            jax.ShapeDtypeStruct((m, s5w), F32),
            jax.ShapeDtypeStruct((m, naw), BF16),
            jax.ShapeDtypeStruct((m, naw), BF16),
            jax.ShapeDtypeStruct((m, naw), BF16),
            cache_shape, cache_shape,
        ],
        input_output_aliases={n_in_args + a: 4 + a for a in range(n_alias)},
        scratch_shapes=[pltpu.VMEM((tm, d), BF16)],
        compiler_params=_cparams(1),
        name="ab_in_proj",
    )(*xs, modt, g, w_in, qg, kg, *alias_args)


def _s5prep_kernel(row_ref, row2_ref, col_ref, btr_ref, bti_ref, cer_ref, cei_ref, d_ref,
                   wu_ref, wy_ref, pw_ref, *, n_state, n_pw):
    t_chunk = S5_CHUNK
    width = t_chunk * S5_GROUP_CH
    p = n_state
    lg_ch = int(math.log2(S5_GROUP_CH))
    r_i = lax.broadcasted_iota(jnp.int32, (width, 1), 0)
    c_i = lax.broadcasted_iota(jnp.int32, (1, width), 1)
    s_i = r_i >> lg_ch
    t_i = c_i >> lg_ch
    s16 = lax.broadcasted_iota(jnp.int32, (t_chunk, 1), 0)
    n_pow = -(-(t_chunk + 1) // 8) * 8

    def cmul(ar, ai, br, bi):
        return ar * br - ai * bi, ar * bi + ai * br

    acc = jnp.zeros((width, width), F32)
    for d in range(2):
        lr = row_ref[d, 0:1, 0:p]
        li = row_ref[d, 1:2, 0:p]
        dt = jnp.exp(row_ref[d, 2:3, 0:p])
        ar = lr * dt
        ai = li * dt
        er = jnp.exp(ar)
        abr = er * jnp.cos(ai)
        abi = er * jnp.sin(ai)
        den = lr * lr + li * li
        nr = abr - 1.0
        f_re = (nr * lr + abi * li) / den
        f_im = (abi * lr - nr * li) / den
        bb_re, bb_im = cmul(f_re, f_im, btr_ref[d], bti_ref[d])

        lrc = col_ref[d, :, 0:1]
        lic = col_ref[d, :, 1:2]
        dtc = jnp.exp(col_ref[d, :, 2:3])
        arc = lrc * dtc
        aic = lic * dtc
        ce_re = cer_ref[d]
        ce_im = cei_ref[d]

        k_r = lax.broadcasted_iota(jnp.int32, (n_pow, 1), 0).astype(F32)
        k_c = lax.broadcasted_iota(jnp.int32, (1, 128), 1).astype(F32)
        mag_r = jnp.exp(k_r * ar)
        tr_re, tr_im = mag_r * jnp.cos(k_r * ai), mag_r * jnp.sin(k_r * ai)
        mag_c = jnp.exp(arc * k_c)
        tc_re, tc_im = mag_c * jnp.cos(aic * k_c), mag_c * jnp.sin(aic * k_c)

        def pow_row(e, e_max):
            re = jnp.zeros((t_chunk, p), F32)
            im = jnp.zeros((t_chunk, p), F32)
            for kk in range(e_max + 1):
                hit = e == kk
                re = jnp.where(hit, tr_re[kk:kk + 1, :], re)
                im = jnp.where(hit, tr_im[kk:kk + 1, :], im)
            rep = lambda t: jnp.broadcast_to(t[:, None, :], (t_chunk, S5_GROUP_CH, p)).reshape(width, p)
            return rep(re), rep(im)

        def pow_col(e, e_max):
            re = jnp.zeros((p, width), F32)
            im = jnp.zeros((p, width), F32)
            for kk in range(e_max + 1):
                hit = e == kk
                re = jnp.where(hit, tc_re[:, kk:kk + 1], re)
                im = jnp.where(hit, tc_im[:, kk:kk + 1], im)
            return re, im

        terms = [(jnp.zeros_like(s16), jnp.zeros_like(t_i), s_i == t_i, 0)]
        blk = 2
        while blk <= t_chunk:
            lb = int(math.log2(blk))
            half = blk // 2
            mid = lambda idx: ((idx >> lb) << lb) + half
            same = (s_i >> lb) == (t_i >> lb)
            if d == 0:
                e_s, ok_s = mid(s16) - s16, s_i < mid(s_i)
                e_t, ok_t = t_i - mid(t_i), t_i >= mid(t_i)
            else:
                e_s, ok_s = s16 - mid(s16), s_i >= mid(s_i)
                e_t, ok_t = mid(t_i) - t_i, t_i < mid(t_i)
            terms.append((jnp.maximum(e_s, 0), jnp.maximum(e_t, 0), same & ok_s & ok_t, half))
            blk *= 2
        for e_s, e_t, mask, e_max in terms:
            l_re, l_im = cmul(*pow_row(e_s, e_max), bb_re, bb_im)
            r_re, r_im = cmul(*pow_col(e_t, e_max), ce_re, ce_im)
            term = _dot_split(l_re, r_re) - _dot_split(l_im, r_im)
            acc = acc + jnp.where(mask, term, 0.0)

        if d == 0:
            ws_re, ws_im = cmul(*pow_row(t_chunk - 1 - s16, t_chunk), bb_re, bb_im)
            ca_re, ca_im = cmul(*pow_col(t_i + 1, t_chunk), ce_re, ce_im)
        else:
            ws_re, ws_im = cmul(*pow_row(s16, t_chunk), bb_re, bb_im)
            ca_re, ca_im = cmul(*pow_col(t_chunk - t_i, t_chunk), ce_re, ce_im)
        wu_ref[:, width + d * p:width + (d + 1) * p] = ws_re.astype(wu_ref.dtype)
        wu_ref[:, width + (2 + d) * p:width + (3 + d) * p] = ws_im.astype(wu_ref.dtype)
        wy_ref[d * p:(d + 1) * p, :] = ca_re.astype(wy_ref.dtype)
        wy_ref[(2 + d) * p:(3 + d) * p, :] = (-ca_im).astype(wy_ref.dtype)

    lr2 = row2_ref[0:1, :]
    li2 = row2_ref[1:2, :]
    dt2 = jnp.exp(row2_ref[2:3, :])
    k16 = float(t_chunk)
    mag = jnp.exp(k16 * (lr2 * dt2))
    pr = mag * jnp.cos(k16 * (li2 * dt2))
    pi = mag * jnp.sin(k16 * (li2 * dt2))
    for i in range(n_pw):
        pw_ref[2 * i:2 * i + 1, :] = pr
        pw_ref[2 * i + 1:2 * i + 2, :] = pi
        pr, pi = pr * pr - pi * pi, 2.0 * pr * pi

    wu_ref[:, 0:width] = (acc + jnp.where(r_i == c_i, d_ref[...], 0.0)).astype(wu_ref.dtype)


def _s5prep_call(lam_re, lam_im, log_dt, b_re, b_im, c_re, c_im, d_skip, n_pw):
    n_e, _, n_g, p = lam_re.shape
    n_ch = S5_GROUP_CH
    width = S5_CHUNK * n_ch
    assert 2 * p == 128, "state rows are packed as [fwd | bwd] in one 128-lane tile"
    n_pw_rows = -(-2 * n_pw // 8) * 8
    dtb = jnp.broadcast_to(log_dt[..., None], lam_re.shape)
    zeros = jnp.zeros_like(lam_re)
    rowp = jnp.stack([lam_re, lam_im, dtb] + [zeros] * 5, axis=-2)
    row2 = jnp.concatenate([rowp[:, 0], rowp[:, 1]], axis=-1)
    colp = jnp.stack([lam_re, lam_im, dtb] + [zeros] * 5, axis=-1)
    bt = lambda b: jnp.tile(jnp.swapaxes(b, -1, -2), (1, 1, 1, S5_CHUNK, 1))
    ce = lambda c: jnp.tile(jnp.swapaxes(c, -1, -2), (1, 1, 1, 1, S5_CHUNK))
    d_row = jnp.tile(d_skip.reshape(n_e, n_g, 1, n_ch), (1, 1, 1, S5_CHUNK))

    def dspec(shape):
        return pl.BlockSpec((None, 2, None) + shape, lambda e, g: (e, 0, g, 0, 0))

    def ospec(shape):
        return pl.BlockSpec((None, None) + shape, lambda e, g: (e, g, 0, 0))

    return pl.pallas_call(
        functools.partial(_s5prep_kernel, n_state=p, n_pw=n_pw),
        grid=(n_e, n_g),
        in_specs=[dspec((8, p)), ospec((8, 2 * p)), dspec((p, 8)), dspec((width, p)), dspec((width, p)),
                  dspec((p, width)), dspec((p, width)), ospec((1, width))],
        out_specs=[ospec((width, 2 * width)), ospec((width, width)), ospec((n_pw_rows, 2 * p))],
        out_shape=[
            jax.ShapeDtypeStruct((n_e, n_g, width, 2 * width), BF16),
            jax.ShapeDtypeStruct((n_e, n_g, width, width), BF16),
            jax.ShapeDtypeStruct((n_e, n_g, n_pw_rows, 2 * p), F32),
        ],
        compiler_params=_cparams(2),
        name="s5_prep",
    )(rowp, row2, colp, bt(b_re), bt(b_im), ce(c_re), ce(c_im), d_row)


def _gelu_tanh(y):
    return 0.5 * y * (1.0 + jnp.tanh(0.7978845608028654 * (y + 0.044715 * (y * y * y))))


def _block_transpose8(v, lane):
    for dist in (4, 2, 1):
        width = dist * S5_GROUP_CH
        low = (lane & width) == 0
        out = list(v)
        for a in range(8):
            if a & dist == 0:
                lo, hi = v[a], v[a + dist]
                out[a] = jnp.where(low, lo, pltpu.roll(hi, width, 1))
                out[a + dist] = jnp.where(low, pltpu.roll(lo, 128 - width, 1), hi)
        v = out
    return v


def _s5_kernel(u_ref, wu_ref, wy_ref, pw_ref, h0_ref, g_ref, z_ref, zs_s, ug_s, yg_s,
               *, n_p_tiles, cp, cs):
    tile = pl.program_id(1)
    t_chunk = S5_CHUNK
    n = u_ref.shape[0] // t_chunk
    ngb, width = wy_ref.shape[0], wy_ref.shape[1]
    half = width // 2
    lanes = u_ref.shape[1]

    is_fwd = lax.broadcasted_iota(jnp.int32, (1, half), 1) < half // 2
    is_fwd2 = (lax.broadcasted_iota(jnp.int32, (1, width), 1) & (half - 1)) < half // 2

    lane = lax.broadcasted_iota(jnp.int32, (1, lanes), 1)
    for s_hi in range(t_chunk // 8):
        w = _block_transpose8([u_ref[pl.ds(s_hi * 8 + a, n, stride=t_chunk), :] for a in range(8)], lane)
        for gi in range(ngb):
            ug_s[gi, :, s_hi * lanes:(s_hi + 1) * lanes] = w[gi].astype(ug_s.dtype)

    def reversal(cseq):
        r = lax.broadcasted_iota(jnp.int32, (n, n), 0)
        c = lax.broadcasted_iota(jnp.int32, (n, n), 1)
        lg = int(math.log2(cseq))
        hit = ((r >> lg) == (c >> lg)) & ((r & (cseq - 1)) + (c & (cseq - 1)) == cseq - 1)
        return jnp.where(hit, 1.0, 0.0).astype(BF16)

    def run(cseq, j):
        rev = reversal(cseq)
        for gi in range(ngb):
            ug = ug_s[gi]
            wu = wu_ref[gi]
            ys = _dot(ug, wu)
            ug_rev = _dot(rev, ug).astype(BF16)
            ys_rev = _dot(ug_rev, wu[:, width:])
            yg_s[gi] = ys[:, 0:width]
            zs_s[0, gi] = jnp.where(is_fwd, ys[:, width:width + half], ys_rev[:, 0:half])
            zs_s[1, gi] = jnp.where(is_fwd, ys[:, width + half:], ys_rev[:, half:])

        zr, zi = zs_s[0], zs_s[1]
        pos = lax.broadcasted_iota(jnp.int32, (1, n, 1), 1) & (cseq - 1)
        edge = pos == 0
        if j is not None:
            h0 = h0_ref[:, pl.ds(j, 1), :]
            h0r, h0i = h0[:, :, 0:half], h0[:, :, half:]
            pr, pi = pw_ref[:, 0:1, :], pw_ref[:, 1:2, :]
            zr = zr + jnp.where(edge, pr * h0r - pi * h0i, 0.0)
            zi = zi + jnp.where(edge, pr * h0i + pi * h0r, 0.0)
        k = 1
        i = 0
        while k < cseq:
            ar = pw_ref[:, 2 * i:2 * i + 1, :]
            ai = pw_ref[:, 2 * i + 1:2 * i + 2, :]
            sr, si = pltpu.roll(zr, k, 1), pltpu.roll(zi, k, 1)
            valid = pos >= k
            zr = zr + jnp.where(valid, ar * sr - ai * si, 0.0)
            zi = zi + jnp.where(valid, ar * si + ai * sr, 0.0)
            k *= 2
            i += 1
        if j is None:
            z_ref[...] = jnp.concatenate([zr, zi], axis=2)
        xr, xi = pltpu.roll(zr, 1, 1), pltpu.roll(zi, 1, 1)
        if j is None:
            zs_s[0], zs_s[1] = jnp.where(edge, 0.0, xr), jnp.where(edge, 0.0, xi)
        else:
            zs_s[0], zs_s[1] = jnp.where(edge, h0r, xr), jnp.where(edge, h0i, xi)

        for gi in range(ngb):
            x = jnp.concatenate([zs_s[0, gi], zs_s[1, gi]], axis=1)
            x_rev = _dot(rev, x.astype(BF16))
            x = jnp.where(is_fwd2, x, x_rev).astype(BF16)
            yg_s[gi] = _gelu_tanh(yg_s[gi] + _dot(x, wy_ref[gi]))

    pl.when(tile < n_p_tiles)(lambda: run(cp, None))
    pl.when(tile >= n_p_tiles)(lambda: run(cs, tile - n_p_tiles))

    for t_hi in range(t_chunk // 8):
        w = _block_transpose8([yg_s[gi, :, t_hi * lanes:(t_hi + 1) * lanes] for gi in range(ngb)], lane)
        for a in range(8):
            g_ref[pl.ds(t_hi * 8 + a, n, stride=t_chunk), :] = w[a]


def _s5_call(u, w_u, w_y, pw, h0, *, l_tile, n_p_tiles, cp, cs):
    m, s5w = u.shape
    n_g, width = w_y.shape[0], w_y.shape[1]
    ngb = min(8, n_g)
    lanes = ngb * S5_GROUP_CH
    assert lanes == 128 and ngb == 8 and S5_CHUNK % 8 == 0 and n_g % ngb == 0 and m % l_tile == 0
    n = l_tile // S5_CHUNK
    n_h0 = h0.shape[1]
    tok = pl.BlockSpec((l_tile, lanes), lambda cb, t: (t, cb))
    grp = lambda shape: pl.BlockSpec((ngb,) + shape, lambda cb, t: (cb, 0, 0))
    return pl.pallas_call(
        functools.partial(_s5_kernel, n_p_tiles=n_p_tiles, cp=cp, cs=cs),
        grid=(n_g // ngb, m // l_tile),
        in_specs=[tok, grp((width, 2 * width)), grp((width, width)), grp(pw.shape[1:]), grp((n_h0, width))],
        out_specs=[tok, pl.BlockSpec((ngb, n, width), lambda cb, t: (cb, jnp.minimum(t, n_p_tiles - 1), 0))],
        out_shape=[jax.ShapeDtypeStruct((m, s5w), F32),
                   jax.ShapeDtypeStruct((n_g, n_p_tiles * n, width), F32)],
        scratch_shapes=[pltpu.VMEM((2, ngb, n, width // 2), F32),
                        pltpu.VMEM((ngb, n, width), BF16),
                        pltpu.VMEM((ngb, n, width), F32)],
        compiler_params=_cparams(2),
        name="s5_core",
    )(u, w_u, w_y, pw, h0)


def _ctx_attn_kernel(q_ref, k_ref, v_ref, o_ref, *, scale):
    hd = HEAD_DIM
    for h in range(q_ref.shape[1] // hd):
        sl = slice(h * hd, (h + 1) * hd)
        s = _dot_nt(q_ref[:, sl], k_ref[:, sl]) * scale
        m = jnp.max(s, axis=-1, keepdims=True)
        p = jnp.exp(s - m)
        l = jnp.sum(p, axis=-1, keepdims=True)
        o = _dot(p.astype(BF16), v_ref[:, sl]) / l
        o_ref[:, sl] = o.astype(o_ref.dtype)


def _ctx_attn_call(q, k, v, *, n_seq, seq_len):
    naw = q.shape[1]
    spec = pl.BlockSpec((seq_len, naw), lambda b: (b, 0))
    return pl.pallas_call(
        functools.partial(_ctx_attn_kernel, scale=HEAD_DIM ** -0.5),
        grid=(n_seq,),
        in_specs=[spec, spec, spec],
        out_specs=spec,
        out_shape=jax.ShapeDtypeStruct((n_seq * seq_len, naw), BF16),
        compiler_params=_cparams(1),
        name="ctx_attn",
    )(q, k, v)


def _na_bias_kernel(rpb_ref, o_ref, t_s):
    inv_scale = HEAD_DIM ** 0.5
    h = pl.program_id(0)
    n_dr = 2 * NA_WIN_R - 1
    n_dc = 2 * NA_WIN_C - 1
    w = GRID_W
    qc = lax.broadcasted_iota(jnp.int32, (w, w), 0)
    kc = lax.broadcasted_iota(jnp.int32, (w, w), 1)
    dc = kc - qc + (NA_WIN_C - 1)
    c0 = jnp.clip(qc - NA_WIN_C // 2, 0, w - NA_WIN_C)
    col_ok = (kc >= c0) & (kc < c0 + NA_WIN_C)
    for dr in range(n_dr):
        def pick(j, t, dr=dr):
            return jnp.where(dc == j, rpb_ref[h * (n_dr * n_dc) + dr * n_dc + j] * inv_scale, t)
        t = lax.fori_loop(0, n_dc, pick, jnp.zeros((w, w), F32))
        t_s[dr] = jnp.where(col_ok, t, NEG_INF)
    neg = jnp.full((w, w), NEG_INF, F32)
    patterns = [(0, lambda i: 0), (NA_WIN_R // 2, lambda i: i), (NA_WIN_R, lambda i: NA_QROWS)]
    for pat, (r_rel, r0_rel) in enumerate(patterns):
        for i in range(NA_QROWS):
            for kr in range(NA_KROWS):
                dr = kr - i - r_rel + (NA_WIN_R - 1)
                valid = r0_rel(i) <= kr < r0_rel(i) + NA_WIN_R
                o_ref[pat, i * w:(i + 1) * w, kr * w:(kr + 1) * w] = t_s[dr] if valid else neg


def _na_bias_call(rpb):
    n_h = rpb.shape[0]
    nq = NA_QROWS * GRID_W
    nk = NA_KROWS * GRID_W
    return pl.pallas_call(
        _na_bias_kernel,
        grid=(n_h,),
        in_specs=[pl.BlockSpec(memory_space=pltpu.SMEM)],
        out_specs=pl.BlockSpec((None, 3, nq, nk), lambda h: (h, 0, 0, 0)),
        out_shape=jax.ShapeDtypeStruct((n_h, 3, nq, nk), F32),
        scratch_shapes=[pltpu.VMEM((2 * NA_WIN_R - 1, GRID_W, GRID_W), F32)],
        compiler_params=_cparams(1),
        name="na_bias",
    )(rpb.reshape(-1))


def _na_kernel(q_ref, k_ref, v_ref, kc_ref, vc_ref, bb_ref, o_ref, kcb_s, vcb_s, s_s, p_s, l_s,
               *, rows, scale):
    w = GRID_W
    nq = NA_QROWS * w
    nk = NA_KROWS * w
    n_blk = rows // NA_QROWS
    kcb_s[...] = kc_ref[...].astype(BF16)
    vcb_s[...] = vc_ref[...].astype(BF16)

    def q_rows(blk):
        return pl.ds(pl.multiple_of(jnp.int32(blk) * nq, nq), nq)

    def k_rows(blk):
        k_base = jnp.clip(jnp.int32(blk) * NA_QROWS - NA_WIN_R // 2, 0, rows - NA_KROWS)
        return pl.ds(pl.multiple_of(k_base * w, nq), nk)

    def scores(blk, slot):
        pat = jnp.where(blk == 0, 0, jnp.where(blk == n_blk - 1, 2, 1))
        q = q_ref[q_rows(blk), :]
        s_s[slot, :, 0:nk] = _dot_nt(q, k_ref[k_rows(blk), :]) + bb_ref[pat]
        s_s[slot, :, nk:] = _dot_nt(q, kcb_s[...])

    def softmax(slot):
        s = s_s[slot]
        p = jnp.exp2((s - jnp.max(s, axis=-1, keepdims=True)) * (scale * math.log2(math.e)))
        l_s[slot] = jnp.sum(p, axis=-1, keepdims=True)
        p_s[slot] = p.astype(p_s.dtype)

    def values(blk, slot):
        o = _dot(p_s[slot, :, 0:nk], v_ref[k_rows(blk), :]) + _dot(p_s[slot, :, nk:], vcb_s[...])
        o_ref[q_rows(blk), :] = (o / l_s[slot]).astype(o_ref.dtype)

    scores(0, 0)
    softmax(0)
    scores(1, 1)

    def body(jj, carry):
        values(2 * jj, 0)
        softmax(1)
        scores(2 * jj + 2, 0)
        values(2 * jj + 1, 1)
        softmax(0)
        scores(2 * jj + 3, 1)
        return carry

    lax.fori_loop(0, (n_blk - 2) // 2, body, 0)
    values(n_blk - 2, 0)
    softmax(1)
    values(n_blk - 1, 1)


def _na_call(q, k, v, cache_k, cache_v, bias, *, layer_e, n_b, seq_len, row_blk0):
    naw = q.shape[1]
    n_h = naw // HEAD_DIM
    past = cache_k.shape[2]
    rows = seq_len // GRID_W
    assert rows % (2 * NA_QROWS) == 0 and rows >= NA_KROWS
    nq, nk = NA_QROWS * GRID_W, NA_KROWS * GRID_W
    tok =pl.BlockSpec((seq_len, HEAD_DIM), lambda b, h: (row_blk0 + b, h))
    ctx = pl.BlockSpec((None, None, past, HEAD_DIM), lambda b, h: (b, layer_e, 0, h))
    return pl.pallas_call(
        functools.partial(_na_kernel, rows=rows, scale=HEAD_DIM ** -0.5),
        grid=(n_b, n_h),
        in_specs=[tok, tok, tok, ctx, ctx,
                  pl.BlockSpec((None,) + bias.shape[1:], lambda b, h: (h, 0, 0, 0))],
        out_specs=pl.BlockSpec((seq_len, HEAD_DIM), lambda b, h: (b, h)),
        out_shape=jax.ShapeDtypeStruct((n_b * seq_len, naw), BF16),
        scratch_shapes=[pltpu.VMEM((past, HEAD_DIM), BF16), pltpu.VMEM((past, HEAD_DIM), BF16),
                        pltpu.VMEM((2, nq, nk + past), F32), pltpu.VMEM((2, nq, nk + past), BF16),
                        pltpu.VMEM((2, nq, 1), F32)],
        compiler_params=_cparams(2),
        name="na_attn",
    )(q, k, v, cache_k, cache_v, bias)


def _about_kernel(*refs, n_x, n_p_tiles, nc, rc):
    g_ref, ap_ref, as_ref = refs[:3]
    x_refs = refs[3:3 + n_x]
    mod_ref, gw_ref, gb_ref, w_ref, o_ref = refs[3 + n_x:]
    s5w = g_ref.shape[1]
    tm, d = o_ref.shape

    def run(part):
        a_ref, x_ref = part
        for r0 in range(0, tm, rc):
            rows = slice(r0, r0 + rc)
            g = g_ref[rows, :]
            gl = _dot(g.astype(BF16), gw_ref[...]) + gb_ref[...]
            s5o = (g * jax.nn.sigmoid(gl)).astype(BF16)
            a = a_ref[rows, :]
            for n0 in range(0, d, nc):
                y = _dot(s5o, w_ref[0:s5w, n0:n0 + nc]) + _dot(a, w_ref[s5w:, n0:n0 + nc])
                o_ref[rows, n0:n0 + nc] = x_ref[rows, n0:n0 + nc] + mod_ref[2:3, n0:n0 + nc] * y

    _for_part(pl.program_id(0), n_p_tiles, [(ap_ref, x_refs[0]), (as_ref, x_refs[-1])], run)


def _about_call(g, attn_p, attn_s, xs, modt, glu_w, glu_b, w_out, *, layer, tm, n_p_tiles):
    m = sum(x.shape[0] for x in xs)
    d = xs[0].shape[1]
    s5w = g.shape[1]
    naw = attn_p.shape[1]
    nc = min(512, d)
    row = lambda i: (i, 0)
    return pl.pallas_call(
        functools.partial(_about_kernel, n_x=len(xs), n_p_tiles=n_p_tiles, nc=nc, rc=min(ROW_CHUNK, tm)),
        grid=(m // tm,),
        in_specs=[pl.BlockSpec((tm, s5w), row)] + _x_specs(2, tm, naw, n_p_tiles)
        + _x_specs(len(xs), tm, d, n_p_tiles) + [
            pl.BlockSpec((None, 8, d), lambda i: (i, 0, 0)),
            _const_spec((None, s5w, s5w), lambda i: (layer, 0, 0)),
            _const_spec((1, s5w), lambda i: (0, 0)),
            _const_spec((None, s5w + naw, d), lambda i: (layer, 0, 0)),
        ],
        out_specs=pl.BlockSpec((tm, d), row),
        out_shape=jax.ShapeDtypeStruct((m, d), F32),
        compiler_params=_cparams(1),
        name="ab_out_proj",
    )(g, attn_p, attn_s, *xs, modt, glu_w, glu_b, w_out)


def _mlp_kernel(x_ref, mod_ref, g_ref, w1_ref, w2_ref, o_ref, h_s, a_s, *, rc, nc1, nc2):
    tm, d = o_ref.shape
    tf = w1_ref.shape[1]
    ta = a_s.shape[1]

    def chunk_dots(rows):
        h = h_s[rows, :]
        for f0 in range(0, tf, ta):
            for c0 in range(0, ta, nc1):
                a = jnp.maximum(_dot(h, w1_ref[:, f0 + c0:f0 + c0 + nc1]), 0.0)
                a_s[rows, c0:c0 + nc1] = (a * a).astype(a_s.dtype)
            a = a_s[rows, :]
            for n0 in range(0, d, nc2):
                o_ref[rows, n0:n0 + nc2] += mod_ref[5:6, n0:n0 + nc2] * _dot(a, w2_ref[f0:f0 + ta, n0:n0 + nc2])

    @pl.when(pl.program_id(1) == 0)
    def _():
        g, shift, scale = g_ref[...], mod_ref[3:4, :], mod_ref[4:5, :]
        for r0 in range(0, tm, rc):
            rows = slice(r0, r0 + rc)
            x = x_ref[rows, :]
            h_s[rows, :] = _modulate(x, g, shift, scale).astype(h_s.dtype)
            o_ref[rows, :] = x
            chunk_dots(rows)

    @pl.when(pl.program_id(1) != 0)
    def _():
        chunk_dots(slice(0, tm))


def _mlp_call(x, modt, g, w1, w2, *, layer, tm, tf, tile0=0, n_tiles=None):
    d = x.shape[1]
    n_tiles = x.shape[0] // tm if n_tiles is None else n_tiles
    m = n_tiles * tm
    d_ff = w1.shape[2]
    rc = min(2 * ROW_CHUNK, tm)
    return pl.pallas_call(
        functools.partial(_mlp_kernel, rc=rc, nc1=min(256, tf), nc2=min(512, d)),
        grid=(n_tiles, d_ff // tf),
        in_specs=[
            pl.BlockSpec((tm, d), lambda i, f: (i + tile0, 0)),
            pl.BlockSpec((None, 8, d), lambda i, f: (i + tile0, 0, 0)),
            _const_spec((1, d), lambda i, f: (0, 0)),
            pl.BlockSpec((None, d, tf), lambda i, f: (layer, 0, f)),
            pl.BlockSpec((None, tf, d), lambda i, f: (layer, f, 0)),
        ],
        out_specs=pl.BlockSpec((tm, d), lambda i, f: (i, 0)),
        out_shape=jax.ShapeDtypeStruct((m, d), F32),
        scratch_shapes=[pltpu.VMEM((tm, d), BF16), pltpu.VMEM((tm, min(512, tf)), BF16)],
        compiler_params=_cparams(2),
        name="mlp",
    )(x, modt, g, w1, w2)


def _convin_kernel(x_ref, mod_ref, g_ref, w_ref, gb_ref, z_ref, h_s, *, rc, nc):
    tm, d = gb_ref.shape
    g, shift, scale = g_ref[...], mod_ref[0:1, :], mod_ref[1:2, :]
    for r0 in range(0, tm, rc):
        rows = slice(r0, r0 + rc)
        h_s[rows, :] = _modulate(x_ref[rows, :], g, shift, scale).astype(h_s.dtype)
        h = h_s[rows, :]
        for n0 in range(0, d, nc):
            gb_ref[rows, n0:n0 + nc] = _dot(h, w_ref[:, n0:n0 + nc]).astype(gb_ref.dtype)
            z_ref[rows, n0:n0 + nc] = (_dot(h, w_ref[:, d + n0:d + n0 + nc])
                                       * _dot(h, w_ref[:, 2 * d + n0:2 * d + n0 + nc])).astype(z_ref.dtype)


def _convin_call(x, modt, g, w_in, *, layer, tm):
    m, d = x.shape
    row = lambda i: (i, 0)
    return pl.pallas_call(
        functools.partial(_convin_kernel, rc=min(ROW_CHUNK, tm), nc=min(512, d)),
        grid=(m // tm,),
        in_specs=[
            pl.BlockSpec((tm, d), row),
            pl.BlockSpec((None, 8, d), lambda i: (i, 0, 0)),
            _const_spec((1, d), lambda i: (0, 0)),
            _const_spec((None, d, 3 * d), lambda i: (layer, 0, 0)),
        ],
        out_specs=[pl.BlockSpec((tm, d), row), pl.BlockSpec((tm, d), row)],
        out_shape=[jax.ShapeDtypeStruct((m, d), BF16), jax.ShapeDtypeStruct((m, d), BF16)],
        scratch_shapes=[pltpu.VMEM((tm, d), BF16)],
        compiler_params=_cparams(1),
        name="conv_in_proj",
    )(x, modt, g, w_in)


def _convout_kernel(gb_ref, z_ref, zp_ref, zn_ref, x_ref, mod_ref, cw_ref, cb_ref, w_ref, o_ref, t_s,
                    *, n_p_tiles, lp, ls, rc, cc, nc):
    i = pl.program_id(0)
    tm, d = z_ref.shape
    halo = zp_ref.shape[0]
    seq_mask = jnp.where(i < n_p_tiles, lp - 1, ls - 1)
    ridx = lax.broadcasted_iota(jnp.int32, (rc, 1), 0)
    chunk_aligned = lp % rc == 0 and ls % rc == 0
    for r0 in range(0, tm, rc):
        rows = slice(r0, r0 + rc)
        if chunk_aligned:
            starts = ((i * tm + r0) & seq_mask) == 0
            ends = ((i * tm + r0 + rc) & seq_mask) == 0
        else:
            pos = (i * tm + r0 + ridx) & seq_mask
            is_start = pos == 0
            is_end = pos == seq_mask
        for c0 in range(0, d, cc):
            cs = slice(c0, c0 + cc)
            z = z_ref[rows, cs].astype(F32)
            before = (zp_ref[halo - 1:halo, cs] if r0 == 0 else z_ref[r0 - 1:r0, cs]).astype(F32)
            after = (zn_ref[0:1, cs] if r0 + rc == tm else z_ref[r0 + rc:r0 + rc + 1, cs]).astype(F32)
            if chunk_aligned:
                before = jnp.where(starts, 0.0, before)
                after = jnp.where(ends, 0.0, after)
            z_prev = jnp.where(ridx == 0, before, pltpu.roll(z, 1, 0))
            z_next = jnp.where(ridx == rc - 1, after, pltpu.roll(z, rc - 1, 0))
            if not chunk_aligned:
                z_prev = jnp.where(is_start, 0.0, z_prev)
                z_next = jnp.where(is_end, 0.0, z_next)
            conv = z_prev * cw_ref[0:1, cs] + z * cw_ref[1:2, cs] + z_next * cw_ref[2:3, cs] + cb_ref[:, cs]
            t_s[rows, cs] = (gb_ref[rows, cs].astype(F32) * conv).astype(t_s.dtype)
        t = t_s[rows, :]
        for n0 in range(0, d, nc):
            o_ref[rows, n0:n0 + nc] = (x_ref[rows, n0:n0 + nc]
                                       + mod_ref[2:3, n0:n0 + nc] * _dot(t, w_ref[:, n0:n0 + nc]))


def _convout_call(gb, z, x, modt, conv_w, conv_b, w_out, *, layer, tm, n_p_tiles, lp, ls):
    m, d = x.shape
    halo = 16
    assert tm % halo == 0 and lp & (lp - 1) == 0 and ls & (ls - 1) == 0
    assert (n_p_tiles * tm) % ls == 0 or True
    hb = tm // halo
    last = m // halo - 1
    row = lambda i: (i, 0)
    return pl.pallas_call(
        functools.partial(_convout_kernel, n_p_tiles=n_p_tiles, lp=lp, ls=ls, rc=min(ROW_CHUNK, tm),
                          cc=min(256, d), nc=min(512, d)),
        grid=(m // tm,),
        in_specs=[
            pl.BlockSpec((tm, d), row),
            pl.BlockSpec((tm, d), row),
            pl.BlockSpec((halo, d), lambda i: (jnp.maximum(i * hb - 1, 0), 0)),
            pl.BlockSpec((halo, d), lambda i: (jnp.minimum((i + 1) * hb, last), 0)),
            pl.BlockSpec((tm, d), row),
            pl.BlockSpec((None, 8, d), lambda i: (i, 0, 0)),
            _const_spec((8, d), lambda i: (0, 0)),
            _const_spec((1, d), lambda i: (0, 0)),
            _const_spec((None, d, d), lambda i: (layer, 0, 0)),
        ],
        out_specs=pl.BlockSpec((tm, d), row),
        out_shape=jax.ShapeDtypeStruct((m, d), F32),
        scratch_shapes=[pltpu.VMEM((tm, d), BF16)],
        compiler_params=_cparams(1),
        name="conv_out_proj",
    )(gb, z, z, z, x, modt, conv_w, conv_b, w_out)


def _tile_rows(m_p, m_s, l_s, n_b, tm):
    assert m_p % tm == 0 and l_s % tm == 0
    return np.concatenate([np.full(m_p // tm, n_b), np.repeat(np.arange(n_b), l_s // tm)])


def _pick_tile(pref, m_p, l_s):
    tm = pref
    while m_p % tm or l_s % tm:
        tm //= 2
    return tm


def kernel(x_prompt, x_sample, c, cache_k, cache_v, state_ssm_re, state_ssm_im, c_ctx, ada_w, ada_b, norm1_g, norm2_g, ab_w_in, ab_w_out, s5_lam_re, s5_lam_im, s5_log_dt, s5_b_re, s5_b_im, s5_c_re, s5_c_im, s5_d, s5_glu_w, s5_glu_b, q_norm_g, k_norm_g, na_rpb, conv_w_in, conv_w, conv_b, conv_w_out, mlp_w1, mlp_w2):
    n_bp, l_p, d = x_prompt.shape
    n_bs, l_s, _ = x_sample.shape
    depth = ada_w.shape[0]
    m_p, m_s = n_bp * l_p, n_bs * l_s
    m = m_p + m_s
    n_g, n_state = s5_lam_re.shape[2], s5_lam_re.shape[3]
    s5w = n_g * S5_GROUP_CH
    naw = (ab_w_in.shape[2] - s5w) // 3
    n_h = naw // HEAD_DIM
    past = cache_k.shape[2]
    d_ff = mlp_w1.shape[2]
    assert m_p % l_s == 0, "latent sequences must start on a sequence-length row block"
    assert l_p % S5_CHUNK == 0 and l_s % S5_CHUNK == 0
    c_p, c_s = l_p // S5_CHUNK, l_s // S5_CHUNK
    assert c_p & (c_p - 1) == 0 and c_s & (c_s - 1) == 0
    r_p, r_s = m_p // S5_CHUNK, m_s // S5_CHUNK
    width = S5_CHUNK * S5_GROUP_CH

    xs = [x_prompt.reshape(m_p, d), x_sample.reshape(m_s, d)]

    n_rows = -(-(n_bs + 1) // 16) * 16
    cvec = jnp.concatenate([c, c_ctx[None], jnp.zeros((n_rows - n_bs - 1, d), F32)], axis=0)
    mod = _ada_call(cvec, ada_w, ada_b)

    def mod_tiles(layer, tm):
        t = mod[layer][_tile_rows(m_p, m_s, l_s, n_bs, tm)].reshape(m // tm, 6, d)
        return jnp.pad(t, ((0, 0), (0, 2), (0, 0)))

    tm_big = _pick_tile(1024, m_p, l_s)
    tm_mid = _pick_tile(512, m_p, l_s)
    tf = min(1024, d_ff)

    n_pw = max(int(math.log2(c_s)), 1)
    w_u, w_y, pw = _s5prep_call(s5_lam_re, s5_lam_im, s5_log_dt, s5_b_re, s5_b_im, s5_c_re, s5_c_im,
                                s5_d, n_pw)
    n_h0 = -(-n_bs // 8) * 8
    h0_all = jnp.stack([state_ssm_re, state_ssm_im], axis=2)
    h0_all = h0_all.transpose(1, 4, 0, 2, 3, 5).reshape(-1, n_g, n_bs, 4 * n_state)
    h0_all = jnp.pad(h0_all, ((0, 0), (0, 0), (0, n_h0 - n_bs), (0, 0)))
    cache_k4 = cache_k.reshape(n_bs, -1, past, naw)
    cache_v4 = cache_v.reshape(n_bs, -1, past, naw)
    ab_w_in_b, ab_w_out_b, glu_w_b = ab_w_in.astype(BF16), ab_w_out.astype(BF16), s5_glu_w.astype(BF16)
    conv_w_in_b, conv_w_out_b = conv_w_in.astype(BF16), conv_w_out.astype(BF16)
    mlp_w1_b, mlp_w2_b = mlp_w1.astype(BF16), mlp_w2.astype(BF16)

    caches = None
    new_re, new_im = [], []
    for layer in range(depth):
        g1 = norm1_g[layer].reshape(1, d)
        g2 = norm2_g[layer].reshape(1, d)
        if layer % 2 == 0:
            e = layer // 2
            u, q, k, v, *caches = _inproj_call(
                xs, mod_tiles(layer, tm_mid), g1, ab_w_in_b,
                q_norm_g[e].reshape(1, HEAD_DIM), k_norm_g[e].reshape(1, HEAD_DIM), caches,
                layer=e, n_e=ab_w_in.shape[0], l_p=l_p, tm=tm_mid, s5w=s5w, naw=naw, n_p_tiles=m_p // tm_mid)
            g_tok, z_all = _s5_call(u, w_u[e], w_y[e], pw[e], h0_all[e],
                                    l_tile=l_s, n_p_tiles=m_p // l_s, cp=c_p, cs=c_s)
            z_p = z_all.reshape(n_g, n_bp, c_p, 4 * n_state)
            fin = z_p[:, :, c_p - 1].reshape(n_g, n_bp, 2, 2, n_state)
            new_re.append(fin[:, :, 0].transpose(1, 2, 0, 3))
            new_im.append(fin[:, :, 1].transpose(1, 2, 0, 3))
            attn_p = _ctx_attn_call(q, k, v, n_seq=n_bp, seq_len=l_p)
            bias = _na_bias_call(na_rpb[e])
            attn_s = _na_call(q, k, v, cache_k4, cache_v4, bias, layer_e=e, n_b=n_bs, seq_len=l_s,
                              row_blk0=m_p // l_s)
            x = _about_call(g_tok, attn_p, attn_s, xs, mod_tiles(layer, tm_mid), glu_w_b,
                            s5_glu_b[e].reshape(1, s5w), ab_w_out_b, layer=e, tm=tm_mid,
                            n_p_tiles=m_p // tm_mid)
        else:
            x = xs[0]
            o = layer // 2
            gb, z = _convin_call(x, mod_tiles(layer, tm_mid), g1, conv_w_in_b, layer=o, tm=tm_mid)
            cw8 = jnp.pad(conv_w[o], ((0, 5), (0, 0)))
            x = _convout_call(gb, z, x, mod_tiles(layer, tm_mid), cw8, conv_b[o].reshape(1, d),
                              conv_w_out_b, layer=o, tm=tm_mid, n_p_tiles=m_p // tm_mid,
                              lp=l_p, ls=l_s)
        mlp = functools.partial(_mlp_call, x, mod_tiles(layer, tm_big), g2, mlp_w1_b, mlp_w2_b,
                                layer=layer, tm=tm_big, tf=tf)
        if layer < depth - 1:
            xs = [mlp()]
        else:
            y_prompt = mlp(tile0=0, n_tiles=m_p // tm_big).reshape(n_bp, l_p, d)
            y_sample = mlp(tile0=m_p // tm_big, n_tiles=m_s // tm_big).reshape(n_bs, l_s, d)

    new_k, new_v = (t.reshape(n_bp, -1, l_p, n_h, HEAD_DIM) for t in caches)
    return (y_prompt, y_sample, new_k, new_v, jnp.stack(new_re, axis=1), jnp.stack(new_im, axis=1))
```

```python
import functools
import math

import numpy as np
import jax
import jax.numpy as jnp
from jax import lax
from jax.experimental import pallas as pl
from jax.experimental.pallas import tpu as pltpu

F32 = jnp.float32
BF16 = jnp.bfloat16

NORM_EPS = 1e-6
NEG_INF = -1e30

S5_GROUP_CH = 16
HEAD_DIM = 128
GRID_W = 64
NA_WIN_R = 8
NA_WIN_C = 16
S5_CHUNK = 16
NA_QROWS = 4
NA_KROWS = NA_QROWS + NA_WIN_R
ROW_CHUNK = 256

LANES = 128
VMEM_LIMIT_BYTES = 62 * 1024 * 1024


def _cparams(n_axes):
    return pltpu.CompilerParams(dimension_semantics=("arbitrary",) * n_axes,
                                vmem_limit_bytes=VMEM_LIMIT_BYTES)


def _const_spec(shape, index_map):
    return pl.BlockSpec(shape, index_map, pipeline_mode=pl.Buffered(1))


def _dot(a, b):
    return jnp.dot(a, b, preferred_element_type=F32)


def _dot_nt(a, b):
    return lax.dot_general(a, b, (((1,), (1,)), ((), ())), preferred_element_type=F32)


def _dot_split(a, b):
    a_hi = a.astype(BF16)
    a_lo = (a - a_hi.astype(F32)).astype(BF16)
    b_hi = b.astype(BF16)
    b_lo = (b - b_hi.astype(F32)).astype(BF16)
    return _dot(a_hi, b_hi) + _dot(a_hi, b_lo) + _dot(a_lo, b_hi)


def _modulate(x, g, shift, scale):
    ms = jnp.mean(x * x, axis=-1, keepdims=True)
    y = x * lax.rsqrt(ms + NORM_EPS) * g
    return y * (1.0 + scale) + shift


def _ada_kernel(c_ref, w_ref, b_ref, o_ref, *, nc):
    cv = c_ref[...]
    sc = (cv * jax.nn.sigmoid(cv)).astype(BF16)
    tn = w_ref.shape[1]
    for n0 in range(0, tn, nc):
        w = w_ref[:, n0:n0 + nc].astype(BF16)
        o_ref[:, n0:n0 + nc] = _dot(sc, w) + b_ref[:, n0:n0 + nc]


def _ada_call(cvec, ada_w, ada_b):
    depth, d, n6 = ada_w.shape
    rows = cvec.shape[0]
    tn = 1536 if n6 % 1536 == 0 else n6
    nc = 512 if tn % 512 == 0 else tn
    return pl.pallas_call(
        functools.partial(_ada_kernel, nc=nc),
        grid=(depth, n6 // tn),
        in_specs=[
            pl.BlockSpec((rows, d), lambda l, j: (0, 0)),
            pl.BlockSpec((None, d, tn), lambda l, j: (l, 0, j)),
            pl.BlockSpec((None, 1, tn), lambda l, j: (l, 0, j)),
        ],
        out_specs=pl.BlockSpec((None, rows, tn), lambda l, j: (l, 0, j)),
        out_shape=jax.ShapeDtypeStruct((depth, rows, n6), F32),
        compiler_params=_cparams(2),
        name="ada_params",
    )(cvec, ada_w, ada_b.reshape(depth, 1, n6))


def _x_specs(n_x, tm, d, n_p_tiles):
    if n_x == 1:
        return [pl.BlockSpec((tm, d), lambda i: (i, 0))]
    return [pl.BlockSpec((tm, d), lambda i: (jnp.minimum(i, n_p_tiles - 1), 0)),
            pl.BlockSpec((tm, d), lambda i: (jnp.maximum(i - n_p_tiles, 0), 0))]


def _row_chunks(tm, rc):
    return [(r0, min(rc, tm - r0)) for r0 in range(0, tm, rc)]


def _for_part(i, n_p_tiles, refs, fn):
    if len(refs) == 1:
        fn(refs[0])
    else:
        pl.when(i < n_p_tiles)(lambda: fn(refs[0]))
        pl.when(i >= n_p_tiles)(lambda: fn(refs[1]))


def _inproj_kernel(*refs, n_x, n_alias, n_p_tiles, l_p, s5w, naw, rc):
    x_refs = refs[:n_x]
    mod_ref, g_ref, w_ref, qg_ref, kg_ref = refs[n_x:n_x + 5]
    u_ref, q_ref, k_ref, v_ref, ck_ref, cv_ref, h_s = refs[n_x + 5 + n_alias:]
    i = pl.program_id(0)
    tm = u_ref.shape[0]
    n_heads = naw // HEAD_DIM

    def cache_store(c_ref, r0, col0, val):
        head = col0 // HEAD_DIM
        t0 = r0
        while t0 < r0 + val.shape[0]:
            piece = min(r0 + val.shape[0], (t0 // l_p + 1) * l_p) - t0
            c_ref[t0 // l_p, pl.ds((t0 % l_p) * n_heads + head, piece, stride=n_heads), :] = (
                val[t0 - r0:t0 - r0 + piece])
            t0 += piece
    hd = HEAD_DIM
    cw = min(4 * hd, naw)
    g, shift, scale = g_ref[...], mod_ref[0:1, :], mod_ref[1:2, :]
    qg = qg_ref[...]
    kg = kg_ref[...]

    def head_norm(t, gain):
        return t * lax.rsqrt(jnp.mean(t * t, axis=-1, keepdims=True) + NORM_EPS) * gain

    def maybe_cache(prompt, fn):
        if prompt:
            fn()

    def run(x_ref, prompt):
        for r0, rn in _row_chunks(tm, rc):
            rows = slice(r0, r0 + rn)
            h_s[rows, :] = _modulate(x_ref[rows, :], g, shift, scale).astype(h_s.dtype)
            h = h_s[rows, :]
            u_ref[rows, :] = _dot(h, w_ref[:, 0:s5w]).astype(u_ref.dtype)
            for c0 in range(0, naw, cw):
                qc = _dot(h, w_ref[:, s5w + c0:s5w + c0 + cw])
                for j in range(cw // hd):
                    q_ref[rows, c0 + j * hd:c0 + (j + 1) * hd] = head_norm(
                        qc[:, j * hd:(j + 1) * hd], qg).astype(q_ref.dtype)
                kc = _dot(h, w_ref[:, s5w + naw + c0:s5w + naw + c0 + cw])
                for j in range(cw // hd):
                    kh = head_norm(kc[:, j * hd:(j + 1) * hd], kg)
                    k_ref[rows, c0 + j * hd:c0 + (j + 1) * hd] = kh.astype(k_ref.dtype)
                    maybe_cache(prompt, functools.partial(cache_store, ck_ref, r0, c0 + j * hd, kh))
                vc = _dot(h, w_ref[:, s5w + 2 * naw + c0:s5w + 2 * naw + c0 + cw])
                v_ref[rows, c0:c0 + cw] = vc.astype(v_ref.dtype)
                for j in range(cw // hd):
                    maybe_cache(prompt, functools.partial(cache_store, cv_ref, r0, c0 + j * hd,
                                                          vc[:, j * hd:(j + 1) * hd]))

    pl.when(i < n_p_tiles)(lambda: run(x_refs[0], True))
    pl.when(i >= n_p_tiles)(lambda: run(x_refs[-1], False))


def _inproj_call(xs, modt, g, w_in, qg, kg, caches, *, layer, n_e, l_p, tm, s5w, naw, n_p_tiles):
    m = sum(x.shape[0] for x in xs)
    d = xs[0].shape[1]
    n_in = w_in.shape[2]
    rc = min(ROW_CHUNK, tm)
    n_heads = naw // HEAD_DIM
    assert tm % l_p == 0 and (rc % l_p == 0 or l_p % rc == 0)
    seqs = tm // l_p
    cache_shape = jax.ShapeDtypeStruct((n_p_tiles * seqs, n_e, l_p * n_heads, HEAD_DIM), F32)
    cache_spec = pl.BlockSpec((seqs, None, l_p * n_heads, HEAD_DIM),
                              lambda i: (jnp.minimum(i, n_p_tiles - 1), layer, 0, 0))
    n_alias = 0 if caches is None else 2
    alias_args = [] if caches is None else list(caches)
    n_in_args = len(xs) + 5
    row = lambda i: (i, 0)
    return pl.pallas_call(
        functools.partial(_inproj_kernel, n_x=len(xs), n_alias=n_alias, n_p_tiles=n_p_tiles, l_p=l_p,
                          s5w=s5w, naw=naw, rc=rc),
        grid=(m // tm,),
        in_specs=_x_specs(len(xs), tm, d, n_p_tiles) + [
            pl.BlockSpec((None, 8, d), lambda i: (i, 0, 0)),
            _const_spec((1, d), lambda i: (0, 0)),
            _const_spec((None, d, n_in), lambda i: (layer, 0, 0)),
            _const_spec((1, HEAD_DIM), lambda i: (0, 0)),
            _const_spec((1, HEAD_DIM), lambda i: (0, 0)),
        ] + [pl.BlockSpec(memory_space=pl.ANY)] * n_alias,
        out_specs=[
            pl.BlockSpec((tm, s5w), row),
            pl.BlockSpec((tm, naw), row),
            pl.BlockSpec((tm, naw), row),
            pl.BlockSpec((tm, naw), row),
            cache_spec, cache_spec,
        ],
        out_shape=[
            jax.ShapeDtypeStruct((m, s5w), F32),
            jax.ShapeDtypeStruct((m, naw), BF16),
            jax.ShapeDtypeStruct((m, naw), BF16),
            jax.ShapeDtypeStruct((m, naw), BF16),
            cache_shape, cache_shape,
        ],
        input_output_aliases={n_in_args + a: 4 + a for a in range(n_alias)},
        scratch_shapes=[pltpu.VMEM((tm, d), BF16)],
        compiler_params=_cparams(1),
        name="ab_in_proj",
    )(*xs, modt, g, w_in, qg, kg, *alias_args)


def _s5prep_kernel(row_ref, row2_ref, col_ref, btr_ref, bti_ref, cer_ref, cei_ref, d_ref,
                   wu_ref, wy_ref, pw_ref, *, n_state, n_pw):
    t_chunk = S5_CHUNK
    width = t_chunk * S5_GROUP_CH
    p = n_state
    lg_ch = int(math.log2(S5_GROUP_CH))
    r_i = lax.broadcasted_iota(jnp.int32, (width, 1), 0)
    c_i = lax.broadcasted_iota(jnp.int32, (1, width), 1)
    s_i = r_i >> lg_ch
    t_i = c_i >> lg_ch
    s16 = lax.broadcasted_iota(jnp.int32, (t_chunk, 1), 0)
    n_pow = -(-(t_chunk + 1) // 8) * 8

    def cmul(ar, ai, br, bi):
        return ar * br - ai * bi, ar * bi + ai * br

    acc = jnp.zeros((width, width), F32)
    for d in range(2):
        lr = row_ref[d, 0:1, 0:p]
        li = row_ref[d, 1:2, 0:p]
        dt = jnp.exp(row_ref[d, 2:3, 0:p])
        ar = lr * dt
        ai = li * dt
        er = jnp.exp(ar)
        abr = er * jnp.cos(ai)
        abi = er * jnp.sin(ai)
        den = lr * lr + li * li
        nr = abr - 1.0
        f_re = (nr * lr + abi * li) / den
        f_im = (abi * lr - nr * li) / den
        bb_re, bb_im = cmul(f_re, f_im, btr_ref[d], bti_ref[d])

        lrc = col_ref[d, :, 0:1]
        lic = col_ref[d, :, 1:2]
        dtc = jnp.exp(col_ref[d, :, 2:3])
        arc = lrc * dtc
        aic = lic * dtc
        ce_re = cer_ref[d]
        ce_im = cei_ref[d]

        k_r = lax.broadcasted_iota(jnp.int32, (n_pow, 1), 0).astype(F32)
        k_c = lax.broadcasted_iota(jnp.int32, (1, LANES), 1).astype(F32)
        mag_r = jnp.exp(k_r * ar)
        tr_re, tr_im = mag_r * jnp.cos(k_r * ai), mag_r * jnp.sin(k_r * ai)
        mag_c = jnp.exp(arc * k_c)
        tc_re, tc_im = mag_c * jnp.cos(aic * k_c), mag_c * jnp.sin(aic * k_c)

        def pow_row(e, e_max):
            re = jnp.zeros((t_chunk, p), F32)
            im = jnp.zeros((t_chunk, p), F32)
            for kk in range(e_max + 1):
                hit = e == kk
                re = jnp.where(hit, tr_re[kk:kk + 1, :], re)
                im = jnp.where(hit, tr_im[kk:kk + 1, :], im)
            rep = lambda t: jnp.broadcast_to(t[:, None, :], (t_chunk, S5_GROUP_CH, p)).reshape(width, p)
            return rep(re), rep(im)

        def pow_col(e, e_max):
            re = jnp.zeros((p, width), F32)
            im = jnp.zeros((p, width), F32)
            for kk in range(e_max + 1):
                hit = e == kk
                re = jnp.where(hit, tc_re[:, kk:kk + 1], re)
                im = jnp.where(hit, tc_im[:, kk:kk + 1], im)
            return re, im

        terms = [(jnp.zeros_like(s16), jnp.zeros_like(t_i), s_i == t_i, 0)]
        blk = 2
        while blk <= t_chunk:
            lb = int(math.log2(blk))
            half = blk // 2
            mid = lambda idx: ((idx >> lb) << lb) + half
            same = (s_i >> lb) == (t_i >> lb)
            if d == 0:
                e_s, ok_s = mid(s16) - s16, s_i < mid(s_i)
                e_t, ok_t = t_i - mid(t_i), t_i >= mid(t_i)
            else:
                e_s, ok_s = s16 - mid(s16), s_i >= mid(s_i)
                e_t, ok_t = mid(t_i) - t_i, t_i < mid(t_i)
            terms.append((jnp.maximum(e_s, 0), jnp.maximum(e_t, 0), same & ok_s & ok_t, half))
            blk *= 2
        for e_s, e_t, mask, e_max in terms:
            l_re, l_im = cmul(*pow_row(e_s, e_max), bb_re, bb_im)
            r_re, r_im = cmul(*pow_col(e_t, e_max), ce_re, ce_im)
            term = _dot_split(l_re, r_re) - _dot_split(l_im, r_im)
            acc = acc + jnp.where(mask, term, 0.0)

        if d == 0:
            ws_re, ws_im = cmul(*pow_row(t_chunk - 1 - s16, t_chunk), bb_re, bb_im)
            ca_re, ca_im = cmul(*pow_col(t_i + 1, t_chunk), ce_re, ce_im)
        else:
            ws_re, ws_im = cmul(*pow_row(s16, t_chunk), bb_re, bb_im)
            ca_re, ca_im = cmul(*pow_col(t_chunk - t_i, t_chunk), ce_re, ce_im)
        wu_ref[:, width + d * p:width + (d + 1) * p] = ws_re.astype(wu_ref.dtype)
        wu_ref[:, width + (2 + d) * p:width + (3 + d) * p] = ws_im.astype(wu_ref.dtype)
        wy_ref[d * p:(d + 1) * p, :] = ca_re.astype(wy_ref.dtype)
        wy_ref[(2 + d) * p:(3 + d) * p, :] = (-ca_im).astype(wy_ref.dtype)

    lr2 = row2_ref[0:1, :]
    li2 = row2_ref[1:2, :]
    dt2 = jnp.exp(row2_ref[2:3, :])
    k16 = float(t_chunk)
    mag = jnp.exp(k16 * (lr2 * dt2))
    pr = mag * jnp.cos(k16 * (li2 * dt2))
    pi = mag * jnp.sin(k16 * (li2 * dt2))
    for i in range(n_pw):
        pw_ref[2 * i:2 * i + 1, :] = pr
        pw_ref[2 * i + 1:2 * i + 2, :] = pi
        pr, pi = pr * pr - pi * pi, 2.0 * pr * pi

    wu_ref[:, 0:width] = (acc + jnp.where(r_i == c_i, d_ref[...], 0.0)).astype(wu_ref.dtype)


def _s5prep_call(lam_re, lam_im, log_dt, b_re, b_im, c_re, c_im, d_skip, n_pw):
    n_e, _, n_g, p = lam_re.shape
    n_ch = S5_GROUP_CH
    width = S5_CHUNK * n_ch
    assert 2 * p == 128, "state rows are packed as [fwd | bwd] in one 128-lane tile"
    n_pw_rows = -(-2 * n_pw // 8) * 8
    dtb = jnp.broadcast_to(log_dt[..., None], lam_re.shape)
    zeros = jnp.zeros_like(lam_re)
    rowp = jnp.stack([lam_re, lam_im, dtb] + [zeros] * 5, axis=-2)
    row2 = jnp.concatenate([rowp[:, 0], rowp[:, 1]], axis=-1)
    colp = jnp.stack([lam_re, lam_im, dtb] + [zeros] * 5, axis=-1)
    bt = lambda b: jnp.tile(jnp.swapaxes(b, -1, -2), (1, 1, 1, S5_CHUNK, 1))
    ce = lambda c: jnp.tile(jnp.swapaxes(c, -1, -2), (1, 1, 1, 1, S5_CHUNK))
    d_row = jnp.tile(d_skip.reshape(n_e, n_g, 1, n_ch), (1, 1, 1, S5_CHUNK))

    def dspec(shape):
        return pl.BlockSpec((None, 2, None) + shape, lambda e, g: (e, 0, g, 0, 0))

    def ospec(shape):
        return pl.BlockSpec((None, None) + shape, lambda e, g: (e, g, 0, 0))

    return pl.pallas_call(
        functools.partial(_s5prep_kernel, n_state=p, n_pw=n_pw),
        grid=(n_e, n_g),
        in_specs=[dspec((8, p)), ospec((8, 2 * p)), dspec((p, 8)), dspec((width, p)), dspec((width, p)),
                  dspec((p, width)), dspec((p, width)), ospec((1, width))],
        out_specs=[ospec((width, 2 * width)), ospec((width, width)), ospec((n_pw_rows, 2 * p))],
        out_shape=[
            jax.ShapeDtypeStruct((n_e, n_g, width, 2 * width), BF16),
            jax.ShapeDtypeStruct((n_e, n_g, width, width), BF16),
            jax.ShapeDtypeStruct((n_e, n_g, n_pw_rows, 2 * p), F32),
        ],
        compiler_params=_cparams(2),
        name="s5_prep",
    )(rowp, row2, colp, bt(b_re), bt(b_im), ce(c_re), ce(c_im), d_row)


def _gelu_tanh(y):
    return 0.5 * y * (1.0 + jnp.tanh(0.7978845608028654 * (y + 0.044715 * (y * y * y))))


def _block_transpose8(v, lane):
    for dist in (4, 2, 1):
        width = dist * S5_GROUP_CH
        low = (lane & width) == 0
        out = list(v)
        for a in range(8):
            if a & dist == 0:
                lo, hi = v[a], v[a + dist]
                out[a] = jnp.where(low, lo, pltpu.roll(hi, width, 1))
                out[a + dist] = jnp.where(low, pltpu.roll(lo, LANES - width, 1), hi)
        v = out
    return v


def _s5_kernel(u_ref, wu_ref, wy_ref, pw_ref, h0_ref, g_ref, z_ref, zs_s, ug_s, yg_s,
               *, n_p_tiles, cp, cs):
    tile = pl.program_id(1)
    t_chunk = S5_CHUNK
    n = u_ref.shape[0] // t_chunk
    ngb, width = wy_ref.shape[0], wy_ref.shape[1]
    half = width // 2
    lanes = u_ref.shape[1]

    is_fwd = lax.broadcasted_iota(jnp.int32, (1, half), 1) < half // 2
    is_fwd2 = (lax.broadcasted_iota(jnp.int32, (1, width), 1) & (half - 1)) < half // 2

    lane = lax.broadcasted_iota(jnp.int32, (1, lanes), 1)
    for s_hi in range(t_chunk // 8):
        w = _block_transpose8([u_ref[pl.ds(s_hi * 8 + a, n, stride=t_chunk), :] for a in range(8)], lane)
        for gi in range(ngb):
            ug_s[gi, :, s_hi * lanes:(s_hi + 1) * lanes] = w[gi].astype(ug_s.dtype)

    def reversal(cseq):
        r = lax.broadcasted_iota(jnp.int32, (n, n), 0)
        c = lax.broadcasted_iota(jnp.int32, (n, n), 1)
        lg = int(math.log2(cseq))
        hit = ((r >> lg) == (c >> lg)) & ((r & (cseq - 1)) + (c & (cseq - 1)) == cseq - 1)
        return jnp.where(hit, 1.0, 0.0).astype(BF16)

    def run(cseq, j):
        rev = reversal(cseq)
        for gi in range(ngb):
            ug = ug_s[gi]
            wu = wu_ref[gi]
            ys = _dot(ug, wu)
            ug_rev = _dot(rev, ug).astype(BF16)
            ys_rev = _dot(ug_rev, wu[:, width:])
            yg_s[gi] = ys[:, 0:width]
            zs_s[0, gi] = jnp.where(is_fwd, ys[:, width:width + half], ys_rev[:, 0:half])
            zs_s[1, gi] = jnp.where(is_fwd, ys[:, width + half:], ys_rev[:, half:])

        zr, zi = zs_s[0], zs_s[1]
        pos = lax.broadcasted_iota(jnp.int32, (1, n, 1), 1) & (cseq - 1)
        edge = pos == 0
        if j is not None:
            h0 = h0_ref[:, pl.ds(j, 1), :]
            h0r, h0i = h0[:, :, 0:half], h0[:, :, half:]
            pr, pi = pw_ref[:, 0:1, :], pw_ref[:, 1:2, :]
            zr = zr + jnp.where(edge, pr * h0r - pi * h0i, 0.0)
            zi = zi + jnp.where(edge, pr * h0i + pi * h0r, 0.0)
        k = 1
        i = 0
        while k < cseq:
            ar = pw_ref[:, 2 * i:2 * i + 1, :]
            ai = pw_ref[:, 2 * i + 1:2 * i + 2, :]
            sr, si = pltpu.roll(zr, k, 1), pltpu.roll(zi, k, 1)
            valid = pos >= k
            zr = zr + jnp.where(valid, ar * sr - ai * si, 0.0)
            zi = zi + jnp.where(valid, ar * si + ai * sr, 0.0)
            k *= 2
            i += 1
        if j is None:
            z_ref[...] = jnp.concatenate([zr, zi], axis=2)
        xr, xi = pltpu.roll(zr, 1, 1), pltpu.roll(zi, 1, 1)
        if j is None:
            zs_s[0], zs_s[1] = jnp.where(edge, 0.0, xr), jnp.where(edge, 0.0, xi)
        else:
            zs_s[0], zs_s[1] = jnp.where(edge, h0r, xr), jnp.where(edge, h0i, xi)

        for gi in range(ngb):
            x = jnp.concatenate([zs_s[0, gi], zs_s[1, gi]], axis=1)
            x_rev = _dot(rev, x.astype(BF16))
            x = jnp.where(is_fwd2, x, x_rev).astype(BF16)
            yg_s[gi] = _gelu_tanh(yg_s[gi] + _dot(x, wy_ref[gi]))

    pl.when(tile < n_p_tiles)(lambda: run(cp, None))
    pl.when(tile >= n_p_tiles)(lambda: run(cs, tile - n_p_tiles))

    for t_hi in range(t_chunk // 8):
        w = _block_transpose8([yg_s[gi, :, t_hi * lanes:(t_hi + 1) * lanes] for gi in range(ngb)], lane)
        for a in range(8):
            g_ref[pl.ds(t_hi * 8 + a, n, stride=t_chunk), :] = w[a]


def _s5_call(u, w_u, w_y, pw, h0, *, l_tile, n_p_tiles, cp, cs):
    m, s5w = u.shape
    n_g, width = w_y.shape[0], w_y.shape[1]
    ngb = min(8, n_g)
    lanes = ngb * S5_GROUP_CH
    assert lanes == 128 and ngb == 8 and S5_CHUNK % 8 == 0 and n_g % ngb == 0 and m % l_tile == 0
    n = l_tile // S5_CHUNK
    n_h0 = h0.shape[1]
    tok = pl.BlockSpec((l_tile, lanes), lambda cb, t: (t, cb))
    grp = lambda shape: pl.BlockSpec((ngb,) + shape, lambda cb, t: (cb, 0, 0))
    return pl.pallas_call(
        functools.partial(_s5_kernel, n_p_tiles=n_p_tiles, cp=cp, cs=cs),
        grid=(n_g // ngb, m // l_tile),
        in_specs=[tok, grp((width, 2 * width)), grp((width, width)), grp(pw.shape[1:]), grp((n_h0, width))],
        out_specs=[tok, pl.BlockSpec((ngb, n, width), lambda cb, t: (cb, jnp.minimum(t, n_p_tiles - 1), 0))],
        out_shape=[jax.ShapeDtypeStruct((m, s5w), F32),
                   jax.ShapeDtypeStruct((n_g, n_p_tiles * n, width), F32)],
        scratch_shapes=[pltpu.VMEM((2, ngb, n, width // 2), F32),
                        pltpu.VMEM((ngb, n, width), BF16),
                        pltpu.VMEM((ngb, n, width), F32)],
        compiler_params=_cparams(2),
        name="s5_core",
    )(u, w_u, w_y, pw, h0)


def _ctx_attn_kernel(q_ref, k_ref, v_ref, o_ref, *, scale):
    hd = HEAD_DIM
    for h in range(q_ref.shape[1] // hd):
        sl = slice(h * hd, (h + 1) * hd)
        s = _dot_nt(q_ref[:, sl], k_ref[:, sl]) * scale
        m = jnp.max(s, axis=-1, keepdims=True)
        p = jnp.exp(s - m)
        l = jnp.sum(p, axis=-1, keepdims=True)
        o = _dot(p.astype(BF16), v_ref[:, sl]) / l
        o_ref[:, sl] = o.astype(o_ref.dtype)


def _ctx_attn_call(q, k, v, *, n_seq, seq_len):
    naw = q.shape[1]
    spec = pl.BlockSpec((seq_len, naw), lambda b: (b, 0))
    return pl.pallas_call(
        functools.partial(_ctx_attn_kernel, scale=HEAD_DIM ** -0.5),
        grid=(n_seq,),
        in_specs=[spec, spec, spec],
        out_specs=spec,
        out_shape=jax.ShapeDtypeStruct((n_seq * seq_len, naw), BF16),
        compiler_params=_cparams(1),
        name="ctx_attn",
    )(q, k, v)


def _na_bias_kernel(rpb_ref, o_ref, t_s):
    inv_scale = HEAD_DIM ** 0.5
    h = pl.program_id(0)
    n_dr = 2 * NA_WIN_R - 1
    n_dc = 2 * NA_WIN_C - 1
    w = GRID_W
    qc = lax.broadcasted_iota(jnp.int32, (w, w), 0)
    kc = lax.broadcasted_iota(jnp.int32, (w, w), 1)
    dc = kc - qc + (NA_WIN_C - 1)
    c0 = jnp.clip(qc - NA_WIN_C // 2, 0, w - NA_WIN_C)
    col_ok = (kc >= c0) & (kc < c0 + NA_WIN_C)
    for dr in range(n_dr):
        def pick(j, t, dr=dr):
            return jnp.where(dc == j, rpb_ref[h * (n_dr * n_dc) + dr * n_dc + j] * inv_scale, t)
        t = lax.fori_loop(0, n_dc, pick, jnp.zeros((w, w), F32))
        t_s[dr] = jnp.where(col_ok, t, NEG_INF)
    neg = jnp.full((w, w), NEG_INF, F32)
    patterns = [(0, lambda i: 0), (NA_WIN_R // 2, lambda i: i), (NA_WIN_R, lambda i: NA_QROWS)]
    for pat, (r_rel, r0_rel) in enumerate(patterns):
        for i in range(NA_QROWS):
            for kr in range(NA_KROWS):
                dr = kr - i - r_rel + (NA_WIN_R - 1)
                valid = r0_rel(i) <= kr < r0_rel(i) + NA_WIN_R
                o_ref[pat, i * w:(i + 1) * w, kr * w:(kr + 1) * w] = t_s[dr] if valid else neg


def _na_bias_call(rpb):
    n_h = rpb.shape[0]
    nq = NA_QROWS * GRID_W
    nk = NA_KROWS * GRID_W
    return pl.pallas_call(
        _na_bias_kernel,
        grid=(n_h,),
        in_specs=[pl.BlockSpec(memory_space=pltpu.SMEM)],
        out_specs=pl.BlockSpec((None, 3, nq, nk), lambda h: (h, 0, 0, 0)),
        out_shape=jax.ShapeDtypeStruct((n_h, 3, nq, nk), F32),
        scratch_shapes=[pltpu.VMEM((2 * NA_WIN_R - 1, GRID_W, GRID_W), F32)],
        compiler_params=_cparams(1),
        name="na_bias",
    )(rpb.reshape(-1))


def _na_kernel(q_ref, k_ref, v_ref, kc_ref, vc_ref, bb_ref, o_ref, kcb_s, vcb_s, s_s, p_s, l_s,
               *, rows, scale):
    w = GRID_W
    nq = NA_QROWS * w
    nk = NA_KROWS * w
    n_blk = rows // NA_QROWS
    kcb_s[...] = kc_ref[...].astype(BF16)
    vcb_s[...] = vc_ref[...].astype(BF16)

    def q_rows(blk):
        return pl.ds(pl.multiple_of(jnp.int32(blk) * nq, nq), nq)

    def k_rows(blk):
        k_base = jnp.clip(jnp.int32(blk) * NA_QROWS - NA_WIN_R // 2, 0, rows - NA_KROWS)
        return pl.ds(pl.multiple_of(k_base * w, nq), nk)

    def scores(blk, slot):
        pat = jnp.where(blk == 0, 0, jnp.where(blk == n_blk - 1, 2, 1))
        q = q_ref[q_rows(blk), :]
        s_s[slot, :, 0:nk] = _dot_nt(q, k_ref[k_rows(blk), :]) + bb_ref[pat]
        s_s[slot, :, nk:] = _dot_nt(q, kcb_s[...])

    def softmax(slot):
        s = s_s[slot]
        p = jnp.exp2((s - jnp.max(s, axis=-1, keepdims=True)) * (scale * math.log2(math.e)))
        l_s[slot] = jnp.sum(p, axis=-1, keepdims=True)
        p_s[slot] = p.astype(p_s.dtype)

    def values(blk, slot):
        o = _dot(p_s[slot, :, 0:nk], v_ref[k_rows(blk), :]) + _dot(p_s[slot, :, nk:], vcb_s[...])
        o_ref[q_rows(blk), :] = (o / l_s[slot]).astype(o_ref.dtype)

    scores(0, 0)
    softmax(0)
    scores(1, 1)

    def body(jj, carry):
        values(2 * jj, 0)
        softmax(1)
        scores(2 * jj + 2, 0)
        values(2 * jj + 1, 1)
        softmax(0)
        scores(2 * jj + 3, 1)
        return carry

    lax.fori_loop(0, (n_blk - 2) // 2, body, 0)
    values(n_blk - 2, 0)
    softmax(1)
    values(n_blk - 1, 1)


def _na_call(q, k, v, cache_k, cache_v, bias, *, layer_e, n_b, seq_len, row_blk0):
    naw = q.shape[1]
    n_h = naw // HEAD_DIM
    past = cache_k.shape[2]
    rows = seq_len // GRID_W
    assert rows % (2 * NA_QROWS) == 0 and rows >= NA_KROWS
    nq, nk = NA_QROWS * GRID_W, NA_KROWS * GRID_W
    tok =pl.BlockSpec((seq_len, HEAD_DIM), lambda b, h: (row_blk0 + b, h))
    ctx = pl.BlockSpec((None, None, past, HEAD_DIM), lambda b, h: (b, layer_e, 0, h))
    return pl.pallas_call(
        functools.partial(_na_kernel, rows=rows, scale=HEAD_DIM ** -0.5),
        grid=(n_b, n_h),
        in_specs=[tok, tok, tok, ctx, ctx,
                  pl.BlockSpec((None,) + bias.shape[1:], lambda b, h: (h, 0, 0, 0))],
        out_specs=pl.BlockSpec((seq_len, HEAD_DIM), lambda b, h: (b, h)),
        out_shape=jax.ShapeDtypeStruct((n_b * seq_len, naw), BF16),
        scratch_shapes=[pltpu.VMEM((past, HEAD_DIM), BF16), pltpu.VMEM((past, HEAD_DIM), BF16),
                        pltpu.VMEM((2, nq, nk + past), F32), pltpu.VMEM((2, nq, nk + past), BF16),
                        pltpu.VMEM((2, nq, 1), F32)],
        compiler_params=_cparams(2),
        name="na_attn",
    )(q, k, v, cache_k, cache_v, bias)


def _about_kernel(*refs, n_x, n_p_tiles, nc, rc):
    g_ref, ap_ref, as_ref = refs[:3]
    x_refs = refs[3:3 + n_x]
    mod_ref, gw_ref, gb_ref, w_ref, o_ref = refs[3 + n_x:]
    s5w = g_ref.shape[1]
    tm, d = o_ref.shape

    def run(part):
        a_ref, x_ref = part
        for r0, rn in _row_chunks(tm, rc):
            rows = slice(r0, r0 + rn)
            g = g_ref[rows, :]
            gl = _dot(g.astype(BF16), gw_ref[...]) + gb_ref[...]
            s5o = (g * jax.nn.sigmoid(gl)).astype(BF16)
            a = a_ref[rows, :]
            for n0 in range(0, d, nc):
                y = _dot(s5o, w_ref[0:s5w, n0:n0 + nc]) + _dot(a, w_ref[s5w:, n0:n0 + nc])
                o_ref[rows, n0:n0 + nc] = x_ref[rows, n0:n0 + nc] + mod_ref[2:3, n0:n0 + nc] * y

    _for_part(pl.program_id(0), n_p_tiles, [(ap_ref, x_refs[0]), (as_ref, x_refs[-1])], run)


def _about_call(g, attn_p, attn_s, xs, modt, glu_w, glu_b, w_out, *, layer, tm, n_p_tiles):
    m = sum(x.shape[0] for x in xs)
    d = xs[0].shape[1]
    s5w = g.shape[1]
    naw = attn_p.shape[1]
    nc = min(512, d)
    row = lambda i: (i, 0)
    return pl.pallas_call(
        functools.partial(_about_kernel, n_x=len(xs), n_p_tiles=n_p_tiles, nc=nc, rc=min(ROW_CHUNK, tm)),
        grid=(m // tm,),
        in_specs=[pl.BlockSpec((tm, s5w), row)] + _x_specs(2, tm, naw, n_p_tiles)
        + _x_specs(len(xs), tm, d, n_p_tiles) + [
            pl.BlockSpec((None, 8, d), lambda i: (i, 0, 0)),
            _const_spec((None, s5w, s5w), lambda i: (layer, 0, 0)),
            _const_spec((1, s5w), lambda i: (0, 0)),
            _const_spec((None, s5w + naw, d), lambda i: (layer, 0, 0)),
        ],
        out_specs=pl.BlockSpec((tm, d), row),
        out_shape=jax.ShapeDtypeStruct((m, d), F32),
        compiler_params=_cparams(1),
        name="ab_out_proj",
    )(g, attn_p, attn_s, *xs, modt, glu_w, glu_b, w_out)


def _mlp_kernel(x_ref, mod_ref, g_ref, w1_ref, w2_ref, o_ref, h_s, a_s, *, rc, nc1, nc2):
    tm, d = o_ref.shape
    tf = w1_ref.shape[1]
    ta = a_s.shape[1]

    def chunk_dots(rows):
        h = h_s[rows, :]
        for f0 in range(0, tf, ta):
            for c0 in range(0, ta, nc1):
                a = jnp.maximum(_dot(h, w1_ref[:, f0 + c0:f0 + c0 + nc1]), 0.0)
                a_s[rows, c0:c0 + nc1] = (a * a).astype(a_s.dtype)
            a = a_s[rows, :]
            for n0 in range(0, d, nc2):
                o_ref[rows, n0:n0 + nc2] += mod_ref[5:6, n0:n0 + nc2] * _dot(a, w2_ref[f0:f0 + ta, n0:n0 + nc2])

    @pl.when(pl.program_id(1) == 0)
    def _():
        g, shift, scale = g_ref[...], mod_ref[3:4, :], mod_ref[4:5, :]
        for r0, rn in _row_chunks(tm, rc):
            rows = slice(r0, r0 + rn)
            x = x_ref[rows, :]
            h_s[rows, :] = _modulate(x, g, shift, scale).astype(h_s.dtype)
            o_ref[rows, :] = x
            chunk_dots(rows)

    @pl.when(pl.program_id(1) != 0)
    def _():
        chunk_dots(slice(0, tm))


def _mlp_call(x, modt, g, w1, w2, *, layer, tm, tf, tile0=0, n_tiles=None):
    d = x.shape[1]
    n_tiles = x.shape[0] // tm if n_tiles is None else n_tiles
    m = n_tiles * tm
    d_ff = w1.shape[2]
    rc = min(2 * ROW_CHUNK, tm)
    return pl.pallas_call(
        functools.partial(_mlp_kernel, rc=rc, nc1=min(256, tf), nc2=min(512, d)),
        grid=(n_tiles, d_ff // tf),
        in_specs=[
            pl.BlockSpec((tm, d), lambda i, f: (i + tile0, 0)),
            pl.BlockSpec((None, 8, d), lambda i, f: (i + tile0, 0, 0)),
            _const_spec((1, d), lambda i, f: (0, 0)),
            pl.BlockSpec((None, d, tf), lambda i, f: (layer, 0, f)),
            pl.BlockSpec((None, tf, d), lambda i, f: (layer, f, 0)),
        ],
        out_specs=pl.BlockSpec((tm, d), lambda i, f: (i, 0)),
        out_shape=jax.ShapeDtypeStruct((m, d), F32),
        scratch_shapes=[pltpu.VMEM((tm, d), BF16), pltpu.VMEM((tm, min(512, tf)), BF16)],
        compiler_params=_cparams(2),
        name="mlp",
    )(x, modt, g, w1, w2)


def _convin_kernel(x_ref, mod_ref, g_ref, w_ref, gb_ref, z_ref, h_s, *, rc, nc):
    tm, d = gb_ref.shape
    g, shift, scale = g_ref[...], mod_ref[0:1, :], mod_ref[1:2, :]
    for r0, rn in _row_chunks(tm, rc):
        rows = slice(r0, r0 + rn)
        h_s[rows, :] = _modulate(x_ref[rows, :], g, shift, scale).astype(h_s.dtype)
        h = h_s[rows, :]
        for n0 in range(0, d, nc):
            gb_ref[rows, n0:n0 + nc] = _dot(h, w_ref[:, n0:n0 + nc]).astype(gb_ref.dtype)
            z_ref[rows, n0:n0 + nc] = (_dot(h, w_ref[:, d + n0:d + n0 + nc])
                                       * _dot(h, w_ref[:, 2 * d + n0:2 * d + n0 + nc])).astype(z_ref.dtype)


def _convin_call(x, modt, g, w_in, *, layer, tm):
    m, d = x.shape
    row = lambda i: (i, 0)
    return pl.pallas_call(
        functools.partial(_convin_kernel, rc=min(ROW_CHUNK, tm), nc=min(512, d)),
        grid=(m // tm,),
        in_specs=[
            pl.BlockSpec((tm, d), row),
            pl.BlockSpec((None, 8, d), lambda i: (i, 0, 0)),
            _const_spec((1, d), lambda i: (0, 0)),
            _const_spec((None, d, 3 * d), lambda i: (layer, 0, 0)),
        ],
        out_specs=[pl.BlockSpec((tm, d), row), pl.BlockSpec((tm, d), row)],
        out_shape=[jax.ShapeDtypeStruct((m, d), BF16), jax.ShapeDtypeStruct((m, d), BF16)],
        scratch_shapes=[pltpu.VMEM((tm, d), BF16)],
        compiler_params=_cparams(1),
        name="conv_in_proj",
    )(x, modt, g, w_in)


def _convout_kernel(gb_ref, z_ref, zp_ref, zn_ref, x_ref, mod_ref, cw_ref, cb_ref, w_ref, o_ref, t_s,
                    *, n_p_tiles, lp, ls, rc, cc, nc):
    i = pl.program_id(0)
    tm, d = z_ref.shape
    halo = zp_ref.shape[0]
    seq_mask = jnp.where(i < n_p_tiles, lp - 1, ls - 1)
    chunk_aligned = lp % rc == 0 and ls % rc == 0
    for r0, rn in _row_chunks(tm, rc):
        rows = slice(r0, r0 + rn)
        ridx = lax.broadcasted_iota(jnp.int32, (rn, 1), 0)
        if chunk_aligned:
            starts = ((i * tm + r0) & seq_mask) == 0
            ends = ((i * tm + r0 + rn) & seq_mask) == 0
        else:
            pos = (i * tm + r0 + ridx) & seq_mask
            is_start = pos == 0
            is_end = pos == seq_mask
        for c0 in range(0, d, cc):
            cs = slice(c0, c0 + cc)
            z = z_ref[rows, cs].astype(F32)
            before = (zp_ref[halo - 1:halo, cs] if r0 == 0 else z_ref[r0 - 1:r0, cs]).astype(F32)
            after = (zn_ref[0:1, cs] if r0 + rn == tm else z_ref[r0 + rn:r0 + rn + 1, cs]).astype(F32)
            if chunk_aligned:
                before = jnp.where(starts, 0.0, before)
                after = jnp.where(ends, 0.0, after)
            z_prev = jnp.where(ridx == 0, before, pltpu.roll(z, 1, 0))
            z_next = jnp.where(ridx == rn - 1, after, pltpu.roll(z, rn - 1, 0))
            if not chunk_aligned:
                z_prev = jnp.where(is_start, 0.0, z_prev)
                z_next = jnp.where(is_end, 0.0, z_next)
            conv = z_prev * cw_ref[0:1, cs] + z * cw_ref[1:2, cs] + z_next * cw_ref[2:3, cs] + cb_ref[:, cs]
            t_s[rows, cs] = (gb_ref[rows, cs].astype(F32) * conv).astype(t_s.dtype)
        t = t_s[rows, :]
        for n0 in range(0, d, nc):
            o_ref[rows, n0:n0 + nc] = (x_ref[rows, n0:n0 + nc]
                                       + mod_ref[2:3, n0:n0 + nc] * _dot(t, w_ref[:, n0:n0 + nc]))


def _convout_call(gb, z, x, modt, conv_w, conv_b, w_out, *, layer, tm, n_p_tiles, lp, ls):
    m, d = x.shape
    halo = 16
    assert tm % halo == 0 and lp & (lp - 1) == 0 and ls & (ls - 1) == 0
    hb = tm // halo
    last = m // halo - 1
    row = lambda i: (i, 0)
    return pl.pallas_call(
        functools.partial(_convout_kernel, n_p_tiles=n_p_tiles, lp=lp, ls=ls, rc=min(ROW_CHUNK, tm),
                          cc=min(256, d), nc=min(512, d)),
        grid=(m // tm,),
        in_specs=[
            pl.BlockSpec((tm, d), row),
            pl.BlockSpec((tm, d), row),
            pl.BlockSpec((halo, d), lambda i: (jnp.maximum(i * hb - 1, 0), 0)),
            pl.BlockSpec((halo, d), lambda i: (jnp.minimum((i + 1) * hb, last), 0)),
            pl.BlockSpec((tm, d), row),
            pl.BlockSpec((None, 8, d), lambda i: (i, 0, 0)),
            _const_spec((8, d), lambda i: (0, 0)),
            _const_spec((1, d), lambda i: (0, 0)),
            _const_spec((None, d, d), lambda i: (layer, 0, 0)),
        ],
        out_specs=pl.BlockSpec((tm, d), row),
        out_shape=jax.ShapeDtypeStruct((m, d), F32),
        scratch_shapes=[pltpu.VMEM((tm, d), BF16)],
        compiler_params=_cparams(1),
        name="conv_out_proj",
    )(gb, z, z, z, x, modt, conv_w, conv_b, w_out)


def _tile_rows(m_p, m_s, l_s, n_b, tm):
    assert m_p % tm == 0 and l_s % tm == 0
    return np.concatenate([np.full(m_p // tm, n_b), np.repeat(np.arange(n_b), l_s // tm)])


def _pick_tile(pref, m_p, l_s):
    tm = pref
    while m_p % tm or l_s % tm:
        tm //= 2
    return tm


def kernel(x_prompt, x_sample, c, cache_k, cache_v, state_ssm_re, state_ssm_im, c_ctx, ada_w, ada_b, norm1_g, norm2_g, ab_w_in, ab_w_out, s5_lam_re, s5_lam_im, s5_log_dt, s5_b_re, s5_b_im, s5_c_re, s5_c_im, s5_d, s5_glu_w, s5_glu_b, q_norm_g, k_norm_g, na_rpb, conv_w_in, conv_w, conv_b, conv_w_out, mlp_w1, mlp_w2):
    n_bp, l_p, d = x_prompt.shape
    n_bs, l_s, _ = x_sample.shape
    depth = ada_w.shape[0]
    m_p, m_s = n_bp * l_p, n_bs * l_s
    m = m_p + m_s
    n_g, n_state = s5_lam_re.shape[2], s5_lam_re.shape[3]
    s5w = n_g * S5_GROUP_CH
    naw = (ab_w_in.shape[2] - s5w) // 3
    n_h = naw // HEAD_DIM
    past = cache_k.shape[2]
    d_ff = mlp_w1.shape[2]
    assert m_p % l_s == 0, "latent sequences must start on a sequence-length row block"
    assert l_p % S5_CHUNK == 0 and l_s % S5_CHUNK == 0
    c_p, c_s = l_p // S5_CHUNK, l_s // S5_CHUNK
    assert c_p & (c_p - 1) == 0 and c_s & (c_s - 1) == 0

    xs = [x_prompt.reshape(m_p, d), x_sample.reshape(m_s, d)]

    n_rows = -(-(n_bs + 1) // 16) * 16
    cvec = jnp.concatenate([c, c_ctx[None], jnp.zeros((n_rows - n_bs - 1, d), F32)], axis=0)
    mod = _ada_call(cvec, ada_w, ada_b)

    def mod_tiles(layer, tm):
        t = mod[layer][_tile_rows(m_p, m_s, l_s, n_bs, tm)].reshape(m // tm, 6, d)
        return jnp.pad(t, ((0, 0), (0, 2), (0, 0)))

    tm_big = _pick_tile(1024, m_p, l_s)
    tm_mid = _pick_tile(512, m_p, l_s)
    tf = min(1024, d_ff)

    n_pw = max(int(math.log2(c_s)), 1)
    w_u, w_y, pw = _s5prep_call(s5_lam_re, s5_lam_im, s5_log_dt, s5_b_re, s5_b_im, s5_c_re, s5_c_im,
                                s5_d, n_pw)
    n_h0 = -(-n_bs // 8) * 8
    h0_all = jnp.stack([state_ssm_re, state_ssm_im], axis=2)
    h0_all = h0_all.transpose(1, 4, 0, 2, 3, 5).reshape(-1, n_g, n_bs, 4 * n_state)
    h0_all = jnp.pad(h0_all, ((0, 0), (0, 0), (0, n_h0 - n_bs), (0, 0)))
    cache_k4 = cache_k.reshape(n_bs, -1, past, naw)
    cache_v4 = cache_v.reshape(n_bs, -1, past, naw)
    ab_w_in_b, ab_w_out_b, glu_w_b = ab_w_in.astype(BF16), ab_w_out.astype(BF16), s5_glu_w.astype(BF16)
    conv_w_in_b, conv_w_out_b = conv_w_in.astype(BF16), conv_w_out.astype(BF16)
    mlp_w1_b, mlp_w2_b = mlp_w1.astype(BF16), mlp_w2.astype(BF16)

    caches = None
    new_re, new_im = [], []
    for layer in range(depth):
        g1 = norm1_g[layer].reshape(1, d)
        g2 = norm2_g[layer].reshape(1, d)
        if layer % 2 == 0:
            e = layer // 2
            u, q, k, v, *caches = _inproj_call(
                xs, mod_tiles(layer, tm_mid), g1, ab_w_in_b,
                q_norm_g[e].reshape(1, HEAD_DIM), k_norm_g[e].reshape(1, HEAD_DIM), caches,
                layer=e, n_e=ab_w_in.shape[0], l_p=l_p, tm=tm_mid, s5w=s5w, naw=naw, n_p_tiles=m_p // tm_mid)
            g_tok, z_all = _s5_call(u, w_u[e], w_y[e], pw[e], h0_all[e],
                                    l_tile=l_s, n_p_tiles=m_p // l_s, cp=c_p, cs=c_s)
            z_p = z_all.reshape(n_g, n_bp, c_p, 4 * n_state)
            fin = z_p[:, :, c_p - 1].reshape(n_g, n_bp, 2, 2, n_state)
            new_re.append(fin[:, :, 0].transpose(1, 2, 0, 3))
            new_im.append(fin[:, :, 1].transpose(1, 2, 0, 3))
            attn_p = _ctx_attn_call(q, k, v, n_seq=n_bp, seq_len=l_p)
            bias = _na_bias_call(na_rpb[e])
            attn_s = _na_call(q, k, v, cache_k4, cache_v4, bias, layer_e=e, n_b=n_bs, seq_len=l_s,
                              row_blk0=m_p // l_s)
            x = _about_call(g_tok, attn_p, attn_s, xs, mod_tiles(layer, tm_mid), glu_w_b,
                            s5_glu_b[e].reshape(1, s5w), ab_w_out_b, layer=e, tm=tm_mid,
                            n_p_tiles=m_p // tm_mid)
        else:
            x = xs[0]
            o = layer // 2
            gb, z = _convin_call(x, mod_tiles(layer, tm_mid), g1, conv_w_in_b, layer=o, tm=tm_mid)
            cw8 = jnp.pad(conv_w[o], ((0, 5), (0, 0)))
            x = _convout_call(gb, z, x, mod_tiles(layer, tm_mid), cw8, conv_b[o].reshape(1, d),
                              conv_w_out_b, layer=o, tm=tm_mid, n_p_tiles=m_p // tm_mid,
                              lp=l_p, ls=l_s)
        mlp = functools.partial(_mlp_call, x, mod_tiles(layer, tm_big), g2, mlp_w1_b, mlp_w2_b,
                                layer=layer, tm=tm_big, tf=tf)
        if layer < depth - 1:
            xs = [mlp()]
        else:
            y_prompt = mlp(tile0=0, n_tiles=m_p // tm_big).reshape(n_bp, l_p, d)
            y_sample = mlp(tile0=m_p // tm_big, n_tiles=m_s // tm_big).reshape(n_bs, l_s, d)

    new_k, new_v = (t.reshape(n_bp, -1, l_p, n_h, HEAD_DIM) for t in caches)
    return (y_prompt, y_sample, new_k, new_v, jnp.stack(new_re, axis=1), jnp.stack(new_im, axis=1))
```

```python
import functools
import math

import numpy as np
import jax
import jax.numpy as jnp
from jax import lax
from jax.experimental import pallas as pl
from jax.experimental.pallas import tpu as pltpu

F32 = jnp.float32
BF16 = jnp.bfloat16

NORM_EPS = 1e-6
NEG_INF = -1e30

S5_GROUP_CH = 16
HEAD_DIM = 128
GRID_W = 64
NA_WIN_R = 8
NA_WIN_C = 16
S5_CHUNK = 16
NA_QROWS = 4
NA_KROWS = NA_QROWS + NA_WIN_R
ROW_CHUNK = 256

LANES = 128
VMEM_LIMIT_BYTES = 62 * 1024 * 1024


def _cparams(n_axes):
    return pltpu.CompilerParams(dimension_semantics=("arbitrary",) * n_axes,
                                vmem_limit_bytes=VMEM_LIMIT_BYTES)


def _const_spec(shape, index_map):
    return pl.BlockSpec(shape, index_map, pipeline_mode=pl.Buffered(1))


def _dot(a, b):
    return jnp.dot(a, b, preferred_element_type=F32)


def _dot_nt(a, b):
    return lax.dot_general(a, b, (((1,), (1,)), ((), ())), preferred_element_type=F32)


def _dot_split(a, b):
    a_hi = a.astype(BF16)
    a_lo = (a - a_hi.astype(F32)).astype(BF16)
    b_hi = b.astype(BF16)
    b_lo = (b - b_hi.astype(F32)).astype(BF16)
    return _dot(a_hi, b_hi) + _dot(a_hi, b_lo) + _dot(a_lo, b_hi)


def _modulate(x, g, shift, scale):
    ms = jnp.mean(x * x, axis=-1, keepdims=True)
    y = x * lax.rsqrt(ms + NORM_EPS) * g
    return y * (1.0 + scale) + shift


def _ada_kernel(c_ref, w_ref, b_ref, o_ref, *, nc):
    cv = c_ref[...]
    sc = (cv * jax.nn.sigmoid(cv)).astype(BF16)
    tn = w_ref.shape[1]
    for n0 in range(0, tn, nc):
        w = w_ref[:, n0:n0 + nc].astype(BF16)
        o_ref[:, n0:n0 + nc] = _dot(sc, w) + b_ref[:, n0:n0 + nc]


def _ada_call(cvec, ada_w, ada_b):
    depth, d, n6 = ada_w.shape
    rows = cvec.shape[0]
    tn = 1536 if n6 % 1536 == 0 else n6
    nc = 512 if tn % 512 == 0 else tn
    return pl.pallas_call(
        functools.partial(_ada_kernel, nc=nc),
        grid=(depth, n6 // tn),
        in_specs=[
            pl.BlockSpec((rows, d), lambda l, j: (0, 0)),
            pl.BlockSpec((None, d, tn), lambda l, j: (l, 0, j)),
            pl.BlockSpec((None, 1, tn), lambda l, j: (l, 0, j)),
        ],
        out_specs=pl.BlockSpec((None, rows, tn), lambda l, j: (l, 0, j)),
        out_shape=jax.ShapeDtypeStruct((depth, rows, n6), F32),
        compiler_params=_cparams(2),
        name="ada_params",
    )(cvec, ada_w, ada_b.reshape(depth, 1, n6))


def _x_specs(n_x, tm, d, n_p_tiles):
    if n_x == 1:
        return [pl.BlockSpec((tm, d), lambda i: (i, 0))]
    return [pl.BlockSpec((tm, d), lambda i: (jnp.minimum(i, n_p_tiles - 1), 0)),
            pl.BlockSpec((tm, d), lambda i: (jnp.maximum(i - n_p_tiles, 0), 0))]


def _row_chunks(tm, rc):
    return [(r0, min(rc, tm - r0)) for r0 in range(0, tm, rc)]


def _for_part(i, n_p_tiles, refs, fn):
    if len(refs) == 1:
        fn(refs[0])
    else:
        pl.when(i < n_p_tiles)(lambda: fn(refs[0]))
        pl.when(i >= n_p_tiles)(lambda: fn(refs[1]))


def _inproj_kernel(*refs, n_x, n_alias, n_p_tiles, l_p, s5w, naw, rc):
    x_refs = refs[:n_x]
    mod_ref, g_ref, w_ref, qg_ref, kg_ref = refs[n_x:n_x + 5]
    u_ref, q_ref, k_ref, v_ref, ck_ref, cv_ref, h_s = refs[n_x + 5 + n_alias:]
    i = pl.program_id(0)
    tm = u_ref.shape[0]
    n_heads = naw // HEAD_DIM

    owns_cache = n_alias == 0

    def cache_store(c_ref, r0, col0, val):
        head = col0 // HEAD_DIM
        t0 = r0
        while t0 < r0 + val.shape[0]:
            piece = min(r0 + val.shape[0], (t0 // l_p + 1) * l_p) - t0
            rows = pl.ds((t0 % l_p) * n_heads + head, piece, stride=n_heads)
            if owns_cache:
                c_ref[t0 // l_p, 0, rows, :] = val[t0 - r0:t0 - r0 + piece]
            else:
                c_ref[t0 // l_p, rows, :] = val[t0 - r0:t0 - r0 + piece]
            t0 += piece
    hd = HEAD_DIM
    cw = min(4 * hd, naw)
    g, shift, scale = g_ref[...], mod_ref[0:1, :], mod_ref[1:2, :]
    qg = qg_ref[...]
    kg = kg_ref[...]

    def head_norm(t, gain):
        return t * lax.rsqrt(jnp.mean(t * t, axis=-1, keepdims=True) + NORM_EPS) * gain

    def maybe_cache(prompt, fn):
        if prompt:
            fn()

    def run(x_ref, prompt):
        if prompt and owns_cache and ck_ref.shape[1] > 1:
            for c_ref in (ck_ref, cv_ref):
                c_ref[:, 1:, :, :] = jnp.zeros((c_ref.shape[0], c_ref.shape[1] - 1) + c_ref.shape[2:], c_ref.dtype)
        for r0, rn in _row_chunks(tm, rc):
            rows = slice(r0, r0 + rn)
            h_s[rows, :] = _modulate(x_ref[rows, :], g, shift, scale).astype(h_s.dtype)
            h = h_s[rows, :]
            u_ref[rows, :] = _dot(h, w_ref[:, 0:s5w]).astype(u_ref.dtype)
            for c0 in range(0, naw, cw):
                qc = _dot(h, w_ref[:, s5w + c0:s5w + c0 + cw])
                for j in range(cw // hd):
                    q_ref[rows, c0 + j * hd:c0 + (j + 1) * hd] = head_norm(
                        qc[:, j * hd:(j + 1) * hd], qg).astype(q_ref.dtype)
                kc = _dot(h, w_ref[:, s5w + naw + c0:s5w + naw + c0 + cw])
                for j in range(cw // hd):
                    kh = head_norm(kc[:, j * hd:(j + 1) * hd], kg)
                    k_ref[rows, c0 + j * hd:c0 + (j + 1) * hd] = kh.astype(k_ref.dtype)
                    maybe_cache(prompt, functools.partial(cache_store, ck_ref, r0, c0 + j * hd, kh))
                vc = _dot(h, w_ref[:, s5w + 2 * naw + c0:s5w + 2 * naw + c0 + cw])
                v_ref[rows, c0:c0 + cw] = vc.astype(v_ref.dtype)
                for j in range(cw // hd):
                    maybe_cache(prompt, functools.partial(cache_store, cv_ref, r0, c0 + j * hd,
                                                          vc[:, j * hd:(j + 1) * hd]))

    pl.when(i < n_p_tiles)(lambda: run(x_refs[0], True))
    pl.when(i >= n_p_tiles)(lambda: run(x_refs[-1], False))


def _inproj_call(xs, modt, g, w_in, qg, kg, caches, *, layer, n_e, l_p, tm, s5w, naw, n_p_tiles):
    m = sum(x.shape[0] for x in xs)
    d = xs[0].shape[1]
    n_in = w_in.shape[2]
    rc = min(ROW_CHUNK, tm)
    n_heads = naw // HEAD_DIM
    assert tm % l_p == 0 and (rc % l_p == 0 or l_p % rc == 0)
    seqs = tm // l_p
    cache_shape = jax.ShapeDtypeStruct((n_p_tiles * seqs, n_e, l_p * n_heads, HEAD_DIM), F32)
    n_alias = 0 if caches is None else 2
    if n_alias == 0:
        assert layer == 0
        cache_spec = pl.BlockSpec((seqs, n_e, l_p * n_heads, HEAD_DIM),
                                  lambda i: (jnp.minimum(i, n_p_tiles - 1), 0, 0, 0))
    else:
        cache_spec = pl.BlockSpec((seqs, None, l_p * n_heads, HEAD_DIM),
                                  lambda i: (jnp.minimum(i, n_p_tiles - 1), layer, 0, 0))
    alias_args = [] if caches is None else list(caches)
    n_in_args = len(xs) + 5
    row = lambda i: (i, 0)
    return pl.pallas_call(
        functools.partial(_inproj_kernel, n_x=len(xs), n_alias=n_alias, n_p_tiles=n_p_tiles, l_p=l_p,
                          s5w=s5w, naw=naw, rc=rc),
        grid=(m // tm,),
        in_specs=_x_specs(len(xs), tm, d, n_p_tiles) + [
            pl.BlockSpec((None, 8, d), lambda i: (i, 0, 0)),
            _const_spec((1, d), lambda i: (0, 0)),
            _const_spec((None, d, n_in), lambda i: (layer, 0, 0)),
            _const_spec((1, HEAD_DIM), lambda i: (0, 0)),
            _const_spec((1, HEAD_DIM), lambda i: (0, 0)),
        ] + [pl.BlockSpec(memory_space=pl.ANY)] * n_alias,
        out_specs=[
            pl.BlockSpec((tm, s5w), row),
            pl.BlockSpec((tm, naw), row),
            pl.BlockSpec((tm, naw), row),
            pl.BlockSpec((tm, naw), row),
            cache_spec, cache_spec,
        ],
        out_shape=[
            jax.ShapeDtypeStruct((m, s5w), F32),
            jax.ShapeDtypeStruct((m, naw), BF16),
            jax.ShapeDtypeStruct((m, naw), BF16),
            jax.ShapeDtypeStruct((m, naw), BF16),
            cache_shape, cache_shape,
        ],
        input_output_aliases={n_in_args + a: 4 + a for a in range(n_alias)},
        scratch_shapes=[pltpu.VMEM((tm, d), BF16)],
        compiler_params=_cparams(1),
        name="ab_in_proj",
    )(*xs, modt, g, w_in, qg, kg, *alias_args)


def _s5prep_kernel(row_ref, row2_ref, col_ref, btr_ref, bti_ref, cer_ref, cei_ref, d_ref,
                   wu_ref, wy_ref, pw_ref, *, n_state, n_pw):
    t_chunk = S5_CHUNK
    width = t_chunk * S5_GROUP_CH
    p = n_state
    lg_ch = int(math.log2(S5_GROUP_CH))
    r_i = lax.broadcasted_iota(jnp.int32, (width, 1), 0)
    c_i = lax.broadcasted_iota(jnp.int32, (1, width), 1)
    s_i = r_i >> lg_ch
    t_i = c_i >> lg_ch
    s16 = lax.broadcasted_iota(jnp.int32, (t_chunk, 1), 0)
    n_pow = -(-(t_chunk + 1) // 8) * 8

    def cmul(ar, ai, br, bi):
        return ar * br - ai * bi, ar * bi + ai * br

    acc = jnp.zeros((width, width), F32)
    for d in range(2):
        lr = row_ref[d, 0:1, 0:p]
        li = row_ref[d, 1:2, 0:p]
        dt = jnp.exp(row_ref[d, 2:3, 0:p])
        ar = lr * dt
        ai = li * dt
        er = jnp.exp(ar)
        abr = er * jnp.cos(ai)
        abi = er * jnp.sin(ai)
        den = lr * lr + li * li
        nr = abr - 1.0
        f_re = (nr * lr + abi * li) / den
        f_im = (abi * lr - nr * li) / den
        bb_re, bb_im = cmul(f_re, f_im, btr_ref[d], bti_ref[d])

        lrc = col_ref[d, :, 0:1]
        lic = col_ref[d, :, 1:2]
        dtc = jnp.exp(col_ref[d, :, 2:3])
        arc = lrc * dtc
        aic = lic * dtc
        ce_re = cer_ref[d]
        ce_im = cei_ref[d]

        k_r = lax.broadcasted_iota(jnp.int32, (n_pow, 1), 0).astype(F32)
        k_c = lax.broadcasted_iota(jnp.int32, (1, LANES), 1).astype(F32)
        mag_r = jnp.exp(k_r * ar)
        tr_re, tr_im = mag_r * jnp.cos(k_r * ai), mag_r * jnp.sin(k_r * ai)
        mag_c = jnp.exp(arc * k_c)
        tc_re, tc_im = mag_c * jnp.cos(aic * k_c), mag_c * jnp.sin(aic * k_c)

        def pow_row(e, e_max):
            re = jnp.zeros((t_chunk, p), F32)
            im = jnp.zeros((t_chunk, p), F32)
            for kk in range(e_max + 1):
                hit = e == kk
                re = jnp.where(hit, tr_re[kk:kk + 1, :], re)
                im = jnp.where(hit, tr_im[kk:kk + 1, :], im)
            rep = lambda t: jnp.broadcast_to(t[:, None, :], (t_chunk, S5_GROUP_CH, p)).reshape(width, p)
            return rep(re), rep(im)

        def pow_col(e, e_max):
            re = jnp.zeros((p, width), F32)
            im = jnp.zeros((p, width), F32)
            for kk in range(e_max + 1):
                hit = e == kk
                re = jnp.where(hit, tc_re[:, kk:kk + 1], re)
                im = jnp.where(hit, tc_im[:, kk:kk + 1], im)
            return re, im

        terms = [(jnp.zeros_like(s16), jnp.zeros_like(t_i), s_i == t_i, 0)]
        blk = 2
        while blk <= t_chunk:
            lb = int(math.log2(blk))
            half = blk // 2
            mid = lambda idx: ((idx >> lb) << lb) + half
            same = (s_i >> lb) == (t_i >> lb)
            if d == 0:
                e_s, ok_s = mid(s16) - s16, s_i < mid(s_i)
                e_t, ok_t = t_i - mid(t_i), t_i >= mid(t_i)
            else:
                e_s, ok_s = s16 - mid(s16), s_i >= mid(s_i)
                e_t, ok_t = mid(t_i) - t_i, t_i < mid(t_i)
            terms.append((jnp.maximum(e_s, 0), jnp.maximum(e_t, 0), same & ok_s & ok_t, half))
            blk *= 2
        for e_s, e_t, mask, e_max in terms:
            l_re, l_im = cmul(*pow_row(e_s, e_max), bb_re, bb_im)
            r_re, r_im = cmul(*pow_col(e_t, e_max), ce_re, ce_im)
            term = _dot_split(l_re, r_re) - _dot_split(l_im, r_im)
            acc = acc + jnp.where(mask, term, 0.0)

        if d == 0:
            ws_re, ws_im = cmul(*pow_row(t_chunk - 1 - s16, t_chunk), bb_re, bb_im)
            ca_re, ca_im = cmul(*pow_col(t_i + 1, t_chunk), ce_re, ce_im)
        else:
            ws_re, ws_im = cmul(*pow_row(s16, t_chunk), bb_re, bb_im)
            ca_re, ca_im = cmul(*pow_col(t_chunk - t_i, t_chunk), ce_re, ce_im)
        wu_ref[:, width + d * p:width + (d + 1) * p] = ws_re.astype(wu_ref.dtype)
        wu_ref[:, width + (2 + d) * p:width + (3 + d) * p] = ws_im.astype(wu_ref.dtype)
        wy_ref[d * p:(d + 1) * p, :] = ca_re.astype(wy_ref.dtype)
        wy_ref[(2 + d) * p:(3 + d) * p, :] = (-ca_im).astype(wy_ref.dtype)

    lr2 = row2_ref[0:1, :]
    li2 = row2_ref[1:2, :]
    dt2 = jnp.exp(row2_ref[2:3, :])
    k16 = float(t_chunk)
    mag = jnp.exp(k16 * (lr2 * dt2))
    pr = mag * jnp.cos(k16 * (li2 * dt2))
    pi = mag * jnp.sin(k16 * (li2 * dt2))
    for i in range(n_pw):
        pw_ref[2 * i:2 * i + 1, :] = pr
        pw_ref[2 * i + 1:2 * i + 2, :] = pi
        pr, pi = pr * pr - pi * pi, 2.0 * pr * pi

    wu_ref[:, 0:width] = (acc + jnp.where(r_i == c_i, d_ref[...], 0.0)).astype(wu_ref.dtype)


def _s5prep_call(lam_re, lam_im, log_dt, b_re, b_im, c_re, c_im, d_skip, n_pw):
    n_e, _, n_g, p = lam_re.shape
    n_ch = S5_GROUP_CH
    width = S5_CHUNK * n_ch
    assert 2 * p == 128, "state rows are packed as [fwd | bwd] in one 128-lane tile"
    n_pw_rows = -(-2 * n_pw // 8) * 8
    dtb = jnp.broadcast_to(log_dt[..., None], lam_re.shape)
    zeros = jnp.zeros_like(lam_re)
    rowp = jnp.stack([lam_re, lam_im, dtb] + [zeros] * 5, axis=-2)
    row2 = jnp.concatenate([rowp[:, 0], rowp[:, 1]], axis=-1)
    colp = jnp.stack([lam_re, lam_im, dtb] + [zeros] * 5, axis=-1)
    bt = lambda b: jnp.tile(jnp.swapaxes(b, -1, -2), (1, 1, 1, S5_CHUNK, 1))
    ce = lambda c: jnp.tile(jnp.swapaxes(c, -1, -2), (1, 1, 1, 1, S5_CHUNK))
    d_row = jnp.tile(d_skip.reshape(n_e, n_g, 1, n_ch), (1, 1, 1, S5_CHUNK))

    def dspec(shape):
        return pl.BlockSpec((None, 2, None) + shape, lambda e, g: (e, 0, g, 0, 0))

    def ospec(shape):
        return pl.BlockSpec((None, None) + shape, lambda e, g: (e, g, 0, 0))

    return pl.pallas_call(
        functools.partial(_s5prep_kernel, n_state=p, n_pw=n_pw),
        grid=(n_e, n_g),
        in_specs=[dspec((8, p)), ospec((8, 2 * p)), dspec((p, 8)), dspec((width, p)), dspec((width, p)),
                  dspec((p, width)), dspec((p, width)), ospec((1, width))],
        out_specs=[ospec((width, 2 * width)), ospec((width, width)), ospec((n_pw_rows, 2 * p))],
        out_shape=[
            jax.ShapeDtypeStruct((n_e, n_g, width, 2 * width), BF16),
            jax.ShapeDtypeStruct((n_e, n_g, width, width), BF16),
            jax.ShapeDtypeStruct((n_e, n_g, n_pw_rows, 2 * p), F32),
        ],
        compiler_params=_cparams(2),
        name="s5_prep",
    )(rowp, row2, colp, bt(b_re), bt(b_im), ce(c_re), ce(c_im), d_row)


def _gelu_tanh(y):
    return 0.5 * y * (1.0 + jnp.tanh(0.7978845608028654 * (y + 0.044715 * (y * y * y))))


def _block_transpose8(v, lane):
    for dist in (4, 2, 1):
        width = dist * S5_GROUP_CH
        low = (lane & width) == 0
        out = list(v)
        for a in range(8):
            if a & dist == 0:
                lo, hi = v[a], v[a + dist]
                out[a] = jnp.where(low, lo, pltpu.roll(hi, width, 1))
                out[a + dist] = jnp.where(low, pltpu.roll(lo, LANES - width, 1), hi)
        v = out
    return v


def _s5_kernel(u_ref, wu_ref, wy_ref, pw_ref, h0_ref, g_ref, z_ref, zs_s, ug_s, yg_s,
               *, n_p_tiles, cp, cs):
    tile = pl.program_id(1)
    t_chunk = S5_CHUNK
    n = u_ref.shape[0] // t_chunk
    ngb, width = wy_ref.shape[0], wy_ref.shape[1]
    half = width // 2
    lanes = u_ref.shape[1]

    is_fwd = lax.broadcasted_iota(jnp.int32, (1, half), 1) < half // 2
    is_fwd2 = (lax.broadcasted_iota(jnp.int32, (1, width), 1) & (half - 1)) < half // 2

    lane = lax.broadcasted_iota(jnp.int32, (1, lanes), 1)
    for s_hi in range(t_chunk // 8):
        w = _block_transpose8([u_ref[pl.ds(s_hi * 8 + a, n, stride=t_chunk), :] for a in range(8)], lane)
        for gi in range(ngb):
            ug_s[gi, :, s_hi * lanes:(s_hi + 1) * lanes] = w[gi].astype(ug_s.dtype)

    def reversal(cseq):
        r = lax.broadcasted_iota(jnp.int32, (n, n), 0)
        c = lax.broadcasted_iota(jnp.int32, (n, n), 1)
        lg = int(math.log2(cseq))
        hit = ((r >> lg) == (c >> lg)) & ((r & (cseq - 1)) + (c & (cseq - 1)) == cseq - 1)
        return jnp.where(hit, 1.0, 0.0).astype(BF16)

    def run(cseq, j):
        rev = reversal(cseq)
        for gi in range(ngb):
            ug = ug_s[gi]
            wu = wu_ref[gi]
            ys = _dot(ug, wu)
            ug_rev = _dot(rev, ug).astype(BF16)
            ys_rev = _dot(ug_rev, wu[:, width:])
            yg_s[gi] = ys[:, 0:width]
            zs_s[0, gi] = jnp.where(is_fwd, ys[:, width:width + half], ys_rev[:, 0:half])
            zs_s[1, gi] = jnp.where(is_fwd, ys[:, width + half:], ys_rev[:, half:])

        zr, zi = zs_s[0], zs_s[1]
        pos = lax.broadcasted_iota(jnp.int32, (1, n, 1), 1) & (cseq - 1)
        edge = pos == 0
        if j is not None:
            h0 = h0_ref[:, pl.ds(j, 1), :]
            h0r, h0i = h0[:, :, 0:half], h0[:, :, half:]
            pr, pi = pw_ref[:, 0:1, :], pw_ref[:, 1:2, :]
            zr = zr + jnp.where(edge, pr * h0r - pi * h0i, 0.0)
            zi = zi + jnp.where(edge, pr * h0i + pi * h0r, 0.0)
        k = 1
        i = 0
        while k < cseq:
            ar = pw_ref[:, 2 * i:2 * i + 1, :]
            ai = pw_ref[:, 2 * i + 1:2 * i + 2, :]
            sr, si = pltpu.roll(zr, k, 1), pltpu.roll(zi, k, 1)
            valid = pos >= k
            zr = zr + jnp.where(valid, ar * sr - ai * si, 0.0)
            zi = zi + jnp.where(valid, ar * si + ai * sr, 0.0)
            k *= 2
            i += 1
        if j is None:
            z_ref[...] = jnp.concatenate([zr, zi], axis=2)
        xr, xi = pltpu.roll(zr, 1, 1), pltpu.roll(zi, 1, 1)
        if j is None:
            zs_s[0], zs_s[1] = jnp.where(edge, 0.0, xr), jnp.where(edge, 0.0, xi)
        else:
            zs_s[0], zs_s[1] = jnp.where(edge, h0r, xr), jnp.where(edge, h0i, xi)

        for gi in range(ngb):
            x = jnp.concatenate([zs_s[0, gi], zs_s[1, gi]], axis=1)
            x_rev = _dot(rev, x.astype(BF16))
            x = jnp.where(is_fwd2, x, x_rev).astype(BF16)
            yg_s[gi] = _gelu_tanh(yg_s[gi] + _dot(x, wy_ref[gi]))

    pl.when(tile < n_p_tiles)(lambda: run(cp, None))
    pl.when(tile >= n_p_tiles)(lambda: run(cs, tile - n_p_tiles))

    for t_hi in range(t_chunk // 8):
        w = _block_transpose8([yg_s[gi, :, t_hi * lanes:(t_hi + 1) * lanes] for gi in range(ngb)], lane)
        for a in range(8):
            g_ref[pl.ds(t_hi * 8 + a, n, stride=t_chunk), :] = w[a]


def _s5_call(u, w_u, w_y, pw, h0, *, l_tile, n_p_tiles, cp, cs):
    m, s5w = u.shape
    n_g, width = w_y.shape[0], w_y.shape[1]
    ngb = min(8, n_g)
    lanes = ngb * S5_GROUP_CH
    assert lanes == 128 and ngb == 8 and S5_CHUNK % 8 == 0 and n_g % ngb == 0 and m % l_tile == 0
    n = l_tile // S5_CHUNK
    n_h0 = h0.shape[1]
    tok = pl.BlockSpec((l_tile, lanes), lambda cb, t: (t, cb))
    grp = lambda shape: pl.BlockSpec((ngb,) + shape, lambda cb, t: (cb, 0, 0))
    return pl.pallas_call(
        functools.partial(_s5_kernel, n_p_tiles=n_p_tiles, cp=cp, cs=cs),
        grid=(n_g // ngb, m // l_tile),
        in_specs=[tok, grp((width, 2 * width)), grp((width, width)), grp(pw.shape[1:]), grp((n_h0, width))],
        out_specs=[tok, pl.BlockSpec((ngb, n, width), lambda cb, t: (cb, jnp.minimum(t, n_p_tiles - 1), 0))],
        out_shape=[jax.ShapeDtypeStruct((m, s5w), F32),
                   jax.ShapeDtypeStruct((n_g, n_p_tiles * n, width), F32)],
        scratch_shapes=[pltpu.VMEM((2, ngb, n, width // 2), F32),
                        pltpu.VMEM((ngb, n, width), BF16),
                        pltpu.VMEM((ngb, n, width), F32)],
        compiler_params=_cparams(2),
        name="s5_core",
    )(u, w_u, w_y, pw, h0)


def _ctx_attn_kernel(q_ref, k_ref, v_ref, o_ref, *, scale):
    hd = HEAD_DIM
    for h in range(q_ref.shape[1] // hd):
        sl = slice(h * hd, (h + 1) * hd)
        s = _dot_nt(q_ref[:, sl], k_ref[:, sl]) * scale
        m = jnp.max(s, axis=-1, keepdims=True)
        p = jnp.exp(s - m)
        l = jnp.sum(p, axis=-1, keepdims=True)
        o = _dot(p.astype(BF16), v_ref[:, sl]) / l
        o_ref[:, sl] = o.astype(o_ref.dtype)


def _ctx_attn_call(q, k, v, *, n_seq, seq_len):
    naw = q.shape[1]
    spec = pl.BlockSpec((seq_len, naw), lambda b: (b, 0))
    return pl.pallas_call(
        functools.partial(_ctx_attn_kernel, scale=HEAD_DIM ** -0.5),
        grid=(n_seq,),
        in_specs=[spec, spec, spec],
        out_specs=spec,
        out_shape=jax.ShapeDtypeStruct((n_seq * seq_len, naw), BF16),
        compiler_params=_cparams(1),
        name="ctx_attn",
    )(q, k, v)


def _na_bias_kernel(rpb_ref, o_ref, t_s):
    inv_scale = HEAD_DIM ** 0.5
    h = pl.program_id(0)
    n_dr = 2 * NA_WIN_R - 1
    n_dc = 2 * NA_WIN_C - 1
    w = GRID_W
    qc = lax.broadcasted_iota(jnp.int32, (w, w), 0)
    kc = lax.broadcasted_iota(jnp.int32, (w, w), 1)
    dc = kc - qc + (NA_WIN_C - 1)
    c0 = jnp.clip(qc - NA_WIN_C // 2, 0, w - NA_WIN_C)
    col_ok = (kc >= c0) & (kc < c0 + NA_WIN_C)
    for dr in range(n_dr):
        def pick(j, t, dr=dr):
            return jnp.where(dc == j, rpb_ref[h * (n_dr * n_dc) + dr * n_dc + j] * inv_scale, t)
        t = lax.fori_loop(0, n_dc, pick, jnp.zeros((w, w), F32))
        t_s[dr] = jnp.where(col_ok, t, NEG_INF)
    neg = jnp.full((w, w), NEG_INF, F32)
    patterns = [(0, lambda i: 0), (NA_WIN_R // 2, lambda i: i), (NA_WIN_R, lambda i: NA_QROWS)]
    for pat, (r_rel, r0_rel) in enumerate(patterns):
        for i in range(NA_QROWS):
            for kr in range(NA_KROWS):
                dr = kr - i - r_rel + (NA_WIN_R - 1)
                valid = r0_rel(i) <= kr < r0_rel(i) + NA_WIN_R
                o_ref[pat, i * w:(i + 1) * w, kr * w:(kr + 1) * w] = t_s[dr] if valid else neg


def _na_bias_call(rpb):
    n_h = rpb.shape[0]
    nq = NA_QROWS * GRID_W
    nk = NA_KROWS * GRID_W
    return pl.pallas_call(
        _na_bias_kernel,
        grid=(n_h,),
        in_specs=[pl.BlockSpec(memory_space=pltpu.SMEM)],
        out_specs=pl.BlockSpec((None, 3, nq, nk), lambda h: (h, 0, 0, 0)),
        out_shape=jax.ShapeDtypeStruct((n_h, 3, nq, nk), F32),
        scratch_shapes=[pltpu.VMEM((2 * NA_WIN_R - 1, GRID_W, GRID_W), F32)],
        compiler_params=_cparams(1),
        name="na_bias",
    )(rpb.reshape(-1))


def _na_kernel(q_ref, k_ref, v_ref, kc_ref, vc_ref, bb_ref, o_ref, kcb_s, vcb_s, s_s, p_s, l_s,
               *, rows, scale):
    w = GRID_W
    nq = NA_QROWS * w
    nk = NA_KROWS * w
    n_blk = rows // NA_QROWS
    kcb_s[...] = kc_ref[...].astype(BF16)
    vcb_s[...] = vc_ref[...].astype(BF16)

    def q_rows(blk):
        return pl.ds(pl.multiple_of(jnp.int32(blk) * nq, nq), nq)

    def k_rows(blk):
        k_base = jnp.clip(jnp.int32(blk) * NA_QROWS - NA_WIN_R // 2, 0, rows - NA_KROWS)
        return pl.ds(pl.multiple_of(k_base * w, nq), nk)

    def scores(blk, slot):
        pat = jnp.where(blk == 0, 0, jnp.where(blk == n_blk - 1, 2, 1))
        q = q_ref[q_rows(blk), :]
        s_s[slot, :, 0:nk] = _dot_nt(q, k_ref[k_rows(blk), :]) + bb_ref[pat]
        s_s[slot, :, nk:] = _dot_nt(q, kcb_s[...])

    def softmax(slot):
        s = s_s[slot]
        p = jnp.exp2((s - jnp.max(s, axis=-1, keepdims=True)) * (scale * math.log2(math.e)))
        l_s[slot] = jnp.sum(p, axis=-1, keepdims=True)
        p_s[slot] = p.astype(p_s.dtype)

    def values(blk, slot):
        o = _dot(p_s[slot, :, 0:nk], v_ref[k_rows(blk), :]) + _dot(p_s[slot, :, nk:], vcb_s[...])
        o_ref[q_rows(blk), :] = (o / l_s[slot]).astype(o_ref.dtype)

    scores(0, 0)
    softmax(0)
    scores(1, 1)

    def body(jj, carry):
        values(2 * jj, 0)
        softmax(1)
        scores(2 * jj + 2, 0)
        values(2 * jj + 1, 1)
        softmax(0)
        scores(2 * jj + 3, 1)
        return carry

    lax.fori_loop(0, (n_blk - 2) // 2, body, 0)
    values(n_blk - 2, 0)
    softmax(1)
    values(n_blk - 1, 1)


def _na_call(q, k, v, cache_k, cache_v, bias, *, layer_e, n_b, seq_len, row_blk0):
    naw = q.shape[1]
    n_h = naw // HEAD_DIM
    past = cache_k.shape[2]
    rows = seq_len // GRID_W
    assert rows % (2 * NA_QROWS) == 0 and rows >= NA_KROWS
    nq, nk = NA_QROWS * GRID_W, NA_KROWS * GRID_W
    tok =pl.BlockSpec((seq_len, HEAD_DIM), lambda b, h: (row_blk0 + b, h))
    ctx = pl.BlockSpec((None, None, past, HEAD_DIM), lambda b, h: (b, layer_e, 0, h))
    return pl.pallas_call(
        functools.partial(_na_kernel, rows=rows, scale=HEAD_DIM ** -0.5),
        grid=(n_b, n_h),
        in_specs=[tok, tok, tok, ctx, ctx,
                  pl.BlockSpec((None,) + bias.shape[1:], lambda b, h: (h, 0, 0, 0))],
        out_specs=pl.BlockSpec((seq_len, HEAD_DIM), lambda b, h: (b, h)),
        out_shape=jax.ShapeDtypeStruct((n_b * seq_len, naw), BF16),
        scratch_shapes=[pltpu.VMEM((past, HEAD_DIM), BF16), pltpu.VMEM((past, HEAD_DIM), BF16),
                        pltpu.VMEM((2, nq, nk + past), F32), pltpu.VMEM((2, nq, nk + past), BF16),
                        pltpu.VMEM((2, nq, 1), F32)],
        compiler_params=_cparams(2),
        name="na_attn",
    )(q, k, v, cache_k, cache_v, bias)


def _about_kernel(*refs, n_x, n_p_tiles, nc, rc):
    g_ref, ap_ref, as_ref = refs[:3]
    x_refs = refs[3:3 + n_x]
    mod_ref, gw_ref, gb_ref, w_ref, o_ref = refs[3 + n_x:]
    s5w = g_ref.shape[1]
    tm, d = o_ref.shape

    def run(part):
        a_ref, x_ref = part
        for r0, rn in _row_chunks(tm, rc):
            rows = slice(r0, r0 + rn)
            g = g_ref[rows, :]
            gl = _dot(g.astype(BF16), gw_ref[...]) + gb_ref[...]
            s5o = (g * jax.nn.sigmoid(gl)).astype(BF16)
            a = a_ref[rows, :]
            for n0 in range(0, d, nc):
                y = _dot(s5o, w_ref[0:s5w, n0:n0 + nc]) + _dot(a, w_ref[s5w:, n0:n0 + nc])
                o_ref[rows, n0:n0 + nc] = x_ref[rows, n0:n0 + nc] + mod_ref[2:3, n0:n0 + nc] * y

    _for_part(pl.program_id(0), n_p_tiles, [(ap_ref, x_refs[0]), (as_ref, x_refs[-1])], run)


def _about_call(g, attn_p, attn_s, xs, modt, glu_w, glu_b, w_out, *, layer, tm, n_p_tiles):
    m = sum(x.shape[0] for x in xs)
    d = xs[0].shape[1]
    s5w = g.shape[1]
    naw = attn_p.shape[1]
    nc = min(512, d)
    row = lambda i: (i, 0)
    return pl.pallas_call(
        functools.partial(_about_kernel, n_x=len(xs), n_p_tiles=n_p_tiles, nc=nc, rc=min(ROW_CHUNK, tm)),
        grid=(m // tm,),
        in_specs=[pl.BlockSpec((tm, s5w), row)] + _x_specs(2, tm, naw, n_p_tiles)
        + _x_specs(len(xs), tm, d, n_p_tiles) + [
            pl.BlockSpec((None, 8, d), lambda i: (i, 0, 0)),
            _const_spec((None, s5w, s5w), lambda i: (layer, 0, 0)),
            _const_spec((1, s5w), lambda i: (0, 0)),
            _const_spec((None, s5w + naw, d), lambda i: (layer, 0, 0)),
        ],
        out_specs=pl.BlockSpec((tm, d), row),
        out_shape=jax.ShapeDtypeStruct((m, d), F32),
        compiler_params=_cparams(1),
        name="ab_out_proj",
    )(g, attn_p, attn_s, *xs, modt, glu_w, glu_b, w_out)


def _mlp_kernel(x_ref, mod_ref, g_ref, w1_ref, w2_ref, o_ref, h_s, a_s, *, rc, nc1, nc2):
    tm, d = o_ref.shape
    tf = w1_ref.shape[1]
    ta = a_s.shape[1]

    def chunk_dots(rows):
        h = h_s[rows, :]
        for f0 in range(0, tf, ta):
            for c0 in range(0, ta, nc1):
                a = jnp.maximum(_dot(h, w1_ref[:, f0 + c0:f0 + c0 + nc1]), 0.0)
                a_s[rows, c0:c0 + nc1] = (a * a).astype(a_s.dtype)
            a = a_s[rows, :]
            for n0 in range(0, d, nc2):
                o_ref[rows, n0:n0 + nc2] += mod_ref[5:6, n0:n0 + nc2] * _dot(a, w2_ref[f0:f0 + ta, n0:n0 + nc2])

    @pl.when(pl.program_id(1) == 0)
    def _():
        g, shift, scale = g_ref[...], mod_ref[3:4, :], mod_ref[4:5, :]
        for r0, rn in _row_chunks(tm, rc):
            rows = slice(r0, r0 + rn)
            x = x_ref[rows, :]
            h_s[rows, :] = _modulate(x, g, shift, scale).astype(h_s.dtype)
            o_ref[rows, :] = x
            chunk_dots(rows)

    @pl.when(pl.program_id(1) != 0)
    def _():
        chunk_dots(slice(0, tm))


def _mlp_call(x, modt, g, w1, w2, *, layer, tm, tf, tile0=0, n_tiles=None):
    d = x.shape[1]
    n_tiles = x.shape[0] // tm if n_tiles is None else n_tiles
    m = n_tiles * tm
    d_ff = w1.shape[2]
    rc = min(2 * ROW_CHUNK, tm)
    return pl.pallas_call(
        functools.partial(_mlp_kernel, rc=rc, nc1=min(256, tf), nc2=min(512, d)),
        grid=(n_tiles, d_ff // tf),
        in_specs=[
            pl.BlockSpec((tm, d), lambda i, f: (i + tile0, 0)),
            pl.BlockSpec((None, 8, d), lambda i, f: (i + tile0, 0, 0)),
            _const_spec((1, d), lambda i, f: (0, 0)),
            pl.BlockSpec((None, d, tf), lambda i, f: (layer, 0, f)),
            pl.BlockSpec((None, tf, d), lambda i, f: (layer, f, 0)),
        ],
        out_specs=pl.BlockSpec((tm, d), lambda i, f: (i, 0)),
        out_shape=jax.ShapeDtypeStruct((m, d), F32),
        scratch_shapes=[pltpu.VMEM((tm, d), BF16), pltpu.VMEM((tm, min(512, tf)), BF16)],
        compiler_params=_cparams(2),
        name="mlp",
    )(x, modt, g, w1, w2)


def _convin_kernel(x_ref, mod_ref, g_ref, w_ref, gb_ref, z_ref, h_s, *, rc, nc):
    tm, d = gb_ref.shape
    g, shift, scale = g_ref[...], mod_ref[0:1, :], mod_ref[1:2, :]
    for r0, rn in _row_chunks(tm, rc):
        rows = slice(r0, r0 + rn)
        h_s[rows, :] = _modulate(x_ref[rows, :], g, shift, scale).astype(h_s.dtype)
        h = h_s[rows, :]
        for n0 in range(0, d, nc):
            gb_ref[rows, n0:n0 + nc] = _dot(h, w_ref[:, n0:n0 + nc]).astype(gb_ref.dtype)
            z_ref[rows, n0:n0 + nc] = (_dot(h, w_ref[:, d + n0:d + n0 + nc])
                                       * _dot(h, w_ref[:, 2 * d + n0:2 * d + n0 + nc])).astype(z_ref.dtype)


def _convin_call(x, modt, g, w_in, *, layer, tm):
    m, d = x.shape
    row = lambda i: (i, 0)
    return pl.pallas_call(
        functools.partial(_convin_kernel, rc=min(ROW_CHUNK, tm), nc=min(512, d)),
        grid=(m // tm,),
        in_specs=[
            pl.BlockSpec((tm, d), row),
            pl.BlockSpec((None, 8, d), lambda i: (i, 0, 0)),
            _const_spec((1, d), lambda i: (0, 0)),
            _const_spec((None, d, 3 * d), lambda i: (layer, 0, 0)),
        ],
        out_specs=[pl.BlockSpec((tm, d), row), pl.BlockSpec((tm, d), row)],
        out_shape=[jax.ShapeDtypeStruct((m, d), BF16), jax.ShapeDtypeStruct((m, d), BF16)],
        scratch_shapes=[pltpu.VMEM((tm, d), BF16)],
        compiler_params=_cparams(1),
        name="conv_in_proj",
    )(x, modt, g, w_in)


def _convout_kernel(gb_ref, z_ref, zp_ref, zn_ref, x_ref, mod_ref, cw_ref, cb_ref, w_ref, o_ref, t_s,
                    *, n_p_tiles, lp, ls, rc, cc, nc):
    i = pl.program_id(0)
    tm, d = z_ref.shape
    halo = zp_ref.shape[0]
    seq_mask = jnp.where(i < n_p_tiles, lp - 1, ls - 1)
    chunk_aligned = lp % rc == 0 and ls % rc == 0
    for r0, rn in _row_chunks(tm, rc):
        rows = slice(r0, r0 + rn)
        ridx = lax.broadcasted_iota(jnp.int32, (rn, 1), 0)
        if chunk_aligned:
            starts = ((i * tm + r0) & seq_mask) == 0
            ends = ((i * tm + r0 + rn) & seq_mask) == 0
        else:
            pos = (i * tm + r0 + ridx) & seq_mask
            is_start = pos == 0
            is_end = pos == seq_mask
        for c0 in range(0, d, cc):
            cs = slice(c0, c0 + cc)
            z = z_ref[rows, cs].astype(F32)
            before = (zp_ref[halo - 1:halo, cs] if r0 == 0 else z_ref[r0 - 1:r0, cs]).astype(F32)
            after = (zn_ref[0:1, cs] if r0 + rn == tm else z_ref[r0 + rn:r0 + rn + 1, cs]).astype(F32)
            if chunk_aligned:
                before = jnp.where(starts, 0.0, before)
                after = jnp.where(ends, 0.0, after)
            z_prev = jnp.where(ridx == 0, before, pltpu.roll(z, 1, 0))
            z_next = jnp.where(ridx == rn - 1, after, pltpu.roll(z, rn - 1, 0))
            if not chunk_aligned:
                z_prev = jnp.where(is_start, 0.0, z_prev)
                z_next = jnp.where(is_end, 0.0, z_next)
            conv = z_prev * cw_ref[0:1, cs] + z * cw_ref[1:2, cs] + z_next * cw_ref[2:3, cs] + cb_ref[:, cs]
            t_s[rows, cs] = (gb_ref[rows, cs].astype(F32) * conv).astype(t_s.dtype)
        t = t_s[rows, :]
        for n0 in range(0, d, nc):
            o_ref[rows, n0:n0 + nc] = (x_ref[rows, n0:n0 + nc]
                                       + mod_ref[2:3, n0:n0 + nc] * _dot(t, w_ref[:, n0:n0 + nc]))


def _convout_call(gb, z, x, modt, conv_w, conv_b, w_out, *, layer, tm, n_p_tiles, lp, ls):
    m, d = x.shape
    halo = 16
    assert tm % halo == 0 and lp & (lp - 1) == 0 and ls & (ls - 1) == 0
    hb = tm // halo
    last = m // halo - 1
    row = lambda i: (i, 0)
    return pl.pallas_call(
        functools.partial(_convout_kernel, n_p_tiles=n_p_tiles, lp=lp, ls=ls, rc=min(ROW_CHUNK, tm),
                          cc=min(256, d), nc=min(512, d)),
        grid=(m // tm,),
        in_specs=[
            pl.BlockSpec((tm, d), row),
            pl.BlockSpec((tm, d), row),
            pl.BlockSpec((halo, d), lambda i: (jnp.maximum(i * hb - 1, 0), 0)),
            pl.BlockSpec((halo, d), lambda i: (jnp.minimum((i + 1) * hb, last), 0)),
            pl.BlockSpec((tm, d), row),
            pl.BlockSpec((None, 8, d), lambda i: (i, 0, 0)),
            _const_spec((8, d), lambda i: (0, 0)),
            _const_spec((1, d), lambda i: (0, 0)),
            _const_spec((None, d, d), lambda i: (layer, 0, 0)),
        ],
        out_specs=pl.BlockSpec((tm, d), row),
        out_shape=jax.ShapeDtypeStruct((m, d), F32),
        scratch_shapes=[pltpu.VMEM((tm, d), BF16)],
        compiler_params=_cparams(1),
        name="conv_out_proj",
    )(gb, z, z, z, x, modt, conv_w, conv_b, w_out)


def _tile_rows(m_p, m_s, l_s, n_b, tm):
    assert m_p % tm == 0 and l_s % tm == 0
    return np.concatenate([np.full(m_p // tm, n_b), np.repeat(np.arange(n_b), l_s // tm)])


def _pick_tile(pref, m_p, l_s):
    tm = pref
    while m_p % tm or l_s % tm:
        tm //= 2
    return tm


def kernel(x_prompt, x_sample, c, cache_k, cache_v, state_ssm_re, state_ssm_im, c_ctx, ada_w, ada_b, norm1_g, norm2_g, ab_w_in, ab_w_out, s5_lam_re, s5_lam_im, s5_log_dt, s5_b_re, s5_b_im, s5_c_re, s5_c_im, s5_d, s5_glu_w, s5_glu_b, q_norm_g, k_norm_g, na_rpb, conv_w_in, conv_w, conv_b, conv_w_out, mlp_w1, mlp_w2):
    n_bp, l_p, d = x_prompt.shape
    n_bs, l_s, _ = x_sample.shape
    depth = ada_w.shape[0]
    m_p, m_s = n_bp * l_p, n_bs * l_s
    m = m_p + m_s
    n_g, n_state = s5_lam_re.shape[2], s5_lam_re.shape[3]
    s5w = n_g * S5_GROUP_CH
    naw = (ab_w_in.shape[2] - s5w) // 3
    n_h = naw // HEAD_DIM
    past = cache_k.shape[2]
    d_ff = mlp_w1.shape[2]
    assert m_p % l_s == 0, "latent sequences must start on a sequence-length row block"
    assert l_p % S5_CHUNK == 0 and l_s % S5_CHUNK == 0
    c_p, c_s = l_p // S5_CHUNK, l_s // S5_CHUNK
    assert c_p & (c_p - 1) == 0 and c_s & (c_s - 1) == 0

    xs = [x_prompt.reshape(m_p, d), x_sample.reshape(m_s, d)]

    n_rows = -(-(n_bs + 1) // 16) * 16
    cvec = jnp.concatenate([c, c_ctx[None], jnp.zeros((n_rows - n_bs - 1, d), F32)], axis=0)
    mod = _ada_call(cvec, ada_w, ada_b)

    def mod_tiles(layer, tm):
        t = mod[layer][_tile_rows(m_p, m_s, l_s, n_bs, tm)].reshape(m // tm, 6, d)
        return jnp.pad(t, ((0, 0), (0, 2), (0, 0)))

    tm_big = _pick_tile(1024, m_p, l_s)
    tm_mid = _pick_tile(512, m_p, l_s)
    tf = min(1024, d_ff)

    n_pw = max(int(math.log2(c_s)), 1)
    w_u, w_y, pw = _s5prep_call(s5_lam_re, s5_lam_im, s5_log_dt, s5_b_re, s5_b_im, s5_c_re, s5_c_im,
                                s5_d, n_pw)
    n_h0 = -(-n_bs // 8) * 8
    h0_all = jnp.stack([state_ssm_re, state_ssm_im], axis=2)
    h0_all = h0_all.transpose(1, 4, 0, 2, 3, 5).reshape(-1, n_g, n_bs, 4 * n_state)
    h0_all = jnp.pad(h0_all, ((0, 0), (0, 0), (0, n_h0 - n_bs), (0, 0)))
    cache_k4 = cache_k.reshape(n_bs, -1, past, naw)
    cache_v4 = cache_v.reshape(n_bs, -1, past, naw)
    ab_w_in_b, ab_w_out_b, glu_w_b = ab_w_in.astype(BF16), ab_w_out.astype(BF16), s5_glu_w.astype(BF16)
    conv_w_in_b, conv_w_out_b = conv_w_in.astype(BF16), conv_w_out.astype(BF16)
    mlp_w1_b, mlp_w2_b = mlp_w1.astype(BF16), mlp_w2.astype(BF16)

    caches = None
    new_re, new_im = [], []
    for layer in range(depth):
        g1 = norm1_g[layer].reshape(1, d)
        g2 = norm2_g[layer].reshape(1, d)
        if layer % 2 == 0:
            e = layer // 2
            u, q, k, v, *caches = _inproj_call(
                xs, mod_tiles(layer, tm_mid), g1, ab_w_in_b,
                q_norm_g[e].reshape(1, HEAD_DIM), k_norm_g[e].reshape(1, HEAD_DIM), caches,
                layer=e, n_e=ab_w_in.shape[0], l_p=l_p, tm=tm_mid, s5w=s5w, naw=naw, n_p_tiles=m_p // tm_mid)
            g_tok, z_all = _s5_call(u, w_u[e], w_y[e], pw[e], h0_all[e],
                                    l_tile=l_s, n_p_tiles=m_p // l_s, cp=c_p, cs=c_s)
            z_p = z_all.reshape(n_g, n_bp, c_p, 4 * n_state)
            fin = z_p[:, :, c_p - 1].reshape(n_g, n_bp, 2, 2, n_state)
            new_re.append(fin[:, :, 0].transpose(1, 2, 0, 3))
            new_im.append(fin[:, :, 1].transpose(1, 2, 0, 3))
            attn_p = _ctx_attn_call(q, k, v, n_seq=n_bp, seq_len=l_p)
            bias = _na_bias_call(na_rpb[e])
            attn_s = _na_call(q, k, v, cache_k4, cache_v4, bias, layer_e=e, n_b=n_bs, seq_len=l_s,
                              row_blk0=m_p // l_s)
            x = _about_call(g_tok, attn_p, attn_s, xs, mod_tiles(layer, tm_mid), glu_w_b,
                            s5_glu_b[e].reshape(1, s5w), ab_w_out_b, layer=e, tm=tm_mid,
                            n_p_tiles=m_p // tm_mid)
        else:
            x = xs[0]
            o = layer // 2
            gb, z = _convin_call(x, mod_tiles(layer, tm_mid), g1, conv_w_in_b, layer=o, tm=tm_mid)
            cw8 = jnp.pad(conv_w[o], ((0, 5), (0, 0)))
            x = _convout_call(gb, z, x, mod_tiles(layer, tm_mid), cw8, conv_b[o].reshape(1, d),
                              conv_w_out_b, layer=o, tm=tm_mid, n_p_tiles=m_p // tm_mid,
                              lp=l_p, ls=l_s)
        mlp = functools.partial(_mlp_call, x, mod_tiles(layer, tm_big), g2, mlp_w1_b, mlp_w2_b,
                                layer=layer, tm=tm_big, tf=tf)
        if layer < depth - 1:
            xs = [mlp()]
        else:
            y_prompt = mlp(tile0=0, n_tiles=m_p // tm_big).reshape(n_bp, l_p, d)
            y_sample = mlp(tile0=m_p // tm_big, n_tiles=m_s // tm_big).reshape(n_bs, l_s, d)

    new_k, new_v = (t.reshape(n_bp, -1, l_p, n_h, HEAD_DIM) for t in caches)
    return (y_prompt, y_sample, new_k, new_v, jnp.stack(new_re, axis=1), jnp.stack(new_im, axis=1))
```

```python
import functools
import math

import numpy as np
import jax
import jax.numpy as jnp
from jax import lax
from jax.experimental import pallas as pl
from jax.experimental.pallas import tpu as pltpu

F32 = jnp.float32
BF16 = jnp.bfloat16

NORM_EPS = 1e-6
NEG_INF = -1e30

S5_GROUP_CH = 16
HEAD_DIM = 128
GRID_W = 64
NA_WIN_R = 8
NA_WIN_C = 16
S5_CHUNK = 16
NA_QROWS = 4
NA_KROWS = NA_QROWS + NA_WIN_R
ROW_CHUNK = 256

LANES = 128
VMEM_LIMIT_BYTES = 62 * 1024 * 1024


def _cparams(n_axes):
    return pltpu.CompilerParams(dimension_semantics=("arbitrary",) * n_axes,
                                vmem_limit_bytes=VMEM_LIMIT_BYTES)


def _const_spec(shape, index_map):
    return pl.BlockSpec(shape, index_map, pipeline_mode=pl.Buffered(1))


def _dot(a, b):
    return jnp.dot(a, b, preferred_element_type=F32)


def _dot_nt(a, b):
    return lax.dot_general(a, b, (((1,), (1,)), ((), ())), preferred_element_type=F32)


def _dot_split(a, b):
    a_hi = a.astype(BF16)
    a_lo = (a - a_hi.astype(F32)).astype(BF16)
    b_hi = b.astype(BF16)
    b_lo = (b - b_hi.astype(F32)).astype(BF16)
    return _dot(a_hi, b_hi) + _dot(a_hi, b_lo) + _dot(a_lo, b_hi)


def _modulate(x, g, shift, scale):
    ms = jnp.mean(x * x, axis=-1, keepdims=True)
    y = x * lax.rsqrt(ms + NORM_EPS) * g
    return y * (1.0 + scale) + shift


def _ada_kernel(c_ref, w_ref, b_ref, o_ref, *, nc):
    cv = c_ref[...]
    sc = (cv * jax.nn.sigmoid(cv)).astype(BF16)
    tn = w_ref.shape[1]
    for n0 in range(0, tn, nc):
        w = w_ref[:, n0:n0 + nc].astype(BF16)
        o_ref[:, n0:n0 + nc] = _dot(sc, w) + b_ref[:, n0:n0 + nc]


def _ada_call(cvec, ada_w, ada_b):
    depth, d, n6 = ada_w.shape
    rows = cvec.shape[0]
    tn = 1536 if n6 % 1536 == 0 else n6
    nc = 512 if tn % 512 == 0 else tn
    return pl.pallas_call(
        functools.partial(_ada_kernel, nc=nc),
        grid=(depth, n6 // tn),
        in_specs=[
            pl.BlockSpec((rows, d), lambda l, j: (0, 0)),
            pl.BlockSpec((None, d, tn), lambda l, j: (l, 0, j)),
            pl.BlockSpec((None, 1, tn), lambda l, j: (l, 0, j)),
        ],
        out_specs=pl.BlockSpec((None, rows, tn), lambda l, j: (l, 0, j)),
        out_shape=jax.ShapeDtypeStruct((depth, rows, n6), F32),
        compiler_params=_cparams(2),
        name="ada_params",
    )(cvec, ada_w, ada_b.reshape(depth, 1, n6))


def _x_specs(n_x, tm, d, n_p_tiles):
    if n_x == 1:
        return [pl.BlockSpec((tm, d), lambda i: (i, 0))]
    return [pl.BlockSpec((tm, d), lambda i: (jnp.minimum(i, n_p_tiles - 1), 0)),
            pl.BlockSpec((tm, d), lambda i: (jnp.maximum(i - n_p_tiles, 0), 0))]


def _row_chunks(tm, rc):
    return [(r0, min(rc, tm - r0)) for r0 in range(0, tm, rc)]


def _for_part(i, n_p_tiles, refs, fn):
    if len(refs) == 1:
        fn(refs[0])
    else:
        pl.when(i < n_p_tiles)(lambda: fn(refs[0]))
        pl.when(i >= n_p_tiles)(lambda: fn(refs[1]))


def _inproj_kernel(*refs, n_x, n_alias, n_p_tiles, l_p, s5w, naw, rc):
    x_refs = refs[:n_x]
    mod_ref, g_ref, w_ref, qg_ref, kg_ref = refs[n_x:n_x + 5]
    u_ref, q_ref, k_ref, v_ref, ck_ref, cv_ref, h_s = refs[n_x + 5 + n_alias:]
    i = pl.program_id(0)
    tm = u_ref.shape[0]
    n_heads = naw // HEAD_DIM

    owns_cache = n_alias == 0

    def cache_store(c_ref, r0, col0, val):
        head = col0 // HEAD_DIM
        t0 = r0
        while t0 < r0 + val.shape[0]:
            piece = min(r0 + val.shape[0], (t0 // l_p + 1) * l_p) - t0
            rows = pl.ds((t0 % l_p) * n_heads + head, piece, stride=n_heads)
            if owns_cache:
                c_ref[t0 // l_p, 0, rows, :] = val[t0 - r0:t0 - r0 + piece]
            else:
                c_ref[t0 // l_p, rows, :] = val[t0 - r0:t0 - r0 + piece]
            t0 += piece
    hd = HEAD_DIM
    cw = min(4 * hd, naw)
    g, shift, scale = g_ref[...], mod_ref[0:1, :], mod_ref[1:2, :]
    qg = qg_ref[...]
    kg = kg_ref[...]

    def head_norm(t, gain):
        return t * lax.rsqrt(jnp.mean(t * t, axis=-1, keepdims=True) + NORM_EPS) * gain

    def maybe_cache(prompt, fn):
        if prompt:
            fn()

    def run(x_ref, prompt):
        if prompt and owns_cache and ck_ref.shape[1] > 1:
            for c_ref in (ck_ref, cv_ref):
                c_ref[:, 1:, :, :] = jnp.zeros((c_ref.shape[0], c_ref.shape[1] - 1) + c_ref.shape[2:], c_ref.dtype)
        for r0, rn in _row_chunks(tm, rc):
            rows = slice(r0, r0 + rn)
            h_s[rows, :] = _modulate(x_ref[rows, :], g, shift, scale).astype(h_s.dtype)
            h = h_s[rows, :]
            u_ref[rows, :] = _dot(h, w_ref[:, 0:s5w]).astype(u_ref.dtype)
            for c0 in range(0, naw, cw):
                qc = _dot(h, w_ref[:, s5w + c0:s5w + c0 + cw])
                for j in range(cw // hd):
                    q_ref[rows, c0 + j * hd:c0 + (j + 1) * hd] = head_norm(
                        qc[:, j * hd:(j + 1) * hd], qg).astype(q_ref.dtype)
                kc = _dot(h, w_ref[:, s5w + naw + c0:s5w + naw + c0 + cw])
                for j in range(cw // hd):
                    kh = head_norm(kc[:, j * hd:(j + 1) * hd], kg)
                    k_ref[rows, c0 + j * hd:c0 + (j + 1) * hd] = kh.astype(k_ref.dtype)
                    maybe_cache(prompt, functools.partial(cache_store, ck_ref, r0, c0 + j * hd, kh))
                vc = _dot(h, w_ref[:, s5w + 2 * naw + c0:s5w + 2 * naw + c0 + cw])
                v_ref[rows, c0:c0 + cw] = vc.astype(v_ref.dtype)
                for j in range(cw // hd):
                    maybe_cache(prompt, functools.partial(cache_store, cv_ref, r0, c0 + j * hd,
                                                          vc[:, j * hd:(j + 1) * hd]))

    pl.when(i < n_p_tiles)(lambda: run(x_refs[0], True))
    pl.when(i >= n_p_tiles)(lambda: run(x_refs[-1], False))


def _inproj_call(xs, modt, g, w_in, qg, kg, caches, *, layer, n_e, l_p, tm, s5w, naw, n_p_tiles):
    m = sum(x.shape[0] for x in xs)
    d = xs[0].shape[1]
    n_in = w_in.shape[2]
    rc = min(ROW_CHUNK, tm)
    n_heads = naw // HEAD_DIM
    assert tm % l_p == 0 and (rc % l_p == 0 or l_p % rc == 0)
    seqs = tm // l_p
    cache_shape = jax.ShapeDtypeStruct((n_p_tiles * seqs, n_e, l_p * n_heads, HEAD_DIM), F32)
    n_alias = 0 if caches is None else 2
    if n_alias == 0:
        assert layer == 0
        cache_spec = pl.BlockSpec((seqs, n_e, l_p * n_heads, HEAD_DIM),
                                  lambda i: (jnp.minimum(i, n_p_tiles - 1), 0, 0, 0))
    else:
        cache_spec = pl.BlockSpec((seqs, None, l_p * n_heads, HEAD_DIM),
                                  lambda i: (jnp.minimum(i, n_p_tiles - 1), layer, 0, 0))
    alias_args = [] if caches is None else list(caches)
    n_in_args = len(xs) + 5
    row = lambda i: (i, 0)
    return pl.pallas_call(
        functools.partial(_inproj_kernel, n_x=len(xs), n_alias=n_alias, n_p_tiles=n_p_tiles, l_p=l_p,
                          s5w=s5w, naw=naw, rc=rc),
        grid=(m // tm,),
        in_specs=_x_specs(len(xs), tm, d, n_p_tiles) + [
            pl.BlockSpec((None, 8, d), lambda i: (i, 0, 0)),
            _const_spec((1, d), lambda i: (0, 0)),
            _const_spec((None, d, n_in), lambda i: (layer, 0, 0)),
            _const_spec((1, HEAD_DIM), lambda i: (0, 0)),
            _const_spec((1, HEAD_DIM), lambda i: (0, 0)),
        ] + [pl.BlockSpec(memory_space=pl.ANY)] * n_alias,
        out_specs=[
            pl.BlockSpec((tm, s5w), row),
            pl.BlockSpec((tm, naw), row),
            pl.BlockSpec((tm, naw), row),
            pl.BlockSpec((tm, naw), row),
            cache_spec, cache_spec,
        ],
        out_shape=[
            jax.ShapeDtypeStruct((m, s5w), F32),
            jax.ShapeDtypeStruct((m, naw), BF16),
            jax.ShapeDtypeStruct((m, naw), BF16),
            jax.ShapeDtypeStruct((m, naw), BF16),
            cache_shape, cache_shape,
        ],
        input_output_aliases={n_in_args + a: 4 + a for a in range(n_alias)},
        scratch_shapes=[pltpu.VMEM((tm, d), BF16)],
        compiler_params=_cparams(1),
        name="ab_in_proj",
    )(*xs, modt, g, w_in, qg, kg, *alias_args)


def _s5prep_kernel(row_ref, row2_ref, col_ref, btr_ref, bti_ref, cer_ref, cei_ref, d_ref,
                   wu_ref, wy_ref, pw_ref, *, n_state, n_pw):
    t_chunk = S5_CHUNK
    width = t_chunk * S5_GROUP_CH
    p = n_state
    lg_ch = int(math.log2(S5_GROUP_CH))
    r_i = lax.broadcasted_iota(jnp.int32, (width, 1), 0)
    c_i = lax.broadcasted_iota(jnp.int32, (1, width), 1)
    s_i = r_i >> lg_ch
    t_i = c_i >> lg_ch
    s16 = lax.broadcasted_iota(jnp.int32, (t_chunk, 1), 0)
    n_pow = -(-(t_chunk + 1) // 8) * 8

    def cmul(ar, ai, br, bi):
        return ar * br - ai * bi, ar * bi + ai * br

    acc = jnp.zeros((width, width), F32)
    for d in range(2):
        lr = row_ref[d, 0:1, 0:p]
        li = row_ref[d, 1:2, 0:p]
        dt = jnp.exp(row_ref[d, 2:3, 0:p])
        ar = lr * dt
        ai = li * dt
        er = jnp.exp(ar)
        abr = er * jnp.cos(ai)
        abi = er * jnp.sin(ai)
        den = lr * lr + li * li
        nr = abr - 1.0
        f_re = (nr * lr + abi * li) / den
        f_im = (abi * lr - nr * li) / den
        bb_re, bb_im = cmul(f_re, f_im, btr_ref[d], bti_ref[d])

        lrc = col_ref[d, :, 0:1]
        lic = col_ref[d, :, 1:2]
        dtc = jnp.exp(col_ref[d, :, 2:3])
        arc = lrc * dtc
        aic = lic * dtc
        ce_re = cer_ref[d]
        ce_im = cei_ref[d]

        k_r = lax.broadcasted_iota(jnp.int32, (n_pow, 1), 0).astype(F32)
        k_c = lax.broadcasted_iota(jnp.int32, (1, LANES), 1).astype(F32)
        mag_r = jnp.exp(k_r * ar)
        tr_re, tr_im = mag_r * jnp.cos(k_r * ai), mag_r * jnp.sin(k_r * ai)
        mag_c = jnp.exp(arc * k_c)
        tc_re, tc_im = mag_c * jnp.cos(aic * k_c), mag_c * jnp.sin(aic * k_c)

        def pow_row(e, e_max):
            re = jnp.zeros((t_chunk, p), F32)
            im = jnp.zeros((t_chunk, p), F32)
            for kk in range(e_max + 1):
                hit = e == kk
                re = jnp.where(hit, tr_re[kk:kk + 1, :], re)
                im = jnp.where(hit, tr_im[kk:kk + 1, :], im)
            rep = lambda t: jnp.broadcast_to(t[:, None, :], (t_chunk, S5_GROUP_CH, p)).reshape(width, p)
            return rep(re), rep(im)

        def pow_col(e, e_max):
            re = jnp.zeros((p, width), F32)
            im = jnp.zeros((p, width), F32)
            for kk in range(e_max + 1):
                hit = e == kk
                re = jnp.where(hit, tc_re[:, kk:kk + 1], re)
                im = jnp.where(hit, tc_im[:, kk:kk + 1], im)
            return re, im

        terms = [(jnp.zeros_like(s16), jnp.zeros_like(t_i), s_i == t_i, 0)]
        blk = 2
        while blk <= t_chunk:
            lb = int(math.log2(blk))
            half = blk // 2
            mid = lambda idx: ((idx >> lb) << lb) + half
            same = (s_i >> lb) == (t_i >> lb)
            if d == 0:
                e_s, ok_s = mid(s16) - s16, s_i < mid(s_i)
                e_t, ok_t = t_i - mid(t_i), t_i >= mid(t_i)
            else:
                e_s, ok_s = s16 - mid(s16), s_i >= mid(s_i)
                e_t, ok_t = mid(t_i) - t_i, t_i < mid(t_i)
            terms.append((jnp.maximum(e_s, 0), jnp.maximum(e_t, 0), same & ok_s & ok_t, half))
            blk *= 2
        for e_s, e_t, mask, e_max in terms:
            l_re, l_im = cmul(*pow_row(e_s, e_max), bb_re, bb_im)
            r_re, r_im = cmul(*pow_col(e_t, e_max), ce_re, ce_im)
            term = _dot_split(l_re, r_re) - _dot_split(l_im, r_im)
            acc = acc + jnp.where(mask, term, 0.0)

        if d == 0:
            ws_re, ws_im = cmul(*pow_row(t_chunk - 1 - s16, t_chunk), bb_re, bb_im)
            ca_re, ca_im = cmul(*pow_col(t_i + 1, t_chunk), ce_re, ce_im)
        else:
            ws_re, ws_im = cmul(*pow_row(s16, t_chunk), bb_re, bb_im)
            ca_re, ca_im = cmul(*pow_col(t_chunk - t_i, t_chunk), ce_re, ce_im)
        wu_ref[:, width + d * p:width + (d + 1) * p] = ws_re.astype(wu_ref.dtype)
        wu_ref[:, width + (2 + d) * p:width + (3 + d) * p] = ws_im.astype(wu_ref.dtype)
        wy_ref[d * p:(d + 1) * p, :] = ca_re.astype(wy_ref.dtype)
        wy_ref[(2 + d) * p:(3 + d) * p, :] = (-ca_im).astype(wy_ref.dtype)

    lr2 = row2_ref[0:1, :]
    li2 = row2_ref[1:2, :]
    dt2 = jnp.exp(row2_ref[2:3, :])
    k16 = float(t_chunk)
    mag = jnp.exp(k16 * (lr2 * dt2))
    pr = mag * jnp.cos(k16 * (li2 * dt2))
    pi = mag * jnp.sin(k16 * (li2 * dt2))
    for i in range(n_pw):
        pw_ref[2 * i:2 * i + 1, :] = pr
        pw_ref[2 * i + 1:2 * i + 2, :] = pi
        pr, pi = pr * pr - pi * pi, 2.0 * pr * pi

    wu_ref[:, 0:width] = (acc + jnp.where(r_i == c_i, d_ref[...], 0.0)).astype(wu_ref.dtype)


def _s5prep_call(lam_re, lam_im, log_dt, b_re, b_im, c_re, c_im, d_skip, n_pw):
    n_e, _, n_g, p = lam_re.shape
    n_ch = S5_GROUP_CH
    width = S5_CHUNK * n_ch
    assert 2 * p == 128, "state rows are packed as [fwd | bwd] in one 128-lane tile"
    n_pw_rows = -(-2 * n_pw // 8) * 8
    dtb = jnp.broadcast_to(log_dt[..., None], lam_re.shape)
    zeros = jnp.zeros_like(lam_re)
    rowp = jnp.stack([lam_re, lam_im, dtb] + [zeros] * 5, axis=-2)
    row2 = jnp.concatenate([rowp[:, 0], rowp[:, 1]], axis=-1)
    colp = jnp.stack([lam_re, lam_im, dtb] + [zeros] * 5, axis=-1)
    bt = lambda b: jnp.tile(jnp.swapaxes(b, -1, -2), (1, 1, 1, S5_CHUNK, 1))
    ce = lambda c: jnp.tile(jnp.swapaxes(c, -1, -2), (1, 1, 1, 1, S5_CHUNK))
    d_row = jnp.tile(d_skip.reshape(n_e, n_g, 1, n_ch), (1, 1, 1, S5_CHUNK))

    def dspec(shape):
        return pl.BlockSpec((None, 2, None) + shape, lambda e, g: (e, 0, g, 0, 0))

    def ospec(shape):
        return pl.BlockSpec((None, None) + shape, lambda e, g: (e, g, 0, 0))

    return pl.pallas_call(
        functools.partial(_s5prep_kernel, n_state=p, n_pw=n_pw),
        grid=(n_e, n_g),
        in_specs=[dspec((8, p)), ospec((8, 2 * p)), dspec((p, 8)), dspec((width, p)), dspec((width, p)),
                  dspec((p, width)), dspec((p, width)), ospec((1, width))],
        out_specs=[ospec((width, 2 * width)), ospec((width, width)), ospec((n_pw_rows, 2 * p))],
        out_shape=[
            jax.ShapeDtypeStruct((n_e, n_g, width, 2 * width), BF16),
            jax.ShapeDtypeStruct((n_e, n_g, width, width), BF16),
            jax.ShapeDtypeStruct((n_e, n_g, n_pw_rows, 2 * p), F32),
        ],
        compiler_params=_cparams(2),
        name="s5_prep",
    )(rowp, row2, colp, bt(b_re), bt(b_im), ce(c_re), ce(c_im), d_row)


def _gelu_tanh(y):
    return 0.5 * y * (1.0 + jnp.tanh(0.7978845608028654 * (y + 0.044715 * (y * y * y))))


def _block_transpose8(v, lane):
    for dist in (4, 2, 1):
        width = dist * S5_GROUP_CH
        low = (lane & width) == 0
        out = list(v)
        for a in range(8):
            if a & dist == 0:
                lo, hi = v[a], v[a + dist]
                out[a] = jnp.where(low, lo, pltpu.roll(hi, width, 1))
                out[a + dist] = jnp.where(low, pltpu.roll(lo, LANES - width, 1), hi)
        v = out
    return v


def _s5_kernel(u_ref, wu_ref, wy_ref, pw_ref, h0_ref, g_ref, z_ref, zs_s, ug_s, yg_s,
               *, n_p_tiles, cp, cs):
    tile = pl.program_id(1)
    t_chunk = S5_CHUNK
    n = u_ref.shape[0] // t_chunk
    ngb, width = wy_ref.shape[0], wy_ref.shape[1]
    half = width // 2
    lanes = u_ref.shape[1]

    is_fwd = lax.broadcasted_iota(jnp.int32, (1, half), 1) < half // 2
    is_fwd2 = (lax.broadcasted_iota(jnp.int32, (1, width), 1) & (half - 1)) < half // 2

    lane = lax.broadcasted_iota(jnp.int32, (1, lanes), 1)
    for s_hi in range(t_chunk // 8):
        w = _block_transpose8([u_ref[pl.ds(s_hi * 8 + a, n, stride=t_chunk), :] for a in range(8)], lane)
        for gi in range(ngb):
            ug_s[gi, :, s_hi * lanes:(s_hi + 1) * lanes] = w[gi].astype(ug_s.dtype)

    def reversal(cseq):
        r = lax.broadcasted_iota(jnp.int32, (n, n), 0)
        c = lax.broadcasted_iota(jnp.int32, (n, n), 1)
        lg = int(math.log2(cseq))
        hit = ((r >> lg) == (c >> lg)) & ((r & (cseq - 1)) + (c & (cseq - 1)) == cseq - 1)
        return jnp.where(hit, 1.0, 0.0).astype(BF16)

    def run(cseq, j):
        rev = reversal(cseq)
        for gi in range(ngb):
            ug = ug_s[gi]
            wu = wu_ref[gi]
            ys = _dot(ug, wu)
            ug_rev = _dot(rev, ug).astype(BF16)
            ys_rev = _dot(ug_rev, wu[:, width:])
            yg_s[gi] = ys[:, 0:width]
            zs_s[0, gi] = jnp.where(is_fwd, ys[:, width:width + half], ys_rev[:, 0:half])
            zs_s[1, gi] = jnp.where(is_fwd, ys[:, width + half:], ys_rev[:, half:])

        zr, zi = zs_s[0], zs_s[1]
        pos = lax.broadcasted_iota(jnp.int32, (1, n, 1), 1) & (cseq - 1)
        edge = pos == 0
        if j is not None:
            h0 = h0_ref[:, pl.ds(j, 1), :]
            h0r, h0i = h0[:, :, 0:half], h0[:, :, half:]
            pr, pi = pw_ref[:, 0:1, :], pw_ref[:, 1:2, :]
            zr = zr + jnp.where(edge, pr * h0r - pi * h0i, 0.0)
            zi = zi + jnp.where(edge, pr * h0i + pi * h0r, 0.0)
        k = 1
        i = 0
        while k < cseq:
            ar = pw_ref[:, 2 * i:2 * i + 1, :]
            ai = pw_ref[:, 2 * i + 1:2 * i + 2, :]
            sr, si = pltpu.roll(zr, k, 1), pltpu.roll(zi, k, 1)
            valid = pos >= k
            zr = zr + jnp.where(valid, ar * sr - ai * si, 0.0)
            zi = zi + jnp.where(valid, ar * si + ai * sr, 0.0)
            k *= 2
            i += 1
        if j is None:
            z_ref[...] = jnp.concatenate([zr, zi], axis=2)
        xr, xi = pltpu.roll(zr, 1, 1), pltpu.roll(zi, 1, 1)
        if j is None:
            zs_s[0], zs_s[1] = jnp.where(edge, 0.0, xr), jnp.where(edge, 0.0, xi)
        else:
            zs_s[0], zs_s[1] = jnp.where(edge, h0r, xr), jnp.where(edge, h0i, xi)

        for gi in range(ngb):
            x = jnp.concatenate([zs_s[0, gi], zs_s[1, gi]], axis=1)
            x_rev = _dot(rev, x.astype(BF16))
            x = jnp.where(is_fwd2, x, x_rev).astype(BF16)
            yg_s[gi] = _gelu_tanh(yg_s[gi] + _dot(x, wy_ref[gi]))

    pl.when(tile < n_p_tiles)(lambda: run(cp, None))
    pl.when(tile >= n_p_tiles)(lambda: run(cs, tile - n_p_tiles))

    for t_hi in range(t_chunk // 8):
        w = _block_transpose8([yg_s[gi, :, t_hi * lanes:(t_hi + 1) * lanes] for gi in range(ngb)], lane)
        for a in range(8):
            g_ref[pl.ds(t_hi * 8 + a, n, stride=t_chunk), :] = w[a]


def _s5_call(u, w_u, w_y, pw, h0, *, l_tile, n_p_tiles, cp, cs):
    m, s5w = u.shape
    n_g, width = w_y.shape[0], w_y.shape[1]
    ngb = min(8, n_g)
    lanes = ngb * S5_GROUP_CH
    assert lanes == 128 and ngb == 8 and S5_CHUNK % 8 == 0 and n_g % ngb == 0 and m % l_tile == 0
    n = l_tile // S5_CHUNK
    n_h0 = h0.shape[1]
    tok = pl.BlockSpec((l_tile, lanes), lambda cb, t: (t, cb))
    grp = lambda shape: pl.BlockSpec((ngb,) + shape, lambda cb, t: (cb, 0, 0))
    return pl.pallas_call(
        functools.partial(_s5_kernel, n_p_tiles=n_p_tiles, cp=cp, cs=cs),
        grid=(n_g // ngb, m // l_tile),
        in_specs=[tok, grp((width, 2 * width)), grp((width, width)), grp(pw.shape[1:]), grp((n_h0, width))],
        out_specs=[tok, pl.BlockSpec((ngb, n, width), lambda cb, t: (cb, jnp.minimum(t, n_p_tiles - 1), 0))],
        out_shape=[jax.ShapeDtypeStruct((m, s5w), F32),
                   jax.ShapeDtypeStruct((n_g, n_p_tiles * n, width), F32)],
        scratch_shapes=[pltpu.VMEM((2, ngb, n, width // 2), F32),
                        pltpu.VMEM((ngb, n, width), BF16),
                        pltpu.VMEM((ngb, n, width), F32)],
        compiler_params=_cparams(2),
        name="s5_core",
    )(u, w_u, w_y, pw, h0)


def _ctx_attn_kernel(q_ref, k_ref, v_ref, o_ref, *, scale):
    hd = HEAD_DIM
    for h in range(q_ref.shape[1] // hd):
        sl = slice(h * hd, (h + 1) * hd)
        s = _dot_nt(q_ref[:, sl], k_ref[:, sl]) * scale
        m = jnp.max(s, axis=-1, keepdims=True)
        p = jnp.exp(s - m)
        l = jnp.sum(p, axis=-1, keepdims=True)
        o = _dot(p.astype(BF16), v_ref[:, sl]) / l
        o_ref[:, sl] = o.astype(o_ref.dtype)


def _ctx_attn_call(q, k, v, *, n_seq, seq_len):
    naw = q.shape[1]
    spec = pl.BlockSpec((seq_len, naw), lambda b: (b, 0))
    return pl.pallas_call(
        functools.partial(_ctx_attn_kernel, scale=HEAD_DIM ** -0.5),
        grid=(n_seq,),
        in_specs=[spec, spec, spec],
        out_specs=spec,
        out_shape=jax.ShapeDtypeStruct((n_seq * seq_len, naw), BF16),
        compiler_params=_cparams(1),
        name="ctx_attn",
    )(q, k, v)


def _na_bias_kernel(rpb_ref, o_ref, t_s):
    inv_scale = HEAD_DIM ** 0.5
    h = pl.program_id(0)
    n_dr = 2 * NA_WIN_R - 1
    n_dc = 2 * NA_WIN_C - 1
    w = GRID_W
    qc = lax.broadcasted_iota(jnp.int32, (w, w), 0)
    kc = lax.broadcasted_iota(jnp.int32, (w, w), 1)
    dc = kc - qc + (NA_WIN_C - 1)
    c0 = jnp.clip(qc - NA_WIN_C // 2, 0, w - NA_WIN_C)
    col_ok = (kc >= c0) & (kc < c0 + NA_WIN_C)
    for dr in range(n_dr):
        def pick(j, t, dr=dr):
            return jnp.where(dc == j, rpb_ref[h * (n_dr * n_dc) + dr * n_dc + j] * inv_scale, t)
        t = lax.fori_loop(0, n_dc, pick, jnp.zeros((w, w), F32))
        t_s[dr] = jnp.where(col_ok, t, NEG_INF)
    neg = jnp.full((w, w), NEG_INF, F32)
    patterns = [(0, lambda i: 0), (NA_WIN_R // 2, lambda i: i), (NA_WIN_R, lambda i: NA_QROWS)]
    for pat, (r_rel, r0_rel) in enumerate(patterns):
        for i in range(NA_QROWS):
            for kr in range(NA_KROWS):
                dr = kr - i - r_rel + (NA_WIN_R - 1)
                valid = r0_rel(i) <= kr < r0_rel(i) + NA_WIN_R
                o_ref[pat, i * w:(i + 1) * w, kr * w:(kr + 1) * w] = t_s[dr] if valid else neg


def _na_bias_call(rpb):
    n_h = rpb.shape[0]
    nq = NA_QROWS * GRID_W
    nk = NA_KROWS * GRID_W
    return pl.pallas_call(
        _na_bias_kernel,
        grid=(n_h,),
        in_specs=[pl.BlockSpec(memory_space=pltpu.SMEM)],
        out_specs=pl.BlockSpec((None, 3, nq, nk), lambda h: (h, 0, 0, 0)),
        out_shape=jax.ShapeDtypeStruct((n_h, 3, nq, nk), F32),
        scratch_shapes=[pltpu.VMEM((2 * NA_WIN_R - 1, GRID_W, GRID_W), F32)],
        compiler_params=_cparams(1),
        name="na_bias",
    )(rpb.reshape(-1))


def _na_kernel(q_ref, k_ref, v_ref, kc_ref, vc_ref, bb_ref, o_ref, kcb_s, vcb_s, s_s, p_s, l_s,
               *, rows, scale):
    w = GRID_W
    nq = NA_QROWS * w
    nk = NA_KROWS * w
    n_blk = rows // NA_QROWS
    kcb_s[...] = kc_ref[...].astype(BF16)
    vcb_s[...] = vc_ref[...].astype(BF16)

    def q_rows(blk):
        return pl.ds(pl.multiple_of(jnp.int32(blk) * nq, nq), nq)

    def k_rows(blk):
        k_base = jnp.clip(jnp.int32(blk) * NA_QROWS - NA_WIN_R // 2, 0, rows - NA_KROWS)
        return pl.ds(pl.multiple_of(k_base * w, nq), nk)

    def scores(blk, slot):
        pat = jnp.where(blk == 0, 0, jnp.where(blk == n_blk - 1, 2, 1))
        q = q_ref[q_rows(blk), :]
        s_s[slot, :, 0:nk] = _dot_nt(q, k_ref[k_rows(blk), :]) + bb_ref[pat]
        s_s[slot, :, nk:] = _dot_nt(q, kcb_s[...])

    def softmax(slot):
        s = s_s[slot]
        p = jnp.exp2((s - jnp.max(s, axis=-1, keepdims=True)) * (scale * math.log2(math.e)))
        l_s[slot] = jnp.sum(p, axis=-1, keepdims=True)
        p_s[slot] = p.astype(p_s.dtype)

    def values(blk, slot):
        o = _dot(p_s[slot, :, 0:nk], v_ref[k_rows(blk), :]) + _dot(p_s[slot, :, nk:], vcb_s[...])
        o_ref[q_rows(blk), :] = (o / l_s[slot]).astype(o_ref.dtype)

    scores(0, 0)
    softmax(0)
    scores(1, 1)

    def body(jj, carry):
        values(2 * jj, 0)
        softmax(1)
        scores(2 * jj + 2, 0)
        values(2 * jj + 1, 1)
        softmax(0)
        scores(2 * jj + 3, 1)
        return carry

    lax.fori_loop(0, (n_blk - 2) // 2, body, 0)
    values(n_blk - 2, 0)
    softmax(1)
    values(n_blk - 1, 1)


def _na_call(q, k, v, cache_k, cache_v, bias, *, layer_e, n_b, seq_len, row_blk0):
    naw = q.shape[1]
    n_h = naw // HEAD_DIM
    past = cache_k.shape[2]
    rows = seq_len // GRID_W
    assert rows % (2 * NA_QROWS) == 0 and rows >= NA_KROWS
    nq, nk = NA_QROWS * GRID_W, NA_KROWS * GRID_W
    tok = pl.BlockSpec((seq_len, HEAD_DIM), lambda h, b: (row_blk0 + b, h))
    ctx = pl.BlockSpec((None, None, past, HEAD_DIM), lambda h, b: (b, layer_e, 0, h))
    return pl.pallas_call(
        functools.partial(_na_kernel, rows=rows, scale=HEAD_DIM ** -0.5),
        grid=(n_h, n_b),
        in_specs=[tok, tok, tok, ctx, ctx,
                  pl.BlockSpec((None,) + bias.shape[1:], lambda h, b: (h, 0, 0, 0))],
        out_specs=pl.BlockSpec((seq_len, HEAD_DIM), lambda h, b: (b, h)),
        out_shape=jax.ShapeDtypeStruct((n_b * seq_len, naw), BF16),
        scratch_shapes=[pltpu.VMEM((past, HEAD_DIM), BF16), pltpu.VMEM((past, HEAD_DIM), BF16),
                        pltpu.VMEM((2, nq, nk + past), F32), pltpu.VMEM((2, nq, nk + past), BF16),
                        pltpu.VMEM((2, nq, 1), F32)],
        compiler_params=_cparams(2),
        name="na_attn",
    )(q, k, v, cache_k, cache_v, bias)


def _about_kernel(*refs, n_x, n_p_tiles, nc, rc):
    g_ref, ap_ref, as_ref = refs[:3]
    x_refs = refs[3:3 + n_x]
    mod_ref, gw_ref, gb_ref, w_ref, o_ref = refs[3 + n_x:]
    s5w = g_ref.shape[1]
    tm, d = o_ref.shape

    def run(part):
        a_ref, x_ref = part
        for r0, rn in _row_chunks(tm, rc):
            rows = slice(r0, r0 + rn)
            g = g_ref[rows, :]
            gl = _dot(g.astype(BF16), gw_ref[...]) + gb_ref[...]
            s5o = (g * jax.nn.sigmoid(gl)).astype(BF16)
            a = a_ref[rows, :]
            for n0 in range(0, d, nc):
                y = _dot(s5o, w_ref[0:s5w, n0:n0 + nc]) + _dot(a, w_ref[s5w:, n0:n0 + nc])
                o_ref[rows, n0:n0 + nc] = x_ref[rows, n0:n0 + nc] + mod_ref[2:3, n0:n0 + nc] * y

    _for_part(pl.program_id(0), n_p_tiles, [(ap_ref, x_refs[0]), (as_ref, x_refs[-1])], run)


def _about_call(g, attn_p, attn_s, xs, modt, glu_w, glu_b, w_out, *, layer, tm, n_p_tiles):
    m = sum(x.shape[0] for x in xs)
    d = xs[0].shape[1]
    s5w = g.shape[1]
    naw = attn_p.shape[1]
    nc = min(512, d)
    row = lambda i: (i, 0)
    return pl.pallas_call(
        functools.partial(_about_kernel, n_x=len(xs), n_p_tiles=n_p_tiles, nc=nc, rc=min(ROW_CHUNK, tm)),
        grid=(m // tm,),
        in_specs=[pl.BlockSpec((tm, s5w), row)] + _x_specs(2, tm, naw, n_p_tiles)
        + _x_specs(len(xs), tm, d, n_p_tiles) + [
            pl.BlockSpec((None, 8, d), lambda i: (i, 0, 0)),
            _const_spec((None, s5w, s5w), lambda i: (layer, 0, 0)),
            _const_spec((1, s5w), lambda i: (0, 0)),
            _const_spec((None, s5w + naw, d), lambda i: (layer, 0, 0)),
        ],
        out_specs=pl.BlockSpec((tm, d), row),
        out_shape=jax.ShapeDtypeStruct((m, d), F32),
        compiler_params=_cparams(1),
        name="ab_out_proj",
    )(g, attn_p, attn_s, *xs, modt, glu_w, glu_b, w_out)


def _mlp_kernel(x_ref, mod_ref, g_ref, w1_ref, w2_ref, o_ref, h_s, a_s, *, rc, nc1, nc2):
    tm, d = o_ref.shape
    tf = w1_ref.shape[1]
    ta = a_s.shape[1]

    def chunk_dots(rows):
        h = h_s[rows, :]
        for f0 in range(0, tf, ta):
            for c0 in range(0, ta, nc1):
                a = jnp.maximum(_dot(h, w1_ref[:, f0 + c0:f0 + c0 + nc1]), 0.0)
                a_s[rows, c0:c0 + nc1] = (a * a).astype(a_s.dtype)
            a = a_s[rows, :]
            for n0 in range(0, d, nc2):
                o_ref[rows, n0:n0 + nc2] += mod_ref[5:6, n0:n0 + nc2] * _dot(a, w2_ref[f0:f0 + ta, n0:n0 + nc2])

    @pl.when(pl.program_id(1) == 0)
    def _():
        g, shift, scale = g_ref[...], mod_ref[3:4, :], mod_ref[4:5, :]
        for r0, rn in _row_chunks(tm, rc):
            rows = slice(r0, r0 + rn)
            x = x_ref[rows, :]
            h_s[rows, :] = _modulate(x, g, shift, scale).astype(h_s.dtype)
            o_ref[rows, :] = x
            chunk_dots(rows)

    @pl.when(pl.program_id(1) != 0)
    def _():
        chunk_dots(slice(0, tm))


def _mlp_call(x, modt, g, w1, w2, *, layer, tm, tf, tile0=0, n_tiles=None):
    d = x.shape[1]
    n_tiles = x.shape[0] // tm if n_tiles is None else n_tiles
    m = n_tiles * tm
    d_ff = w1.shape[2]
    rc = min(2 * ROW_CHUNK, tm)
    return pl.pallas_call(
        functools.partial(_mlp_kernel, rc=rc, nc1=min(256, tf), nc2=min(512, d)),
        grid=(n_tiles, d_ff // tf),
        in_specs=[
            pl.BlockSpec((tm, d), lambda i, f: (i + tile0, 0)),
            pl.BlockSpec((None, 8, d), lambda i, f: (i + tile0, 0, 0)),
            _const_spec((1, d), lambda i, f: (0, 0)),
            pl.BlockSpec((None, d, tf), lambda i, f: (layer, 0, f)),
            pl.BlockSpec((None, tf, d), lambda i, f: (layer, f, 0)),
        ],
        out_specs=pl.BlockSpec((tm, d), lambda i, f: (i, 0)),
        out_shape=jax.ShapeDtypeStruct((m, d), F32),
        scratch_shapes=[pltpu.VMEM((tm, d), BF16), pltpu.VMEM((tm, min(512, tf)), BF16)],
        compiler_params=_cparams(2),
        name="mlp",
    )(x, modt, g, w1, w2)


def _convin_kernel(x_ref, mod_ref, g_ref, w_ref, gb_ref, z_ref, h_s, *, rc, nc):
    tm, d = gb_ref.shape
    g, shift, scale = g_ref[...], mod_ref[0:1, :], mod_ref[1:2, :]
    for r0, rn in _row_chunks(tm, rc):
        rows = slice(r0, r0 + rn)
        h_s[rows, :] = _modulate(x_ref[rows, :], g, shift, scale).astype(h_s.dtype)
        h = h_s[rows, :]
        for n0 in range(0, d, nc):
            gb_ref[rows, n0:n0 + nc] = _dot(h, w_ref[:, n0:n0 + nc]).astype(gb_ref.dtype)
            z_ref[rows, n0:n0 + nc] = (_dot(h, w_ref[:, d + n0:d + n0 + nc])
                                       * _dot(h, w_ref[:, 2 * d + n0:2 * d + n0 + nc])).astype(z_ref.dtype)


def _convin_call(x, modt, g, w_in, *, layer, tm):
    m, d = x.shape
    row = lambda i: (i, 0)
    return pl.pallas_call(
        functools.partial(_convin_kernel, rc=min(ROW_CHUNK, tm), nc=min(512, d)),
        grid=(m // tm,),
        in_specs=[
            pl.BlockSpec((tm, d), row),
            pl.BlockSpec((None, 8, d), lambda i: (i, 0, 0)),
            _const_spec((1, d), lambda i: (0, 0)),
            _const_spec((None, d, 3 * d), lambda i: (layer, 0, 0)),
        ],
        out_specs=[pl.BlockSpec((tm, d), row), pl.BlockSpec((tm, d), row)],
        out_shape=[jax.ShapeDtypeStruct((m, d), BF16), jax.ShapeDtypeStruct((m, d), BF16)],
        scratch_shapes=[pltpu.VMEM((tm, d), BF16)],
        compiler_params=_cparams(1),
        name="conv_in_proj",
    )(x, modt, g, w_in)


def _convout_kernel(gb_ref, z_ref, zp_ref, zn_ref, x_ref, mod_ref, cw_ref, cb_ref, w_ref, o_ref, t_s,
                    *, n_p_tiles, lp, ls, rc, cc, nc):
    i = pl.program_id(0)
    tm, d = z_ref.shape
    halo = zp_ref.shape[0]
    seq_mask = jnp.where(i < n_p_tiles, lp - 1, ls - 1)
    chunk_aligned = lp % rc == 0 and ls % rc == 0
    for r0, rn in _row_chunks(tm, rc):
        rows = slice(r0, r0 + rn)
        ridx = lax.broadcasted_iota(jnp.int32, (rn, 1), 0)
        if chunk_aligned:
            starts = ((i * tm + r0) & seq_mask) == 0
            ends = ((i * tm + r0 + rn) & seq_mask) == 0
        else:
            pos = (i * tm + r0 + ridx) & seq_mask
            is_start = pos == 0
            is_end = pos == seq_mask
        for c0 in range(0, d, cc):
            cs = slice(c0, c0 + cc)
            z = z_ref[rows, cs].astype(F32)
            before = (zp_ref[halo - 1:halo, cs] if r0 == 0 else z_ref[r0 - 1:r0, cs]).astype(F32)
            after = (zn_ref[0:1, cs] if r0 + rn == tm else z_ref[r0 + rn:r0 + rn + 1, cs]).astype(F32)
            if chunk_aligned:
                before = jnp.where(starts, 0.0, before)
                after = jnp.where(ends, 0.0, after)
            z_prev = jnp.where(ridx == 0, before, pltpu.roll(z, 1, 0))
            z_next = jnp.where(ridx == rn - 1, after, pltpu.roll(z, rn - 1, 0))
            if not chunk_aligned:
                z_prev = jnp.where(is_start, 0.0, z_prev)
                z_next = jnp.where(is_end, 0.0, z_next)
            conv = z_prev * cw_ref[0:1, cs] + z * cw_ref[1:2, cs] + z_next * cw_ref[2:3, cs] + cb_ref[:, cs]
            t_s[rows, cs] = (gb_ref[rows, cs].astype(F32) * conv).astype(t_s.dtype)
        t = t_s[rows, :]
        for n0 in range(0, d, nc):
            o_ref[rows, n0:n0 + nc] = (x_ref[rows, n0:n0 + nc]
                                       + mod_ref[2:3, n0:n0 + nc] * _dot(t, w_ref[:, n0:n0 + nc]))


def _convout_call(gb, z, x, modt, conv_w, conv_b, w_out, *, layer, tm, n_p_tiles, lp, ls):
    m, d = x.shape
    halo = 16
    assert tm % halo == 0 and lp & (lp - 1) == 0 and ls & (ls - 1) == 0
    hb = tm // halo
    last = m // halo - 1
    row = lambda i: (i, 0)
    return pl.pallas_call(
        functools.partial(_convout_kernel, n_p_tiles=n_p_tiles, lp=lp, ls=ls, rc=min(ROW_CHUNK, tm),
                          cc=min(256, d), nc=min(512, d)),
        grid=(m // tm,),
        in_specs=[
            pl.BlockSpec((tm, d), row),
            pl.BlockSpec((tm, d), row),
            pl.BlockSpec((halo, d), lambda i: (jnp.maximum(i * hb - 1, 0), 0)),
            pl.BlockSpec((halo, d), lambda i: (jnp.minimum((i + 1) * hb, last), 0)),
            pl.BlockSpec((tm, d), row),
            pl.BlockSpec((None, 8, d), lambda i: (i, 0, 0)),
            _const_spec((8, d), lambda i: (0, 0)),
            _const_spec((1, d), lambda i: (0, 0)),
            _const_spec((None, d, d), lambda i: (layer, 0, 0)),
        ],
        out_specs=pl.BlockSpec((tm, d), row),
        out_shape=jax.ShapeDtypeStruct((m, d), F32),
        scratch_shapes=[pltpu.VMEM((tm, d), BF16)],
        compiler_params=_cparams(1),
        name="conv_out_proj",
    )(gb, z, z, z, x, modt, conv_w, conv_b, w_out)


def _tile_rows(m_p, m_s, l_s, n_b, tm):
    assert m_p % tm == 0 and l_s % tm == 0
    return np.concatenate([np.full(m_p // tm, n_b), np.repeat(np.arange(n_b), l_s // tm)])


def _pick_tile(pref, m_p, l_s):
    tm = pref
    while m_p % tm or l_s % tm:
        tm //= 2
    return tm


def kernel(x_prompt, x_sample, c, cache_k, cache_v, state_ssm_re, state_ssm_im, c_ctx, ada_w, ada_b, norm1_g, norm2_g, ab_w_in, ab_w_out, s5_lam_re, s5_lam_im, s5_log_dt, s5_b_re, s5_b_im, s5_c_re, s5_c_im, s5_d, s5_glu_w, s5_glu_b, q_norm_g, k_norm_g, na_rpb, conv_w_in, conv_w, conv_b, conv_w_out, mlp_w1, mlp_w2):
    n_bp, l_p, d = x_prompt.shape
    n_bs, l_s, _ = x_sample.shape
    depth = ada_w.shape[0]
    m_p, m_s = n_bp * l_p, n_bs * l_s
    m = m_p + m_s
    n_g, n_state = s5_lam_re.shape[2], s5_lam_re.shape[3]
    s5w = n_g * S5_GROUP_CH
    naw = (ab_w_in.shape[2] - s5w) // 3
    n_h = naw // HEAD_DIM
    past = cache_k.shape[2]
    d_ff = mlp_w1.shape[2]
    assert m_p % l_s == 0, "latent sequences must start on a sequence-length row block"
    assert l_p % S5_CHUNK == 0 and l_s % S5_CHUNK == 0
    c_p, c_s = l_p // S5_CHUNK, l_s // S5_CHUNK
    assert c_p & (c_p - 1) == 0 and c_s & (c_s - 1) == 0

    xs = [x_prompt.reshape(m_p, d), x_sample.reshape(m_s, d)]

    n_rows = -(-(n_bs + 1) // 16) * 16
    cvec = jnp.concatenate([c, c_ctx[None], jnp.zeros((n_rows - n_bs - 1, d), F32)], axis=0)
    mod = _ada_call(cvec, ada_w, ada_b)

    def mod_tiles(layer, tm):
        t = mod[layer][_tile_rows(m_p, m_s, l_s, n_bs, tm)].reshape(m // tm, 6, d)
        return jnp.pad(t, ((0, 0), (0, 2), (0, 0)))

    tm_big = _pick_tile(1024, m_p, l_s)
    tm_mid = _pick_tile(512, m_p, l_s)
    tf = min(1024, d_ff)

    n_pw = max(int(math.log2(c_s)), 1)
    w_u, w_y, pw = _s5prep_call(s5_lam_re, s5_lam_im, s5_log_dt, s5_b_re, s5_b_im, s5_c_re, s5_c_im,
                                s5_d, n_pw)
    n_h0 = -(-n_bs // 8) * 8
    h0_all = jnp.stack([state_ssm_re, state_ssm_im], axis=2)
    h0_all = h0_all.transpose(1, 4, 0, 2, 3, 5).reshape(-1, n_g, n_bs, 4 * n_state)
    h0_all = jnp.pad(h0_all, ((0, 0), (0, 0), (0, n_h0 - n_bs), (0, 0)))
    cache_k4 = cache_k.reshape(n_bs, -1, past, naw)
    cache_v4 = cache_v.reshape(n_bs, -1, past, naw)
    ab_w_in_b, ab_w_out_b, glu_w_b = ab_w_in.astype(BF16), ab_w_out.astype(BF16), s5_glu_w.astype(BF16)
    conv_w_in_b, conv_w_out_b = conv_w_in.astype(BF16), conv_w_out.astype(BF16)
    mlp_w1_b, mlp_w2_b = mlp_w1.astype(BF16), mlp_w2.astype(BF16)

    caches = None
    new_re, new_im = [], []
    for layer in range(depth):
        g1 = norm1_g[layer].reshape(1, d)
        g2 = norm2_g[layer].reshape(1, d)
        if layer % 2 == 0:
            e = layer // 2
            u, q, k, v, *caches = _inproj_call(
                xs, mod_tiles(layer, tm_mid), g1, ab_w_in_b,
                q_norm_g[e].reshape(1, HEAD_DIM), k_norm_g[e].reshape(1, HEAD_DIM), caches,
                layer=e, n_e=ab_w_in.shape[0], l_p=l_p, tm=tm_mid, s5w=s5w, naw=naw, n_p_tiles=m_p // tm_mid)
            g_tok, z_all = _s5_call(u, w_u[e], w_y[e], pw[e], h0_all[e],
                                    l_tile=l_s, n_p_tiles=m_p // l_s, cp=c_p, cs=c_s)
            z_p = z_all.reshape(n_g, n_bp, c_p, 4 * n_state)
            fin = z_p[:, :, c_p - 1].reshape(n_g, n_bp, 2, 2, n_state)
            new_re.append(fin[:, :, 0].transpose(1, 2, 0, 3))
            new_im.append(fin[:, :, 1].transpose(1, 2, 0, 3))
            attn_p = _ctx_attn_call(q, k, v, n_seq=n_bp, seq_len=l_p)
            bias = _na_bias_call(na_rpb[e])
            attn_s = _na_call(q, k, v, cache_k4, cache_v4, bias, layer_e=e, n_b=n_bs, seq_len=l_s,
                              row_blk0=m_p // l_s)
            x = _about_call(g_tok, attn_p, attn_s, xs, mod_tiles(layer, tm_mid), glu_w_b,
                            s5_glu_b[e].reshape(1, s5w), ab_w_out_b, layer=e, tm=tm_mid,
                            n_p_tiles=m_p // tm_mid)
        else:
            x = xs[0]
            o = layer // 2
            gb, z = _convin_call(x, mod_tiles(layer, tm_mid), g1, conv_w_in_b, layer=o, tm=tm_mid)
            cw8 = jnp.pad(conv_w[o], ((0, 5), (0, 0)))
            x = _convout_call(gb, z, x, mod_tiles(layer, tm_mid), cw8, conv_b[o].reshape(1, d),
                              conv_w_out_b, layer=o, tm=tm_mid, n_p_tiles=m_p // tm_mid,
                              lp=l_p, ls=l_s)
        mlp = functools.partial(_mlp_call, x, mod_tiles(layer, tm_big), g2, mlp_w1_b, mlp_w2_b,
                                layer=layer, tm=tm_big, tf=tf)
        if layer < depth - 1:
            xs = [mlp()]
        else:
            y_prompt = mlp(tile0=0, n_tiles=m_p // tm_big).reshape(n_bp, l_p, d)
            y_sample = mlp(tile0=m_p // tm_big, n_tiles=m_s // tm_big).reshape(n_bs, l_s, d)

    new_k, new_v = (t.reshape(n_bp, -1, l_p, n_h, HEAD_DIM) for t in caches)
    return (y_prompt, y_sample, new_k, new_v, jnp.stack(new_re, axis=1), jnp.stack(new_im, axis=1))
```
